```python
import math
import jax, jax.numpy as jnp
from jax import lax
import numpy as np

D_MODEL = 2048
BATCH = 1
SEQ = 8192
DEPTH = 1

A_GROUPS = 16
A_GROUP_DIM = 128
A_WIDTH = A_GROUPS * A_GROUP_DIM
A_CHUNK = 128
B_HEADS = 16
B_HEAD_DIM = 128
B_WIDTH = B_HEADS * B_HEAD_DIM
MOBA_BLOCK = 256
MOBA_TOPK = 3
Q_CHUNK = 32
REL_BUCKETS = 32
REL_MAX_DIST = 128
D_FF = 4 * D_MODEL
EPS = 1e-6
NEG = -1e30
IN_COLS = 2 * A_WIDTH + 3 * B_WIDTH + 2 * D_MODEL

kernel_name = "hybrid_gmlp_moba_gated_block"


def rmsnorm(x, g):
    xf = x.astype(jnp.float32)
    y = xf * lax.rsqrt(jnp.mean(xf * xf, axis=-1, keepdims=True) + EPS)
    return (y * g.astype(jnp.float32)).astype(x.dtype)


def layernorm(x, g):
    xf = x.astype(jnp.float32)
    mu = jnp.mean(xf, axis=-1, keepdims=True)
    var = jnp.mean(jnp.square(xf - mu), axis=-1, keepdims=True)
    y = (xf - mu) * lax.rsqrt(var + EPS)
    return (y * g.astype(jnp.float32)).astype(x.dtype)


def t5_bucket(dist):
    n = jnp.maximum(dist, 0)
    max_exact = REL_BUCKETS // 2
    nf = jnp.maximum(n, 1).astype(jnp.float32)
    large = max_exact + (jnp.log(nf / max_exact) / math.log(REL_MAX_DIST / max_exact)
                         * (REL_BUCKETS - max_exact)).astype(jnp.int32)
    large = jnp.minimum(large, REL_BUCKETS - 1)
    return jnp.where(n < max_exact, n, large)


def spatial_gating_mixer(uv, v_gain, w_s, b_s):
    z = jax.nn.gelu(uv, approximate=False)
    u, v = jnp.split(z, 2, axis=-1)
    v = layernorm(v, v_gain)
    bsz, s, _ = v.shape
    nc = s // A_CHUNK
    v = v.reshape(bsz, nc, A_CHUNK, A_GROUPS, A_GROUP_DIM)
    causal = jnp.tril(jnp.ones((A_CHUNK, A_CHUNK), dtype=w_s.dtype))
    w = w_s * causal[None]
    sv = jnp.einsum('gts,bcsgd->bctgd', w, v) + b_s.T[:, :, None]
    return u * sv.reshape(bsz, s, A_WIDTH)


def moba_mixer(q, k, v, rel_bias):
    bsz, s, _ = q.shape
    L = MOBA_BLOCK
    def heads(t):
        return t.reshape(bsz, s, B_HEADS, B_HEAD_DIM).transpose(0, 2, 1, 3)
    q, k, v = heads(q), heads(k), heads(v)
    nb = -(-s // L)
    pad = nb * L - s
    k = jnp.pad(k, ((0, 0), (0, 0), (0, pad), (0, 0)))
    v = jnp.pad(v, ((0, 0), (0, 0), (0, pad), (0, 0)))
    kb = k.reshape(bsz, B_HEADS, nb, L, B_HEAD_DIM)
    vb = v.reshape(bsz, B_HEADS, nb, L, B_HEAD_DIM)
    kmean = jnp.mean(kb, axis=3)
    topk = min(MOBA_TOPK, nb)
    scale = B_HEAD_DIM ** -0.5
    bidx = jnp.arange(bsz)[:, None, None, None]
    hidx = jnp.arange(B_HEADS)[None, :, None, None]
    bias_hb = rel_bias.T.astype(jnp.float32)
    blk = jnp.arange(nb)
    offs = jnp.arange(L)

    def chunk_fn(c):
        start = c * Q_CHUNK
        qc = lax.dynamic_slice_in_dim(q, start, Q_CHUNK, axis=2)
        qpos = start + jnp.arange(Q_CHUNK)
        own = start // L
        gs = jnp.einsum('bhqd,bhnd->bhqn', qc, kmean).astype(jnp.float32)
        past = blk[None, :] < (qpos // L)[:, None]
        gs = jnp.where(past, gs, NEG)
        top_s, sel = lax.top_k(gs, topk)
        valid = top_s > NEG / 2
        ks = kb[bidx, hidx, sel]
        vs = vb[bidx, hidx, sel]
        kpos_sel = sel[..., None] * L + offs
        b_sel = bias_hb[hidx[..., None], t5_bucket(qpos[:, None, None] - kpos_sel)]
        logit_sel = (jnp.einsum('bhqd,bhqkld->bhqkl', qc, ks) * scale).astype(jnp.float32) + b_sel
        logit_sel = jnp.where(valid[..., None], logit_sel, NEG).reshape(bsz, B_HEADS, Q_CHUNK, topk * L)
        k_own = lax.dynamic_index_in_dim(kb, own, axis=2, keepdims=False)
        v_own = lax.dynamic_index_in_dim(vb, own, axis=2, keepdims=False)
        dist_own = qpos[:, None] - (own * L + offs)[None, :]
        b_own = bias_hb[:, t5_bucket(dist_own)][None]
        logit_own = (jnp.einsum('bhqd,bhld->bhql', qc, k_own) * scale).astype(jnp.float32) + b_own
        logit_own = jnp.where(dist_own >= 0, logit_own, NEG)
        p = jax.nn.softmax(jnp.concatenate([logit_sel, logit_own], axis=-1), axis=-1)
        p_sel = p[..., :topk * L].reshape(bsz, B_HEADS, Q_CHUNK, topk, L).astype(q.dtype)
        p_own = p[..., topk * L:].astype(q.dtype)
        out = (jnp.einsum('bhqkl,bhqkld->bhqd', p_sel, vs)
               + jnp.einsum('bhql,bhld->bhqd', p_own, v_own))
        return out.astype(q.dtype)

    outs = lax.map(chunk_fn, jnp.arange(s // Q_CHUNK, dtype=jnp.int32))
    return outs.transpose(1, 0, 3, 2, 4).reshape(bsz, s, B_WIDTH)


def setup_inputs(seed: int = 0) -> dict:
    key = jax.random.key(seed)
    ks = jax.random.split(key, 16)
    f32 = jnp.float32
    def nrm(k, shape, scale):
        return jax.random.normal(k, shape, f32) * scale
    return {
        "x": nrm(ks[0], (BATCH, SEQ, D_MODEL), 1.0),
        "ln_mix": 1.0 + nrm(ks[1], (DEPTH, D_MODEL), 0.01),
        "w_in": nrm(ks[2], (DEPTH, D_MODEL, IN_COLS), D_MODEL ** -0.5),
        "a_v_gain": 1.0 + nrm(ks[3], (DEPTH, A_WIDTH), 0.01),
        "a_spatial": nrm(ks[4], (DEPTH, A_GROUPS, A_CHUNK, A_CHUNK), A_CHUNK ** -0.5),
        "a_spatial_bias": 1.0 + nrm(ks[5], (DEPTH, A_GROUPS, A_CHUNK), 0.01),
        "w_proj_a": nrm(ks[6], (DEPTH, A_WIDTH, D_MODEL), A_WIDTH ** -0.5),
        "w_proj_b": nrm(ks[7], (DEPTH, B_WIDTH, D_MODEL), B_WIDTH ** -0.5),
        "w_out": nrm(ks[8], (DEPTH, D_MODEL, D_MODEL), D_MODEL ** -0.5),
        "rel_bias": nrm(ks[9], (REL_BUCKETS, B_HEADS), 0.1),
        "ln_mlp": 1.0 + nrm(ks[10], (DEPTH, D_MODEL), 0.01),
        "w_up": nrm(ks[11], (DEPTH, D_MODEL, D_FF), D_MODEL ** -0.5),
        "w_down": nrm(ks[12], (DEPTH, D_FF, D_MODEL), D_FF ** -0.5),
        "ln_final": 1.0 + nrm(ks[13], (D_MODEL,), 0.01),
    }


def reference(x, ln_mix, w_in, a_v_gain, a_spatial, a_spatial_bias, w_proj_a, w_proj_b,
              w_out, rel_bias, ln_mlp, w_up, w_down, ln_final):
    for l in range(DEPTH):
        h = rmsnorm(x, ln_mix[l])
        z = h @ w_in[l]
        uv, qkv, gates = jnp.split(z, [2 * A_WIDTH, 2 * A_WIDTH + 3 * B_WIDTH], axis=-1)
        y_a = spatial_gating_mixer(uv, a_v_gain[l], a_spatial[l], a_spatial_bias[l]) @ w_proj_a[l]
        q, k, v = jnp.split(qkv, 3, axis=-1)
        y_b = moba_mixer(q, k, v, rel_bias) @ w_proj_b[l]
        g_a, g_b = jnp.split(gates, 2, axis=-1)
        merged = jax.nn.sigmoid(g_a) * y_a + jax.nn.sigmoid(g_b) * y_b
        x = x + merged @ w_out[l]
        h = rmsnorm(x, ln_mlp[l])
        x = x + jnp.square(jax.nn.relu(h @ w_up[l])) @ w_down[l]
    return rmsnorm(x, ln_final)
```

```python
import functools
import math

import jax
import jax.numpy as jnp
from jax import lax
from jax.experimental import pallas as pl
from jax.experimental.pallas import tpu as pltpu

A_GROUPS = 16
A_GROUP_DIM = 128
A_CHUNK = 128
B_HEADS = 16
B_HEAD_DIM = 128
MOBA_BLOCK = 256
MOBA_TOPK = 3
REL_BUCKETS = 32
REL_MAX_DIST = 128
EPS = 1e-6
NEG = -1e30
LOG2E = math.log2(math.e)
NEG_LOG2 = NEG * LOG2E
REMOVED = -3e38

F32 = jnp.float32
BF16 = jnp.bfloat16

MIB = 1024 * 1024
NT_DIMS = (((1,), (1,)), ((), ()))
TN_DIMS = (((0,), (0,)), ((), ()))


def _params(semantics, vmem_mib):
    return pltpu.CompilerParams(dimension_semantics=semantics, vmem_limit_bytes=vmem_mib * MIB)


def _resident(shape, index_map):
    return pl.BlockSpec(shape, index_map, pipeline_mode=pl.Buffered(1))


def _rms(xf, gain):
    return xf * lax.rsqrt(jnp.mean(xf * xf, axis=-1, keepdims=True) + EPS) * gain


def _norm_kernel(x_ref, g_ref, h_ref):
    h_ref[...] = _rms(x_ref[...], g_ref[...]).astype(h_ref.dtype)


def _norm(x, gain, tm=512):
    s, d = x.shape
    return pl.pallas_call(
        _norm_kernel,
        grid=(s // tm,),
        in_specs=[pl.BlockSpec((tm, d), lambda i: (i, 0)),
                  pl.BlockSpec((1, d), lambda i: (0, 0))],
        out_specs=pl.BlockSpec((tm, d), lambda i: (i, 0)),
        out_shape=jax.ShapeDtypeStruct((s, d), BF16),
        compiler_params=_params(("arbitrary",), 32),
        name="rms_norm",
    )(x, gain.reshape(1, d))


def _gelu(a):
    return 0.5 * a * (1.0 + lax.erf(a * math.sqrt(0.5)))


_EPILOGUES = {"gelu": _gelu, "none": lambda a: a, "sigmoid": jax.nn.sigmoid}


def _proj_kernel(h_ref, w_ref, o_ref, *, epilogue):
    acc = jnp.dot(h_ref[...], w_ref[...], preferred_element_type=F32)
    o_ref[...] = _EPILOGUES[epilogue](acc).astype(o_ref.dtype)


def _in_proj(h, w, col0, ncols, epilogue, tm=1024, tn=1024):
    s, d = h.shape
    jb = col0 // tn
    return pl.pallas_call(
        functools.partial(_proj_kernel, epilogue=epilogue),
        grid=(s // tm, ncols // tn),
        in_specs=[pl.BlockSpec((tm, d), lambda i, j: (i, 0)),
                  pl.BlockSpec((d, tn), lambda i, j: (0, j + jb))],
        out_specs=pl.BlockSpec((tm, tn), lambda i, j: (i, j)),
        out_shape=jax.ShapeDtypeStruct((s, ncols), BF16),
        compiler_params=_params(("arbitrary", "arbitrary"), 48),
        name="in_proj_" + epilogue,
    )(h, w)


def _gmlp_kernel(u_ref, v_ref, ga_ref, gain_ref, ws_ref, bt_ref, wpa_ref, o_ref, wm_ref, a_ref):
    tq = u_ref.shape[0]

    @pl.when(pl.program_id(0) == 0)
    def _():
        t = lax.broadcasted_iota(jnp.int32, (A_CHUNK, A_CHUNK), 0)
        s = lax.broadcasted_iota(jnp.int32, (A_CHUNK, A_CHUNK), 1)
        for g in range(A_GROUPS):
            wm_ref[g] = jnp.where(t >= s, ws_ref[g], 0.0).astype(BF16)

    def chunk(c, carry):
        r0 = pl.multiple_of(c * A_CHUNK, A_CHUNK)
        v = v_ref[pl.ds(r0, A_CHUNK), :].astype(F32)
        mu = jnp.mean(v, axis=-1, keepdims=True)
        d = v - mu
        var = jnp.mean(d * d, axis=-1, keepdims=True)
        vn = (d * lax.rsqrt(var + EPS) * gain_ref[...]).astype(BF16)
        for g in range(A_GROUPS):
            cols = slice(g * A_GROUP_DIM, (g + 1) * A_GROUP_DIM)
            sv = jnp.dot(wm_ref[g], vn[:, cols], preferred_element_type=F32) + bt_ref[:, g:g + 1]
            u = u_ref[pl.ds(r0, A_CHUNK), cols].astype(F32)
            a_ref[pl.ds(r0, A_CHUNK), cols] = (u * sv).astype(BF16)
        return carry

    lax.fori_loop(0, tq // A_CHUNK, chunk, 0)
    y = jnp.dot(a_ref[...], wpa_ref[...], preferred_element_type=F32)
    o_ref[...] = ga_ref[...].astype(F32) * y


def _gmlp(zuv, zg, gain, w_s, b_s, w_proj_a, tq=512):
    s = zuv.shape[0]
    aw = A_GROUPS * A_GROUP_DIM
    d = w_proj_a.shape[1]
    return pl.pallas_call(
        _gmlp_kernel,
        grid=(s // tq,),
        in_specs=[pl.BlockSpec((tq, aw), lambda i: (i, 0)),
                  pl.BlockSpec((tq, aw), lambda i: (i, 1)),
                  pl.BlockSpec((tq, d), lambda i: (i, 0)),
                  _resident((1, aw), lambda i: (0, 0)),
                  _resident((A_GROUPS, A_CHUNK, A_CHUNK), lambda i: (0, 0, 0)),
                  _resident((A_CHUNK, A_GROUPS), lambda i: (0, 0)),
                  _resident((aw, d), lambda i: (0, 0))],
        out_specs=pl.BlockSpec((tq, d), lambda i: (i, 0)),
        out_shape=jax.ShapeDtypeStruct((s, d), F32),
        scratch_shapes=[pltpu.VMEM((A_GROUPS, A_CHUNK, A_CHUNK), BF16),
                        pltpu.VMEM((tq, aw), BF16)],
        compiler_params=_params(("arbitrary",), 48),
        name="gmlp_mixer",
    )(zuv, zuv, zg, gain.reshape(1, aw), w_s, b_s.T, w_proj_a)


def _t5_bias_log2(dist, rb_ref, base):
    max_exact = REL_BUCKETS // 2
    n = jnp.maximum(dist, 0)
    nf = jnp.maximum(n, 1).astype(F32)
    large = max_exact + (jnp.log(nf / max_exact) / math.log(REL_MAX_DIST / max_exact)
                         * (REL_BUCKETS - max_exact)).astype(jnp.int32)
    large = jnp.minimum(large, REL_BUCKETS - 1)
    bucket = jnp.where(n < max_exact, n, large)
    out = jnp.zeros(dist.shape, F32)
    for b in range(REL_BUCKETS):
        out = jnp.where(bucket == b, rb_ref[base + b] * LOG2E, out)
    return out


def _moba_kernel(rb_ref, q_ref, k_ref, v_ref, o_ref, kmean_ref, vt_ref, bown_ref, bprev_ref, sel_ref):
    L = MOBA_BLOCK
    nb = q_ref.shape[0] // L
    base = pl.program_id(0) * REL_BUCKETS
    c_log2 = (B_HEAD_DIM ** -0.5) * LOG2E
    far_bias = rb_ref[base + REL_BUCKETS - 1] * LOG2E

    def prep(n, carry):
        r0 = pl.multiple_of(n * L, L)
        kmean_ref[pl.ds(n, 1), :] = jnp.mean(k_ref[pl.ds(r0, L), :].astype(F32), axis=0, keepdims=True)
        vt_ref[n] = v_ref[pl.ds(r0, L), :].astype(F32).T.astype(BF16)
        return carry

    lax.fori_loop(0, nb, prep, 0)

    kk = lax.broadcasted_iota(jnp.int32, (L, L), 0)
    qq = lax.broadcasted_iota(jnp.int32, (L, L), 1)
    d_own = qq - kk
    bown_ref[...] = jnp.where(d_own >= 0, _t5_bias_log2(d_own, rb_ref, base), NEG_LOG2)
    bprev_ref[...] = _t5_bias_log2(d_own + L, rb_ref, base)

    def tile(kj, vtj, q_i, bias, m, l, acc):
        s_t = lax.dot_general(kj, q_i, NT_DIMS, preferred_element_type=F32)
        t = s_t * c_log2 + bias
        m_new = jnp.maximum(m, jnp.max(t, axis=0, keepdims=True))
        alpha = jnp.exp2(m - m_new)
        p = jnp.exp2(t - m_new)
        l = alpha * l + jnp.sum(p, axis=0, keepdims=True)
        acc = alpha * acc + jnp.dot(vtj, p.astype(BF16), preferred_element_type=F32)
        return m_new, l, acc

    def qblock(i, carry):
        r0 = pl.multiple_of(i * L, L)
        q_i = q_ref[pl.ds(r0, L), :]

        gs = lax.dot_general(kmean_ref[...].astype(BF16), q_i, NT_DIMS, preferred_element_type=F32)
        blk = lax.broadcasted_iota(jnp.int32, gs.shape, 0)
        gs = jnp.where(blk < i, gs, NEG)
        sel = jnp.zeros(gs.shape, F32)
        for _ in range(MOBA_TOPK):
            top = jnp.max(gs, axis=0, keepdims=True)
            idx = jnp.min(jnp.where(gs == top, blk, nb), axis=0, keepdims=True)
            hit = blk == idx
            sel = jnp.where(hit, jnp.where(top > NEG / 2, 1.0, 0.0), sel)
            gs = jnp.where(hit, REMOVED, gs)
        sel_ref[...] = jnp.where(sel > 0.5, 0.0, NEG_LOG2)

        m0 = jnp.full((1, L), NEG_LOG2, F32)
        l0 = jnp.zeros((1, L), F32)
        acc0 = jnp.zeros((B_HEAD_DIM, L), F32)
        m, l, acc = tile(k_ref[pl.ds(r0, L), :], vt_ref[i], q_i, bown_ref[...], m0, l0, acc0)

        jp = jnp.maximum(i - 1, 0)
        rp = pl.multiple_of(jp * L, L)
        m, l, acc = tile(k_ref[pl.ds(rp, L), :], vt_ref[jp], q_i,
                         bprev_ref[...] + sel_ref[pl.ds(jp, 1), :], m, l, acc)

        def far(j, c):
            rj = pl.multiple_of(j * L, L)
            return tile(k_ref[pl.ds(rj, L), :], vt_ref[j], q_i,
                        sel_ref[pl.ds(j, 1), :] + far_bias, *c)

        m, l, acc = lax.fori_loop(0, i - 1, far, (m, l, acc))
        o_ref[pl.ds(r0, L), :] = (acc / l).T.astype(o_ref.dtype)
        return carry

    lax.fori_loop(0, nb, qblock, 0)


def _moba(zqkv, rel_bias):
    s = zqkv.shape[0]
    dh = B_HEAD_DIM
    nb = s // MOBA_BLOCK
    rb = rel_bias.T.reshape(-1).astype(F32)
    return pl.pallas_call(
        _moba_kernel,
        grid=(B_HEADS,),
        in_specs=[pl.BlockSpec(memory_space=pltpu.SMEM),
                  pl.BlockSpec((s, dh), lambda h: (0, h)),
                  pl.BlockSpec((s, dh), lambda h: (0, B_HEADS + h)),
                  pl.BlockSpec((s, dh), lambda h: (0, 2 * B_HEADS + h))],
        out_specs=pl.BlockSpec((s, dh), lambda h: (0, h)),
        out_shape=jax.ShapeDtypeStruct((s, B_HEADS * dh), BF16),
        scratch_shapes=[pltpu.VMEM((nb, dh), F32),
                        pltpu.VMEM((nb, dh, MOBA_BLOCK), BF16),
                        pltpu.VMEM((MOBA_BLOCK, MOBA_BLOCK), F32),
                        pltpu.VMEM((MOBA_BLOCK, MOBA_BLOCK), F32),
                        pltpu.VMEM((nb, MOBA_BLOCK), F32)],
        compiler_params=_params(("arbitrary",), 48),
        name="moba_attention",
    )(rb, zqkv, zqkv, zqkv)


def _merge_kernel(attn_ref, gb_ref, gaya_ref, x_ref, wpb_ref, wout_ref, g_ref, x1_ref, h2_ref):
    y_b = jnp.dot(attn_ref[...], wpb_ref[...], preferred_element_type=F32)
    merged = gaya_ref[...] + gb_ref[...].astype(F32) * y_b
    x1 = x_ref[...] + jnp.dot(merged.astype(BF16), wout_ref[...], preferred_element_type=F32)
    x1_ref[...] = x1
    h2_ref[...] = _rms(x1, g_ref[...]).astype(h2_ref.dtype)


def _merge(attn, zg, gaya, x, w_proj_b, w_out, gain, tq=512):
    s, d = x.shape
    bw = attn.shape[1]
    return pl.pallas_call(
        _merge_kernel,
        grid=(s // tq,),
        in_specs=[pl.BlockSpec((tq, bw), lambda i: (i, 0)),
                  pl.BlockSpec((tq, d), lambda i: (i, 1)),
                  pl.BlockSpec((tq, d), lambda i: (i, 0)),
                  pl.BlockSpec((tq, d), lambda i: (i, 0)),
                  _resident((bw, d), lambda i: (0, 0)),
                  _resident((d, d), lambda i: (0, 0)),
                  _resident((1, d), lambda i: (0, 0))],
        out_specs=[pl.BlockSpec((tq, d), lambda i: (i, 0)),
                   pl.BlockSpec((tq, d), lambda i: (i, 0))],
        out_shape=[jax.ShapeDtypeStruct((s, d), F32),
                   jax.ShapeDtypeStruct((s, d), BF16)],
        compiler_params=_params(("arbitrary",), 56),
        name="merge_out_proj",
    )(attn, zg, gaya, x, w_proj_b, w_out, gain.reshape(1, d))


def _ffn_kernel(h_ref, wu_ref, wd_ref, x1_ref, g_ref, o_ref, *, final_norm):
    j = pl.program_id(1)
    a = jnp.dot(h_ref[...], wu_ref[...], preferred_element_type=F32)
    a = jnp.square(jnp.maximum(a, 0.0)).astype(BF16)
    part = jnp.dot(a, wd_ref[...], preferred_element_type=F32)

    @pl.when(j == 0)
    def _():
        o_ref[...] = part

    @pl.when(j > 0)
    def _():
        o_ref[...] += part

    @pl.when(j == pl.num_programs(1) - 1)
    def _():
        x2 = x1_ref[...] + o_ref[...]
        o_ref[...] = _rms(x2, g_ref[...]) if final_norm else x2


def _ffn(h2, w_up, w_down, x1, gain, final_norm, tm=1024, tf=512):
    s, d = x1.shape
    dff = w_up.shape[1]
    return pl.pallas_call(
        functools.partial(_ffn_kernel, final_norm=final_norm),
        grid=(s // tm, dff // tf),
        in_specs=[_resident((tm, d), lambda i, j: (i, 0)),
                  pl.BlockSpec((d, tf), lambda i, j: (0, j)),
                  pl.BlockSpec((tf, d), lambda i, j: (j, 0)),
                  _resident((tm, d), lambda i, j: (i, 0)),
                  _resident((1, d), lambda i, j: (0, 0))],
        out_specs=pl.BlockSpec((tm, d), lambda i, j: (i, 0)),
        out_shape=jax.ShapeDtypeStruct((s, d), F32),
        compiler_params=_params(("arbitrary", "arbitrary"), 56),
        name="ffn_relu2",
    )(h2, w_up, w_down, x1, gain.reshape(1, d))


def kernel(x, ln_mix, w_in, a_v_gain, a_spatial, a_spatial_bias, w_proj_a, w_proj_b, w_out,
           rel_bias, ln_mlp, w_up, w_down, ln_final):
    bsz, s, d = x.shape
    depth = ln_mix.shape[0]
    aw = A_GROUPS * A_GROUP_DIM
    bw = B_HEADS * B_HEAD_DIM
    assert w_in.shape[2] == 2 * aw + 3 * bw + 2 * d
    outs = []
    for b in range(bsz):
        xb = x[b]
        for l in range(depth):
            last = l == depth - 1
            w_in_l = w_in[l].astype(BF16)
            h = _norm(xb, ln_mix[l])
            zuv = _in_proj(h, w_in_l, 0, 2 * aw, "gelu")
            zqkv = _in_proj(h, w_in_l, 2 * aw, 3 * bw, "none")
            zg = _in_proj(h, w_in_l, 2 * aw + 3 * bw, 2 * d, "sigmoid")
            gaya = _gmlp(zuv, zg, a_v_gain[l], a_spatial[l], a_spatial_bias[l], w_proj_a[l].astype(BF16))
            attn = _moba(zqkv, rel_bias)
            x1, h2 = _merge(attn, zg, gaya, xb, w_proj_b[l].astype(BF16), w_out[l].astype(BF16), ln_mlp[l])
            gain = ln_final if last else jnp.ones((d,), F32)
            xb = _ffn(h2, w_up[l].astype(BF16), w_down[l].astype(BF16), x1, gain, final_norm=last)
        outs.append(xb)
    return jnp.stack(outs)
```

```python
import functools
import math

import jax
import jax.numpy as jnp
from jax import lax
from jax.experimental import pallas as pl
from jax.experimental.pallas import tpu as pltpu

A_GROUPS = 16
A_GROUP_DIM = 128
A_CHUNK = 128
B_HEADS = 16
B_HEAD_DIM = 128
MOBA_BLOCK = 256
MOBA_TOPK = 3
FAR_GROUP = 2
Q_STEP_BLOCKS = 2
REL_BUCKETS = 32
REL_MAX_DIST = 128
EPS = 1e-6
NEG = -1e30
LOG2E = math.log2(math.e)
NEG_LOG2 = NEG * LOG2E
FAR_MASKED = 2 * NEG_LOG2
REMOVED = -3e38

F32 = jnp.float32
BF16 = jnp.bfloat16

MIB = 1024 * 1024
NT_DIMS = (((1,), (1,)), ((), ()))


def _params(semantics, vmem_mib):
    return pltpu.CompilerParams(dimension_semantics=semantics, vmem_limit_bytes=vmem_mib * MIB)


def _resident(shape, index_map):
    return pl.BlockSpec(shape, index_map, pipeline_mode=pl.Buffered(1))


def _rms(xf, gain):
    return xf * lax.rsqrt(jnp.mean(xf * xf, axis=-1, keepdims=True) + EPS) * gain


def _norm_kernel(x_ref, g_ref, h_ref):
    h_ref[...] = _rms(x_ref[...], g_ref[...]).astype(h_ref.dtype)


def _norm(x, gain, tm=512):
    s, d = x.shape
    return pl.pallas_call(
        _norm_kernel,
        grid=(s // tm,),
        in_specs=[pl.BlockSpec((tm, d), lambda i: (i, 0)),
                  pl.BlockSpec((1, d), lambda i: (0, 0))],
        out_specs=pl.BlockSpec((tm, d), lambda i: (i, 0)),
        out_shape=jax.ShapeDtypeStruct((s, d), BF16),
        compiler_params=_params(("arbitrary",), 32),
        name="rms_norm",
    )(x, gain.reshape(1, d))


def _gelu(a):
    return 0.5 * a * (1.0 + lax.erf(a * math.sqrt(0.5)))


_EPILOGUES = {"gelu": _gelu, "none": lambda a: a, "sigmoid": jax.nn.sigmoid}


def _proj_kernel(h_ref, w_ref, o_ref, *, epilogue):
    acc = jnp.dot(h_ref[...], w_ref[...], preferred_element_type=F32)
    o_ref[...] = _EPILOGUES[epilogue](acc).astype(o_ref.dtype)


def _in_proj(h, w, col0, ncols, epilogue, tm=1024, tn=1024):
    s, d = h.shape
    jb = col0 // tn
    return pl.pallas_call(
        functools.partial(_proj_kernel, epilogue=epilogue),
        grid=(s // tm, ncols // tn),
        in_specs=[pl.BlockSpec((tm, d), lambda i, j: (i, 0)),
                  pl.BlockSpec((d, tn), lambda i, j: (0, j + jb))],
        out_specs=pl.BlockSpec((tm, tn), lambda i, j: (i, j)),
        out_shape=jax.ShapeDtypeStruct((s, ncols), BF16),
        compiler_params=_params(("arbitrary", "arbitrary"), 48),
        name="in_proj_" + epilogue,
    )(h, w)


def _gmlp_kernel(u_ref, v_ref, ga_ref, gain_ref, ws_ref, bt_ref, wpa_ref, o_ref, wm_ref, a_ref):
    tq = u_ref.shape[0]

    @pl.when(pl.program_id(0) == 0)
    def _():
        t = lax.broadcasted_iota(jnp.int32, (A_CHUNK, A_CHUNK), 0)
        s = lax.broadcasted_iota(jnp.int32, (A_CHUNK, A_CHUNK), 1)
        for g in range(A_GROUPS):
            wm_ref[g] = jnp.where(t >= s, ws_ref[g], 0.0).astype(BF16)

    def chunk(c, carry):
        r0 = pl.multiple_of(c * A_CHUNK, A_CHUNK)
        v = v_ref[pl.ds(r0, A_CHUNK), :].astype(F32)
        mu = jnp.mean(v, axis=-1, keepdims=True)
        d = v - mu
        var = jnp.mean(d * d, axis=-1, keepdims=True)
        vn = (d * lax.rsqrt(var + EPS) * gain_ref[...]).astype(BF16)
        for g in range(A_GROUPS):
            cols = slice(g * A_GROUP_DIM, (g + 1) * A_GROUP_DIM)
            sv = jnp.dot(wm_ref[g], vn[:, cols], preferred_element_type=F32) + bt_ref[:, g:g + 1]
            u = u_ref[pl.ds(r0, A_CHUNK), cols].astype(F32)
            a_ref[pl.ds(r0, A_CHUNK), cols] = (u * sv).astype(BF16)
        return carry

    lax.fori_loop(0, tq // A_CHUNK, chunk, 0)
    y = jnp.dot(a_ref[...], wpa_ref[...], preferred_element_type=F32)
    o_ref[...] = ga_ref[...].astype(F32) * y


def _gmlp(zuv, zg, gain, w_s, b_s, w_proj_a, tq=512):
    s = zuv.shape[0]
    aw = A_GROUPS * A_GROUP_DIM
    d = w_proj_a.shape[1]
    return pl.pallas_call(
        _gmlp_kernel,
        grid=(s // tq,),
        in_specs=[pl.BlockSpec((tq, aw), lambda i: (i, 0)),
                  pl.BlockSpec((tq, aw), lambda i: (i, 1)),
                  pl.BlockSpec((tq, d), lambda i: (i, 0)),
                  _resident((1, aw), lambda i: (0, 0)),
                  _resident((A_GROUPS, A_CHUNK, A_CHUNK), lambda i: (0, 0, 0)),
                  _resident((A_CHUNK, A_GROUPS), lambda i: (0, 0)),
                  _resident((aw, d), lambda i: (0, 0))],
        out_specs=pl.BlockSpec((tq, d), lambda i: (i, 0)),
        out_shape=jax.ShapeDtypeStruct((s, d), F32),
        scratch_shapes=[pltpu.VMEM((A_GROUPS, A_CHUNK, A_CHUNK), BF16),
                        pltpu.VMEM((tq, aw), BF16)],
        compiler_params=_params(("arbitrary",), 48),
        name="gmlp_mixer",
    )(zuv, zuv, zg, gain.reshape(1, aw), w_s, b_s.T, w_proj_a)


def _t5_bias_log2(dist, rb_ref, base):
    max_exact = REL_BUCKETS // 2
    n = jnp.maximum(dist, 0)
    nf = jnp.maximum(n, 1).astype(F32)
    large = max_exact + (jnp.log(nf / max_exact) / math.log(REL_MAX_DIST / max_exact)
                         * (REL_BUCKETS - max_exact)).astype(jnp.int32)
    large = jnp.minimum(large, REL_BUCKETS - 1)
    bucket = jnp.where(n < max_exact, n, large)
    out = jnp.zeros(dist.shape, F32)
    for b in range(REL_BUCKETS):
        out = jnp.where(bucket == b, rb_ref[base + b] * LOG2E, out)
    return out


def _moba_kernel(rb_ref, q_ref, k_ref, v_ref, o_ref, kmean_ref, vt_ref, tdiag_ref, rows_ref):
    L = MOBA_BLOCK
    G = FAR_GROUP
    QB = Q_STEP_BLOCKS
    W = QB * L
    nb = q_ref.shape[0] // L
    log2_l = L.bit_length() - 1
    base = pl.program_id(0) * REL_BUCKETS
    c_log2 = (B_HEAD_DIM ** -0.5) * LOG2E
    far_bias = rb_ref[base + REL_BUCKETS - 1] * LOG2E

    def prep(n, carry):
        r0 = pl.multiple_of(n * L, L)
        kmean_ref[pl.ds(n, 1), :] = jnp.mean(k_ref[pl.ds(r0, L), :].astype(F32), axis=0, keepdims=True)
        vt_ref[n] = v_ref[pl.ds(r0, L), :].astype(F32).T.astype(BF16)
        return carry

    lax.fori_loop(0, nb, prep, 0)

    kk = lax.broadcasted_iota(jnp.int32, (L, L), 0)
    qq = lax.broadcasted_iota(jnp.int32, (L, L), 1)
    d_own = qq - kk
    b_prev = _t5_bias_log2(d_own + L, rb_ref, base)
    b_own = jnp.where(d_own >= 0, _t5_bias_log2(d_own, rb_ref, base), NEG_LOG2)
    masked = jnp.full((L, L), NEG_LOG2, F32)
    window = [[b_prev, jnp.zeros((L, L), F32)], [b_own, b_prev], [masked, b_own], [masked, masked]]
    for r, tiles in enumerate(window):
        for c, tile in enumerate(tiles):
            tdiag_ref[r * L:(r + 1) * L, c * L:(c + 1) * L] = tile

    kmean = kmean_ref[...].astype(BF16)

    def gate(u, carry):
        q_u = q_ref[pl.ds(pl.multiple_of(u * W, W), W), :]
        gs = lax.dot_general(kmean, q_u, NT_DIMS, preferred_element_type=F32)
        blk = lax.broadcasted_iota(jnp.int32, gs.shape, 0)
        own = QB * u + lax.shift_right_logical(lax.broadcasted_iota(jnp.int32, gs.shape, 1), log2_l)
        gs = jnp.where(blk < own, gs, NEG)
        sel = jnp.zeros(gs.shape, F32)
        for _ in range(MOBA_TOPK):
            top = jnp.max(gs, axis=0, keepdims=True)
            idx = jnp.min(jnp.where(gs == top, blk, nb), axis=0, keepdims=True)
            hit = blk == idx
            sel = jnp.where(hit, jnp.where(top > NEG / 2, 1.0, 0.0), sel)
            gs = jnp.where(hit, REMOVED, gs)
        picked = sel > 0.5
        rows_ref[u] = jnp.where(blk < own - 1, jnp.where(picked, far_bias, FAR_MASKED),
                                jnp.where(blk == own - 1, jnp.where(picked, 0.0, FAR_MASKED), 0.0))
        return carry

    lax.fori_loop(0, nb // QB, gate, 0)

    def pv(vts, ps):
        out = None
        for vt, p in zip(vts, ps):
            d = jnp.dot(vt, p.astype(BF16), preferred_element_type=F32)
            out = d if out is None else out + d
        return out

    def qstep(u, carry):
        q0 = pl.multiple_of(u * W, W)
        q_u = q_ref[pl.ds(q0, W), :]
        rows_u = rows_ref.at[u]

        jw = jnp.maximum(QB * u - 1, 0)
        rw = pl.multiple_of(jw * L, L)
        toff = pl.multiple_of(jnp.where(u >= 1, 0, L), L)
        nw = QB + 1
        s_t = lax.dot_general(k_ref[pl.ds(rw, nw * L), :], q_u, NT_DIMS, preferred_element_type=F32)
        ts = [s_t[b * L:(b + 1) * L] * c_log2 + tdiag_ref[pl.ds(toff + b * L, L), :]
              + rows_u[pl.ds(jw + b, 1), :] for b in range(nw)]
        m = functools.reduce(jnp.maximum, [jnp.max(t, axis=0, keepdims=True) for t in ts])
        ps = [jnp.exp2(t - m) for t in ts]
        l = functools.reduce(jnp.add, [jnp.sum(p, axis=0, keepdims=True) for p in ps])
        acc = pv([vt_ref[jw + b] for b in range(nw)], ps)

        n_far = jnp.maximum(QB * u - 1, 0)

        def far(g, c):
            m, l, acc = c
            j0 = g * G
            rj = pl.multiple_of(j0 * L, G * L)
            s_t = lax.dot_general(k_ref[pl.ds(rj, G * L), :], q_u, NT_DIMS, preferred_element_type=F32)
            sb = [s_t[b * L:(b + 1) * L] for b in range(G)]
            rows = [jnp.where(j0 + b < n_far, rows_u[pl.ds(j0 + b, 1), :], FAR_MASKED) for b in range(G)]
            m_new = m
            for b in range(G):
                m_new = jnp.maximum(m_new, jnp.max(sb[b], axis=0, keepdims=True) * c_log2 + rows[b])
            alpha = jnp.exp2(m - m_new)
            ps = [jnp.exp2(sb[b] * c_log2 + (rows[b] - m_new)) for b in range(G)]
            l = alpha * l + functools.reduce(jnp.add, [jnp.sum(p, axis=0, keepdims=True) for p in ps])
            acc = alpha * acc + pv([vt_ref[j0 + b] for b in range(G)], ps)
            return m_new, l, acc

        n_groups = lax.shift_right_logical(n_far + (G - 1), G.bit_length() - 1)
        m, l, acc = lax.fori_loop(0, n_groups, far, (m, l, acc))
        o_ref[pl.ds(q0, W), :] = (acc / l).T.astype(o_ref.dtype)
        return carry

    lax.fori_loop(0, nb // QB, qstep, 0)


def _moba(zqkv, rel_bias):
    s = zqkv.shape[0]
    dh = B_HEAD_DIM
    L = MOBA_BLOCK
    nb = s // L
    assert nb % Q_STEP_BLOCKS == 0 and nb % FAR_GROUP == 0 and nb >= Q_STEP_BLOCKS + 1
    assert FAR_GROUP & (FAR_GROUP - 1) == 0 and Q_STEP_BLOCKS == 2
    rb = rel_bias.T.reshape(-1).astype(F32)
    return pl.pallas_call(
        _moba_kernel,
        grid=(B_HEADS,),
        in_specs=[pl.BlockSpec(memory_space=pltpu.SMEM),
                  pl.BlockSpec((s, dh), lambda h: (0, h)),
                  pl.BlockSpec((s, dh), lambda h: (0, B_HEADS + h)),
                  pl.BlockSpec((s, dh), lambda h: (0, 2 * B_HEADS + h))],
        out_specs=pl.BlockSpec((s, dh), lambda h: (0, h)),
        out_shape=jax.ShapeDtypeStruct((s, B_HEADS * dh), BF16),
        scratch_shapes=[pltpu.VMEM((nb, dh), F32),
                        pltpu.VMEM((nb, dh, L), BF16),
                        pltpu.VMEM(((Q_STEP_BLOCKS + 2) * L, Q_STEP_BLOCKS * L), F32),
                        pltpu.VMEM((nb // Q_STEP_BLOCKS, nb, Q_STEP_BLOCKS * L), F32)],
        compiler_params=_params(("arbitrary",), 48),
        name="moba_attention",
    )(rb, zqkv, zqkv, zqkv)


def _merge_kernel(attn_ref, gb_ref, gaya_ref, x_ref, wpb_ref, wout_ref, g_ref, x1_ref, h2_ref):
    y_b = jnp.dot(attn_ref[...], wpb_ref[...], preferred_element_type=F32)
    merged = gaya_ref[...] + gb_ref[...].astype(F32) * y_b
    x1 = x_ref[...] + jnp.dot(merged.astype(BF16), wout_ref[...], preferred_element_type=F32)
    x1_ref[...] = x1
    h2_ref[...] = _rms(x1, g_ref[...]).astype(h2_ref.dtype)


def _merge(attn, zg, gaya, x, w_proj_b, w_out, gain, tq=512):
    s, d = x.shape
    bw = attn.shape[1]
    return pl.pallas_call(
        _merge_kernel,
        grid=(s // tq,),
        in_specs=[pl.BlockSpec((tq, bw), lambda i: (i, 0)),
                  pl.BlockSpec((tq, d), lambda i: (i, 1)),
                  pl.BlockSpec((tq, d), lambda i: (i, 0)),
                  pl.BlockSpec((tq, d), lambda i: (i, 0)),
                  _resident((bw, d), lambda i: (0, 0)),
                  _resident((d, d), lambda i: (0, 0)),
                  _resident((1, d), lambda i: (0, 0))],
        out_specs=[pl.BlockSpec((tq, d), lambda i: (i, 0)),
                   pl.BlockSpec((tq, d), lambda i: (i, 0))],
        out_shape=[jax.ShapeDtypeStruct((s, d), F32),
                   jax.ShapeDtypeStruct((s, d), BF16)],
        compiler_params=_params(("arbitrary",), 56),
        name="merge_out_proj",
    )(attn, zg, gaya, x, w_proj_b, w_out, gain.reshape(1, d))


def _ffn_kernel(h_ref, wu_ref, wd_ref, x1_ref, g_ref, o_ref, *, final_norm):
    j = pl.program_id(1)
    a = jnp.dot(h_ref[...], wu_ref[...], preferred_element_type=F32)
    a = jnp.square(jnp.maximum(a, 0.0)).astype(BF16)
    part = jnp.dot(a, wd_ref[...], preferred_element_type=F32)

    @pl.when(j == 0)
    def _():
        o_ref[...] = part

    @pl.when(j > 0)
    def _():
        o_ref[...] += part

    @pl.when(j == pl.num_programs(1) - 1)
    def _():
        x2 = x1_ref[...] + o_ref[...]
        o_ref[...] = _rms(x2, g_ref[...]) if final_norm else x2


def _ffn(h2, w_up, w_down, x1, gain, final_norm, tm=1024, tf=512):
    s, d = x1.shape
    dff = w_up.shape[1]
    return pl.pallas_call(
        functools.partial(_ffn_kernel, final_norm=final_norm),
        grid=(s // tm, dff // tf),
        in_specs=[_resident((tm, d), lambda i, j: (i, 0)),
                  pl.BlockSpec((d, tf), lambda i, j: (0, j)),
                  pl.BlockSpec((tf, d), lambda i, j: (j, 0)),
                  _resident((tm, d), lambda i, j: (i, 0)),
                  _resident((1, d), lambda i, j: (0, 0))],
        out_specs=pl.BlockSpec((tm, d), lambda i, j: (i, 0)),
        out_shape=jax.ShapeDtypeStruct((s, d), F32),
        compiler_params=_params(("arbitrary", "arbitrary"), 56),
        name="ffn_relu2",
    )(h2, w_up, w_down, x1, gain.reshape(1, d))


def kernel(x, ln_mix, w_in, a_v_gain, a_spatial, a_spatial_bias, w_proj_a, w_proj_b, w_out,
           rel_bias, ln_mlp, w_up, w_down, ln_final):
    bsz, s, d = x.shape
    depth = ln_mix.shape[0]
    aw = A_GROUPS * A_GROUP_DIM
    bw = B_HEADS * B_HEAD_DIM
    assert w_in.shape[2] == 2 * aw + 3 * bw + 2 * d
    outs = []
    for b in range(bsz):
        xb = x[b]
        for l in range(depth):
            last = l == depth - 1
            w_in_l = w_in[l].astype(BF16)
            h = _norm(xb, ln_mix[l])
            zuv = _in_proj(h, w_in_l, 0, 2 * aw, "gelu")
            zqkv = _in_proj(h, w_in_l, 2 * aw, 3 * bw, "none")
            zg = _in_proj(h, w_in_l, 2 * aw + 3 * bw, 2 * d, "sigmoid")
            gaya = _gmlp(zuv, zg, a_v_gain[l], a_spatial[l], a_spatial_bias[l], w_proj_a[l].astype(BF16))
            attn = _moba(zqkv, rel_bias)
            x1, h2 = _merge(attn, zg, gaya, xb, w_proj_b[l].astype(BF16), w_out[l].astype(BF16), ln_mlp[l])
            gain = ln_final if last else jnp.ones((d,), F32)
            xb = _ffn(h2, w_up[l].astype(BF16), w_down[l].astype(BF16), x1, gain, final_norm=last)
        outs.append(xb)
    return jnp.stack(outs)
```

```python
import functools
import math

import jax
import jax.numpy as jnp
from jax import lax
from jax.experimental import pallas as pl
from jax.experimental.pallas import tpu as pltpu

A_GROUPS = 16
A_GROUP_DIM = 128
A_CHUNK = 128
B_HEADS = 16
B_HEAD_DIM = 128
MOBA_BLOCK = 256
MOBA_TOPK = 3
FAR_GROUP = 2
Q_STEP_BLOCKS = 2
REL_BUCKETS = 32
REL_MAX_DIST = 128
EPS = 1e-6
NEG = -1e30
LOG2E = math.log2(math.e)
NEG_LOG2 = NEG * LOG2E
FAR_MASKED = 2 * NEG_LOG2
REMOVED = -3e38

F32 = jnp.float32
BF16 = jnp.bfloat16

MIB = 1024 * 1024
NT_DIMS = (((1,), (1,)), ((), ()))


def _params(semantics, vmem_mib):
    return pltpu.CompilerParams(dimension_semantics=semantics, vmem_limit_bytes=vmem_mib * MIB)


def _resident(shape, index_map):
    return pl.BlockSpec(shape, index_map, pipeline_mode=pl.Buffered(1))


def _rms(xf, gain):
    return xf * lax.rsqrt(jnp.mean(xf * xf, axis=-1, keepdims=True) + EPS) * gain


def _norm_kernel(x_ref, g_ref, h_ref):
    h_ref[...] = _rms(x_ref[...], g_ref[...]).astype(h_ref.dtype)


def _norm(x, gain, tm=512):
    s, d = x.shape
    return pl.pallas_call(
        _norm_kernel,
        grid=(s // tm,),
        in_specs=[pl.BlockSpec((tm, d), lambda i: (i, 0)),
                  pl.BlockSpec((1, d), lambda i: (0, 0))],
        out_specs=pl.BlockSpec((tm, d), lambda i: (i, 0)),
        out_shape=jax.ShapeDtypeStruct((s, d), BF16),
        compiler_params=_params(("arbitrary",), 32),
        name="rms_norm",
    )(x, gain.reshape(1, d))


def _gelu(a):
    return 0.5 * a * (1.0 + lax.erf(a * math.sqrt(0.5)))


_EPILOGUES = {"gelu": _gelu, "none": lambda a: a, "sigmoid": jax.nn.sigmoid}


def _proj_kernel(h_ref, w_ref, o_ref, *, epilogue):
    acc = jnp.dot(h_ref[...], w_ref[...], preferred_element_type=F32)
    o_ref[...] = _EPILOGUES[epilogue](acc).astype(o_ref.dtype)


def _in_proj(h, w, col0, ncols, epilogue, tm=1024, tn=1024):
    s, d = h.shape
    jb = col0 // tn
    return pl.pallas_call(
        functools.partial(_proj_kernel, epilogue=epilogue),
        grid=(s // tm, ncols // tn),
        in_specs=[pl.BlockSpec((tm, d), lambda i, j: (i, 0)),
                  pl.BlockSpec((d, tn), lambda i, j: (0, j + jb))],
        out_specs=pl.BlockSpec((tm, tn), lambda i, j: (i, j)),
        out_shape=jax.ShapeDtypeStruct((s, ncols), BF16),
        compiler_params=_params(("arbitrary", "arbitrary"), 48),
        name="in_proj_" + epilogue,
    )(h, w)


def _gmlp_kernel(u_ref, v_ref, ga_ref, gain_ref, ws_ref, bt_ref, wpa_ref, o_ref, wm_ref, a_ref):
    tq = u_ref.shape[0]

    @pl.when(pl.program_id(0) == 0)
    def _():
        t = lax.broadcasted_iota(jnp.int32, (A_CHUNK, A_CHUNK), 0)
        s = lax.broadcasted_iota(jnp.int32, (A_CHUNK, A_CHUNK), 1)
        for g in range(A_GROUPS):
            wm_ref[g] = jnp.where(t >= s, ws_ref[g], 0.0).astype(BF16)

    def chunk(c, carry):
        r0 = pl.multiple_of(c * A_CHUNK, A_CHUNK)
        v = v_ref[pl.ds(r0, A_CHUNK), :].astype(F32)
        mu = jnp.mean(v, axis=-1, keepdims=True)
        d = v - mu
        var = jnp.mean(d * d, axis=-1, keepdims=True)
        vn = (d * lax.rsqrt(var + EPS) * gain_ref[...]).astype(BF16)
        for g in range(A_GROUPS):
            cols = slice(g * A_GROUP_DIM, (g + 1) * A_GROUP_DIM)
            sv = jnp.dot(wm_ref[g], vn[:, cols], preferred_element_type=F32) + bt_ref[:, g:g + 1]
            u = u_ref[pl.ds(r0, A_CHUNK), cols].astype(F32)
            a_ref[pl.ds(r0, A_CHUNK), cols] = (u * sv).astype(BF16)
        return carry

    lax.fori_loop(0, tq // A_CHUNK, chunk, 0)
    y = jnp.dot(a_ref[...], wpa_ref[...], preferred_element_type=F32)
    o_ref[...] = ga_ref[...].astype(F32) * y


def _gmlp(zuv, zg, gain, w_s, b_s, w_proj_a, tq=512):
    s = zuv.shape[0]
    aw = A_GROUPS * A_GROUP_DIM
    d = w_proj_a.shape[1]
    return pl.pallas_call(
        _gmlp_kernel,
        grid=(s // tq,),
        in_specs=[pl.BlockSpec((tq, aw), lambda i: (i, 0)),
                  pl.BlockSpec((tq, aw), lambda i: (i, 1)),
                  pl.BlockSpec((tq, d), lambda i: (i, 0)),
                  _resident((1, aw), lambda i: (0, 0)),
                  _resident((A_GROUPS, A_CHUNK, A_CHUNK), lambda i: (0, 0, 0)),
                  _resident((A_CHUNK, A_GROUPS), lambda i: (0, 0)),
                  _resident((aw, d), lambda i: (0, 0))],
        out_specs=pl.BlockSpec((tq, d), lambda i: (i, 0)),
        out_shape=jax.ShapeDtypeStruct((s, d), F32),
        scratch_shapes=[pltpu.VMEM((A_GROUPS, A_CHUNK, A_CHUNK), BF16),
                        pltpu.VMEM((tq, aw), BF16)],
        compiler_params=_params(("arbitrary",), 48),
        name="gmlp_mixer",
    )(zuv, zuv, zg, gain.reshape(1, aw), w_s, b_s.T, w_proj_a)


def _t5_bias_log2(dist, rb_ref, base):
    max_exact = REL_BUCKETS // 2
    n = jnp.maximum(dist, 0)
    nf = jnp.maximum(n, 1).astype(F32)
    large = max_exact + (jnp.log(nf / max_exact) / math.log(REL_MAX_DIST / max_exact)
                         * (REL_BUCKETS - max_exact)).astype(jnp.int32)
    large = jnp.minimum(large, REL_BUCKETS - 1)
    bucket = jnp.where(n < max_exact, n, large)
    out = jnp.zeros(dist.shape, F32)
    for b in range(REL_BUCKETS):
        out = jnp.where(bucket == b, rb_ref[base + b] * LOG2E, out)
    return out


def _moba_kernel(rb_ref, q_ref, k_ref, v_ref, o_ref, kmean_ref, vt_ref, tdiag_ref, rows_ref, s0_ref, s1_ref):
    L = MOBA_BLOCK
    G = FAR_GROUP
    QB = Q_STEP_BLOCKS
    W = QB * L
    nb = q_ref.shape[0] // L
    log2_l = L.bit_length() - 1
    base = pl.program_id(0) * REL_BUCKETS
    c_log2 = (B_HEAD_DIM ** -0.5) * LOG2E
    far_bias = rb_ref[base + REL_BUCKETS - 1] * LOG2E

    def prep(n, carry):
        r0 = pl.multiple_of(n * L, L)
        kmean_ref[pl.ds(n, 1), :] = jnp.mean(k_ref[pl.ds(r0, L), :].astype(F32), axis=0, keepdims=True)
        vt_ref[n] = v_ref[pl.ds(r0, L), :].astype(F32).T.astype(BF16)
        return carry

    lax.fori_loop(0, nb, prep, 0)

    kk = lax.broadcasted_iota(jnp.int32, (L, L), 0)
    qq = lax.broadcasted_iota(jnp.int32, (L, L), 1)
    d_own = qq - kk
    b_prev = _t5_bias_log2(d_own + L, rb_ref, base)
    b_own = jnp.where(d_own >= 0, _t5_bias_log2(d_own, rb_ref, base), NEG_LOG2)
    masked = jnp.full((L, L), NEG_LOG2, F32)
    window = [[b_prev, jnp.zeros((L, L), F32)], [b_own, b_prev], [masked, b_own], [masked, masked]]
    for r, tiles in enumerate(window):
        for c, tile in enumerate(tiles):
            tdiag_ref[r * L:(r + 1) * L, c * L:(c + 1) * L] = tile

    kmean = kmean_ref[...].astype(BF16)

    def gate(u, carry):
        q_u = q_ref[pl.ds(pl.multiple_of(u * W, W), W), :]
        gs = lax.dot_general(kmean, q_u, NT_DIMS, preferred_element_type=F32)
        blk = lax.broadcasted_iota(jnp.int32, gs.shape, 0)
        own = QB * u + lax.shift_right_logical(lax.broadcasted_iota(jnp.int32, gs.shape, 1), log2_l)
        gs = jnp.where(blk < own, gs, NEG)
        sel = jnp.zeros(gs.shape, F32)
        for _ in range(MOBA_TOPK):
            top = jnp.max(gs, axis=0, keepdims=True)
            idx = jnp.min(jnp.where(gs == top, blk, nb), axis=0, keepdims=True)
            hit = blk == idx
            sel = jnp.where(hit, jnp.where(top > NEG / 2, 1.0, 0.0), sel)
            gs = jnp.where(hit, REMOVED, gs)
        picked = sel > 0.5
        rows_ref[u] = jnp.where(blk < own - 1, jnp.where(picked, far_bias, FAR_MASKED),
                                jnp.where(blk == own - 1, jnp.where(picked, 0.0, FAR_MASKED), 0.0))
        return carry

    lax.fori_loop(0, nb // QB, gate, 0)

    def pv(vts, ps):
        out = None
        for vt, p in zip(vts, ps):
            d = jnp.dot(vt, p.astype(BF16), preferred_element_type=F32)
            out = d if out is None else out + d
        return out

    def qstep(u, carry):
        q0 = pl.multiple_of(u * W, W)
        q_u = q_ref[pl.ds(q0, W), :]
        rows_u = rows_ref.at[u]

        n_far = jnp.maximum(QB * u - 1, 0)

        def far_rows(g):
            jc = jnp.minimum(g, nb // G - 1) * G
            return jc, [jnp.where(g * G + b < n_far, rows_u[pl.ds(jc + b, 1), :], FAR_MASKED)
                        for b in range(G)]

        def produce(g, s_ref):
            jc, rows = far_rows(g)
            rj = pl.multiple_of(jc * L, G * L)
            s_t = lax.dot_general(k_ref[pl.ds(rj, G * L), :], q_u, NT_DIMS, preferred_element_type=F32)
            s_ref[...] = s_t
            return functools.reduce(jnp.maximum, [
                jnp.max(s_t[b * L:(b + 1) * L], axis=0, keepdims=True) * c_log2 + rows[b] for b in range(G)])

        def consume(g, s_ref, cand, m, l, acc):
            jc, rows = far_rows(g)
            m_new = jnp.maximum(m, cand)
            alpha = jnp.exp2(m - m_new)
            ps = [jnp.exp2(s_ref[b * L:(b + 1) * L, :] * c_log2 + (rows[b] - m_new)) for b in range(G)]
            l = alpha * l + functools.reduce(jnp.add, [jnp.sum(p, axis=0, keepdims=True) for p in ps])
            acc = alpha * acc + pv([vt_ref[jc + b] for b in range(G)], ps)
            return m_new, l, acc

        cand0 = produce(0, s0_ref)

        jw = jnp.maximum(QB * u - 1, 0)
        rw = pl.multiple_of(jw * L, L)
        toff = pl.multiple_of(jnp.where(u >= 1, 0, L), L)
        nw = QB + 1
        s_t = lax.dot_general(k_ref[pl.ds(rw, nw * L), :], q_u, NT_DIMS, preferred_element_type=F32)
        ts = [s_t[b * L:(b + 1) * L] * c_log2 + tdiag_ref[pl.ds(toff + b * L, L), :]
              + rows_u[pl.ds(jw + b, 1), :] for b in range(nw)]
        m = functools.reduce(jnp.maximum, [jnp.max(t, axis=0, keepdims=True) for t in ts])
        ps = [jnp.exp2(t - m) for t in ts]
        l = functools.reduce(jnp.add, [jnp.sum(p, axis=0, keepdims=True) for p in ps])
        acc = pv([vt_ref[jw + b] for b in range(nw)], ps)

        def pair(it, c):
            m, l, acc, cand_a = c
            g_a = 2 * it
            cand_b = produce(g_a + 1, s1_ref)
            m, l, acc = consume(g_a, s0_ref, cand_a, m, l, acc)
            cand_next = produce(g_a + 2, s0_ref)
            m, l, acc = consume(g_a + 1, s1_ref, cand_b, m, l, acc)
            return m, l, acc, cand_next

        n_groups = lax.shift_right_logical(n_far + (G - 1), G.bit_length() - 1)
        n_pairs = lax.shift_right_logical(n_groups + 1, 1)
        m, l, acc, _ = lax.fori_loop(0, n_pairs, pair, (m, l, acc, cand0))
        o_ref[pl.ds(q0, W), :] = (acc / l).T.astype(o_ref.dtype)
        return carry

    lax.fori_loop(0, nb // QB, qstep, 0)


def _moba(zqkv, rel_bias):
    s = zqkv.shape[0]
    dh = B_HEAD_DIM
    L = MOBA_BLOCK
    nb = s // L
    assert nb % Q_STEP_BLOCKS == 0 and nb % FAR_GROUP == 0 and nb >= Q_STEP_BLOCKS + 1
    assert FAR_GROUP & (FAR_GROUP - 1) == 0 and Q_STEP_BLOCKS == 2
    rb = rel_bias.T.reshape(-1).astype(F32)
    return pl.pallas_call(
        _moba_kernel,
        grid=(B_HEADS,),
        in_specs=[pl.BlockSpec(memory_space=pltpu.SMEM),
                  pl.BlockSpec((s, dh), lambda h: (0, h)),
                  pl.BlockSpec((s, dh), lambda h: (0, B_HEADS + h)),
                  pl.BlockSpec((s, dh), lambda h: (0, 2 * B_HEADS + h))],
        out_specs=pl.BlockSpec((s, dh), lambda h: (0, h)),
        out_shape=jax.ShapeDtypeStruct((s, B_HEADS * dh), BF16),
        scratch_shapes=[pltpu.VMEM((nb, dh), F32),
                        pltpu.VMEM((nb, dh, L), BF16),
                        pltpu.VMEM(((Q_STEP_BLOCKS + 2) * L, Q_STEP_BLOCKS * L), F32),
                        pltpu.VMEM((nb // Q_STEP_BLOCKS, nb, Q_STEP_BLOCKS * L), F32),
                        pltpu.VMEM((FAR_GROUP * L, Q_STEP_BLOCKS * L), F32),
                        pltpu.VMEM((FAR_GROUP * L, Q_STEP_BLOCKS * L), F32)],
        compiler_params=_params(("arbitrary",), 48),
        name="moba_attention",
    )(rb, zqkv, zqkv, zqkv)


def _merge_kernel(attn_ref, gb_ref, gaya_ref, x_ref, wpb_ref, wout_ref, g_ref, x1_ref, h2_ref):
    y_b = jnp.dot(attn_ref[...], wpb_ref[...], preferred_element_type=F32)
    merged = gaya_ref[...] + gb_ref[...].astype(F32) * y_b
    x1 = x_ref[...] + jnp.dot(merged.astype(BF16), wout_ref[...], preferred_element_type=F32)
    x1_ref[...] = x1
    h2_ref[...] = _rms(x1, g_ref[...]).astype(h2_ref.dtype)


def _merge(attn, zg, gaya, x, w_proj_b, w_out, gain, tq=512):
    s, d = x.shape
    bw = attn.shape[1]
    return pl.pallas_call(
        _merge_kernel,
        grid=(s // tq,),
        in_specs=[pl.BlockSpec((tq, bw), lambda i: (i, 0)),
                  pl.BlockSpec((tq, d), lambda i: (i, 1)),
                  pl.BlockSpec((tq, d), lambda i: (i, 0)),
                  pl.BlockSpec((tq, d), lambda i: (i, 0)),
                  _resident((bw, d), lambda i: (0, 0)),
                  _resident((d, d), lambda i: (0, 0)),
                  _resident((1, d), lambda i: (0, 0))],
        out_specs=[pl.BlockSpec((tq, d), lambda i: (i, 0)),
                   pl.BlockSpec((tq, d), lambda i: (i, 0))],
        out_shape=[jax.ShapeDtypeStruct((s, d), F32),
                   jax.ShapeDtypeStruct((s, d), BF16)],
        compiler_params=_params(("arbitrary",), 56),
        name="merge_out_proj",
    )(attn, zg, gaya, x, w_proj_b, w_out, gain.reshape(1, d))


def _ffn_kernel(h_ref, wu_ref, wd_ref, x1_ref, g_ref, o_ref, *, final_norm):
    j = pl.program_id(1)
    a = jnp.dot(h_ref[...], wu_ref[...], preferred_element_type=F32)
    a = jnp.square(jnp.maximum(a, 0.0)).astype(BF16)
    part = jnp.dot(a, wd_ref[...], preferred_element_type=F32)

    @pl.when(j == 0)
    def _():
        o_ref[...] = part

    @pl.when(j > 0)
    def _():
        o_ref[...] += part

    @pl.when(j == pl.num_programs(1) - 1)
    def _():
        x2 = x1_ref[...] + o_ref[...]
        o_ref[...] = _rms(x2, g_ref[...]) if final_norm else x2


def _ffn(h2, w_up, w_down, x1, gain, final_norm, tm=1024, tf=512):
    s, d = x1.shape
    dff = w_up.shape[1]
    return pl.pallas_call(
        functools.partial(_ffn_kernel, final_norm=final_norm),
        grid=(s // tm, dff // tf),
        in_specs=[_resident((tm, d), lambda i, j: (i, 0)),
                  pl.BlockSpec((d, tf), lambda i, j: (0, j)),
                  pl.BlockSpec((tf, d), lambda i, j: (j, 0)),
                  _resident((tm, d), lambda i, j: (i, 0)),
                  _resident((1, d), lambda i, j: (0, 0))],
        out_specs=pl.BlockSpec((tm, d), lambda i, j: (i, 0)),
        out_shape=jax.ShapeDtypeStruct((s, d), F32),
        compiler_params=_params(("arbitrary", "arbitrary"), 56),
        name="ffn_relu2",
    )(h2, w_up, w_down, x1, gain.reshape(1, d))


def kernel(x, ln_mix, w_in, a_v_gain, a_spatial, a_spatial_bias, w_proj_a, w_proj_b, w_out,
           rel_bias, ln_mlp, w_up, w_down, ln_final):
    bsz, s, d = x.shape
    depth = ln_mix.shape[0]
    aw = A_GROUPS * A_GROUP_DIM
    bw = B_HEADS * B_HEAD_DIM
    assert w_in.shape[2] == 2 * aw + 3 * bw + 2 * d
    outs = []
    for b in range(bsz):
        xb = x[b]
        for l in range(depth):
            last = l == depth - 1
            w_in_l = w_in[l].astype(BF16)
            h = _norm(xb, ln_mix[l])
            zuv = _in_proj(h, w_in_l, 0, 2 * aw, "gelu")
            zqkv = _in_proj(h, w_in_l, 2 * aw, 3 * bw, "none")
            zg = _in_proj(h, w_in_l, 2 * aw + 3 * bw, 2 * d, "sigmoid")
            gaya = _gmlp(zuv, zg, a_v_gain[l], a_spatial[l], a_spatial_bias[l], w_proj_a[l].astype(BF16))
            attn = _moba(zqkv, rel_bias)
            x1, h2 = _merge(attn, zg, gaya, xb, w_proj_b[l].astype(BF16), w_out[l].astype(BF16), ln_mlp[l])
            gain = ln_final if last else jnp.ones((d,), F32)
            xb = _ffn(h2, w_up[l].astype(BF16), w_down[l].astype(BF16), x1, gain, final_norm=last)
        outs.append(xb)
    return jnp.stack(outs)
```

```python
import functools
import math

import jax
import jax.numpy as jnp
from jax import lax
from jax.experimental import pallas as pl
from jax.experimental.pallas import tpu as pltpu

A_GROUPS = 16
A_GROUP_DIM = 128
A_CHUNK = 128
B_HEADS = 16
B_HEAD_DIM = 128
MOBA_BLOCK = 256
MOBA_TOPK = 3
FAR_GROUP = 2
Q_STEP_BLOCKS = 2
REL_BUCKETS = 32
REL_MAX_DIST = 128
EPS = 1e-6
NEG = -1e30
LOG2E = math.log2(math.e)
NEG_LOG2 = NEG * LOG2E
FAR_MASKED = 2 * NEG_LOG2
REMOVED = -3e38
K_SCALE = (B_HEAD_DIM ** -0.5) * LOG2E
BF16_SUBLANES = 16

F32 = jnp.float32
BF16 = jnp.bfloat16

MIB = 1024 * 1024
NT_DIMS = (((1,), (1,)), ((), ()))


def _params(semantics, vmem_mib):
    return pltpu.CompilerParams(dimension_semantics=semantics, vmem_limit_bytes=vmem_mib * MIB)


def _resident(shape, index_map):
    return pl.BlockSpec(shape, index_map, pipeline_mode=pl.Buffered(1))


def _rms(xf, gain):
    return xf * lax.rsqrt(jnp.mean(xf * xf, axis=-1, keepdims=True) + EPS) * gain


def _norm_kernel(x_ref, g_ref, h_ref):
    h_ref[...] = _rms(x_ref[...], g_ref[...]).astype(h_ref.dtype)


def _norm(x, gain, tm=512):
    s, d = x.shape
    return pl.pallas_call(
        _norm_kernel,
        grid=(s // tm,),
        in_specs=[pl.BlockSpec((tm, d), lambda i: (i, 0)),
                  pl.BlockSpec((1, d), lambda i: (0, 0))],
        out_specs=pl.BlockSpec((tm, d), lambda i: (i, 0)),
        out_shape=jax.ShapeDtypeStruct((s, d), BF16),
        compiler_params=_params(("arbitrary",), 32),
        name="rms_norm",
    )(x, gain.reshape(1, d))


def _gelu(a):
    return 0.5 * a * (1.0 + lax.erf(a * math.sqrt(0.5)))


_EPILOGUES = {"gelu": _gelu, "sigmoid": jax.nn.sigmoid}


def _proj_kernel(h_ref, w_ref, o_ref, *, epilogue):
    acc = jnp.dot(h_ref[...], w_ref[...], preferred_element_type=F32)
    o_ref[...] = _EPILOGUES[epilogue](acc).astype(o_ref.dtype)


def _proj_scale_kernel(h_ref, w_ref, scale_ref, o_ref):
    acc = jnp.dot(h_ref[...], w_ref[...], preferred_element_type=F32)
    o_ref[...] = (acc * scale_ref[...]).astype(o_ref.dtype)


def _in_proj(h, w, col0, ncols, epilogue=None, col_scale=None, tm=1024, tn=1024):
    s, d = h.shape
    jb = col0 // tn
    in_specs = [pl.BlockSpec((tm, d), lambda i, j: (i, 0)),
                pl.BlockSpec((d, tn), lambda i, j: (0, j + jb))]
    if col_scale is None:
        body, args, name = functools.partial(_proj_kernel, epilogue=epilogue), (h, w), epilogue
    else:
        in_specs.append(pl.BlockSpec((1, tn), lambda i, j: (0, j)))
        body, args, name = _proj_scale_kernel, (h, w, col_scale.reshape(1, ncols)), "scale"
    return pl.pallas_call(
        body,
        grid=(s // tm, ncols // tn),
        in_specs=in_specs,
        out_specs=pl.BlockSpec((tm, tn), lambda i, j: (i, j)),
        out_shape=jax.ShapeDtypeStruct((s, ncols), BF16),
        compiler_params=_params(("arbitrary", "arbitrary"), 48),
        name="in_proj_" + name,
    )(*args)


def _gmlp_kernel(u_ref, v_ref, ga_ref, gain_ref, ws_ref, bt_ref, wpa_ref, o_ref, wm_ref, a_ref):
    tq = u_ref.shape[0]

    @pl.when(pl.program_id(0) == 0)
    def _():
        t = lax.broadcasted_iota(jnp.int32, (A_CHUNK, A_CHUNK), 0)
        s = lax.broadcasted_iota(jnp.int32, (A_CHUNK, A_CHUNK), 1)
        for g in range(A_GROUPS):
            wm_ref[g] = jnp.where(t >= s, ws_ref[g], 0.0).astype(BF16)

    def chunk(c, carry):
        r0 = pl.multiple_of(c * A_CHUNK, A_CHUNK)
        v = v_ref[pl.ds(r0, A_CHUNK), :].astype(F32)
        mu = jnp.mean(v, axis=-1, keepdims=True)
        d = v - mu
        var = jnp.mean(d * d, axis=-1, keepdims=True)
        vn = (d * lax.rsqrt(var + EPS) * gain_ref[...]).astype(BF16)
        for g in range(A_GROUPS):
            cols = slice(g * A_GROUP_DIM, (g + 1) * A_GROUP_DIM)
            sv = jnp.dot(wm_ref[g], vn[:, cols], preferred_element_type=F32) + bt_ref[:, g:g + 1]
            u = u_ref[pl.ds(r0, A_CHUNK), cols].astype(F32)
            a_ref[pl.ds(r0, A_CHUNK), cols] = (u * sv).astype(BF16)
        return carry

    lax.fori_loop(0, tq // A_CHUNK, chunk, 0)
    y = jnp.dot(a_ref[...], wpa_ref[...], preferred_element_type=F32)
    o_ref[...] = ga_ref[...].astype(F32) * y


def _gmlp(zuv, zg, gain, w_s, b_s, w_proj_a, tq=512):
    s = zuv.shape[0]
    aw = A_GROUPS * A_GROUP_DIM
    d = w_proj_a.shape[1]
    return pl.pallas_call(
        _gmlp_kernel,
        grid=(s // tq,),
        in_specs=[pl.BlockSpec((tq, aw), lambda i: (i, 0)),
                  pl.BlockSpec((tq, aw), lambda i: (i, 1)),
                  pl.BlockSpec((tq, d), lambda i: (i, 0)),
                  _resident((1, aw), lambda i: (0, 0)),
                  _resident((A_GROUPS, A_CHUNK, A_CHUNK), lambda i: (0, 0, 0)),
                  _resident((A_CHUNK, A_GROUPS), lambda i: (0, 0)),
                  _resident((aw, d), lambda i: (0, 0))],
        out_specs=pl.BlockSpec((tq, d), lambda i: (i, 0)),
        out_shape=jax.ShapeDtypeStruct((s, d), F32),
        scratch_shapes=[pltpu.VMEM((A_GROUPS, A_CHUNK, A_CHUNK), BF16),
                        pltpu.VMEM((tq, aw), BF16)],
        compiler_params=_params(("arbitrary",), 48),
        name="gmlp_mixer",
    )(zuv, zuv, zg, gain.reshape(1, aw), w_s, b_s.T, w_proj_a)


def _t5_bias_log2(dist, rb_ref, base):
    max_exact = REL_BUCKETS // 2
    n = jnp.maximum(dist, 0)
    nf = jnp.maximum(n, 1).astype(F32)
    large = max_exact + (jnp.log(nf / max_exact) / math.log(REL_MAX_DIST / max_exact)
                         * (REL_BUCKETS - max_exact)).astype(jnp.int32)
    large = jnp.minimum(large, REL_BUCKETS - 1)
    bucket = jnp.where(n < max_exact, n, large)
    out = jnp.zeros(dist.shape, F32)
    for b in range(REL_BUCKETS):
        out = jnp.where(bucket == b, rb_ref[base + b] * LOG2E, out)
    return out


def _moba_kernel(rb_ref, q_ref, k_ref, v_ref, o_ref,
                 kmean_ref, vt_ref, tile_ref, rows_ref, s0_ref, s1_ref, m_ref, acc_ref):
    L = MOBA_BLOCK
    G = FAR_GROUP
    QB = Q_STEP_BLOCKS
    W = QB * L
    dh = B_HEAD_DIM
    nb = q_ref.shape[0] // L
    nq = nb // QB
    log2_l = L.bit_length() - 1
    base = pl.program_id(0) * REL_BUCKETS
    far_bias = rb_ref[base + REL_BUCKETS - 1] * LOG2E

    def prep(n, carry):
        r0 = pl.multiple_of(n * L, L)
        kmean_ref[pl.ds(n, 1), :] = jnp.mean(k_ref[pl.ds(r0, L), :].astype(F32), axis=0, keepdims=True)
        vt_ref[n, 0:dh, :] = v_ref[pl.ds(r0, L), :].astype(F32).T.astype(BF16)
        vt_ref[n, dh:, :] = jnp.ones((vt_ref.shape[1] - dh, L), BF16)
        return carry

    lax.fori_loop(0, nb, prep, 0)

    kk = lax.broadcasted_iota(jnp.int32, (L, L), 0)
    qq = lax.broadcasted_iota(jnp.int32, (L, L), 1)
    d_own = qq - kk
    b_prev = _t5_bias_log2(d_own + L, rb_ref, base)
    b_own = jnp.where(d_own >= 0, _t5_bias_log2(d_own, rb_ref, base), NEG_LOG2)
    masked = jnp.full((L, L), NEG_LOG2, F32)
    zero = jnp.zeros((L, L), F32)
    kinds = [[[zero, zero], [zero, zero]], [[zero, zero], [b_prev, zero]], [[b_own, b_prev], [masked, b_own]]]
    for kind, blocks in enumerate(kinds):
        for r, tiles in enumerate(blocks):
            for c, tile in enumerate(tiles):
                tile_ref[kind, r * L:(r + 1) * L, c * L:(c + 1) * L] = tile

    kmean = kmean_ref[...].astype(BF16)

    def gate(u, carry):
        q_u = q_ref[pl.ds(pl.multiple_of(u * W, W), W), :]
        gs = lax.dot_general(kmean, q_u, NT_DIMS, preferred_element_type=F32) * (1.0 / K_SCALE)
        blk = lax.broadcasted_iota(jnp.int32, gs.shape, 0)
        own = QB * u + lax.shift_right_logical(lax.broadcasted_iota(jnp.int32, gs.shape, 1), log2_l)
        gs = jnp.where(blk < own, gs, NEG)
        sel = jnp.zeros(gs.shape, F32)
        for _ in range(MOBA_TOPK):
            top = jnp.max(gs, axis=0, keepdims=True)
            idx = jnp.min(jnp.where(gs == top, blk, nb), axis=0, keepdims=True)
            hit = blk == idx
            sel = jnp.where(hit, jnp.where(top > NEG / 2, 1.0, 0.0), sel)
            gs = jnp.where(hit, REMOVED, gs)
        picked = sel > 0.5
        rows_ref[u] = jnp.where(blk < own - 1, jnp.where(picked, far_bias, FAR_MASKED),
                                jnp.where(blk == own - 1, jnp.where(picked, 0.0, FAR_MASKED), 0.0))
        return carry

    lax.fori_loop(0, nq, gate, 0)

    slots = (s0_ref, s1_ref)

    def item(t, ua):
        first = t <= ua
        u = jnp.where(first, ua, nq - 1 - ua)
        g = jnp.where(first, t, t - ua - 1)
        return u, g, jnp.where(first, 0, 1), jnp.clip(g - (u - 2), 0, 2)

    def rows_of(u, g):
        return [rows_ref[u, pl.ds(g * G + b, 1), :] for b in range(G)]

    def produce(t, ua):
        u, g, _, kind = item(t, ua)
        q_u = q_ref[pl.ds(pl.multiple_of(u * W, W), W), :]
        k_g = k_ref[pl.ds(pl.multiple_of(g * (G * L), G * L), G * L), :]
        s_t = lax.dot_general(k_g, q_u, NT_DIMS, preferred_element_type=F32) + tile_ref[kind]
        slots[t % 2][...] = s_t
        rows = rows_of(u, g)
        return functools.reduce(jnp.maximum, [
            jnp.max(s_t[b * L:(b + 1) * L], axis=0, keepdims=True) + rows[b] for b in range(G)])

    def consume(t, ua, cand):
        u, g, w, _ = item(t, ua)
        rows = rows_of(u, g)
        s_ref = slots[t % 2]
        m_old = m_ref[w]
        m_new = jnp.maximum(m_old, cand)
        pv = None
        for b in range(G):
            p = jnp.exp2(s_ref[b * L:(b + 1) * L, :] + (rows[b] - m_new)).astype(BF16)
            d = jnp.dot(vt_ref[g * G + b], p, preferred_element_type=F32)
            pv = d if pv is None else pv + d
        acc_ref[w] = jnp.exp2(m_old - m_new) * acc_ref[w] + pv
        m_ref[w] = m_new

    def step_pair(ua, carry):
        m_ref[...] = jnp.full(m_ref.shape, NEG_LOG2, F32)
        acc_ref[...] = jnp.zeros(acc_ref.shape, F32)
        cand = produce(0, ua)
        for t in range(nq + 1):
            cand_next = produce(t + 1, ua) if t < nq else None
            consume(t, ua, cand)
            cand = cand_next
        for w, u in ((0, ua), (1, nq - 1 - ua)):
            acc = acc_ref[w]
            o_ref[pl.ds(pl.multiple_of(u * W, W), W), :] = (acc[:dh] / acc[dh:dh + 1]).T.astype(o_ref.dtype)
        return carry

    lax.fori_loop(0, nq // 2, step_pair, 0)


def _moba(zqkv, rel_bias):
    s = zqkv.shape[0]
    dh = B_HEAD_DIM
    L = MOBA_BLOCK
    nb = s // L
    assert Q_STEP_BLOCKS == 2 and FAR_GROUP == 2 and nb % (2 * Q_STEP_BLOCKS) == 0
    w = Q_STEP_BLOCKS * L
    rb = rel_bias.T.reshape(-1).astype(F32)
    return pl.pallas_call(
        _moba_kernel,
        grid=(B_HEADS,),
        in_specs=[pl.BlockSpec(memory_space=pltpu.SMEM),
                  pl.BlockSpec((s, dh), lambda h: (0, h)),
                  pl.BlockSpec((s, dh), lambda h: (0, B_HEADS + h)),
                  pl.BlockSpec((s, dh), lambda h: (0, 2 * B_HEADS + h))],
        out_specs=pl.BlockSpec((s, dh), lambda h: (0, h)),
        out_shape=jax.ShapeDtypeStruct((s, B_HEADS * dh), BF16),
        scratch_shapes=[pltpu.VMEM((nb, dh), F32),
                        pltpu.VMEM((nb, dh + BF16_SUBLANES, L), BF16),
                        pltpu.VMEM((3, FAR_GROUP * L, w), F32),
                        pltpu.VMEM((nb // Q_STEP_BLOCKS, nb, w), F32),
                        pltpu.VMEM((FAR_GROUP * L, w), F32),
                        pltpu.VMEM((FAR_GROUP * L, w), F32),
                        pltpu.VMEM((2, 1, w), F32),
                        pltpu.VMEM((2, dh + BF16_SUBLANES, w), F32)],
        compiler_params=_params(("arbitrary",), 48),
        name="moba_attention",
    )(rb, zqkv, zqkv, zqkv)


def _merge_kernel(attn_ref, gb_ref, gaya_ref, x_ref, wpb_ref, wout_ref, g_ref, x1_ref, h2_ref):
    y_b = jnp.dot(attn_ref[...], wpb_ref[...], preferred_element_type=F32)
    merged = gaya_ref[...] + gb_ref[...].astype(F32) * y_b
    x1 = x_ref[...] + jnp.dot(merged.astype(BF16), wout_ref[...], preferred_element_type=F32)
    x1_ref[...] = x1
    h2_ref[...] = _rms(x1, g_ref[...]).astype(h2_ref.dtype)


def _merge(attn, zg, gaya, x, w_proj_b, w_out, gain, tq=512):
    s, d = x.shape
    bw = attn.shape[1]
    return pl.pallas_call(
        _merge_kernel,
        grid=(s // tq,),
        in_specs=[pl.BlockSpec((tq, bw), lambda i: (i, 0)),
                  pl.BlockSpec((tq, d), lambda i: (i, 1)),
                  pl.BlockSpec((tq, d), lambda i: (i, 0)),
                  pl.BlockSpec((tq, d), lambda i: (i, 0)),
                  _resident((bw, d), lambda i: (0, 0)),
                  _resident((d, d), lambda i: (0, 0)),
                  _resident((1, d), lambda i: (0, 0))],
        out_specs=[pl.BlockSpec((tq, d), lambda i: (i, 0)),
                   pl.BlockSpec((tq, d), lambda i: (i, 0))],
        out_shape=[jax.ShapeDtypeStruct((s, d), F32),
                   jax.ShapeDtypeStruct((s, d), BF16)],
        compiler_params=_params(("arbitrary",), 56),
        name="merge_out_proj",
    )(attn, zg, gaya, x, w_proj_b, w_out, gain.reshape(1, d))


def _ffn_kernel(h_ref, wu_ref, wd_ref, x1_ref, g_ref, o_ref, *, final_norm):
    j = pl.program_id(1)
    a = jnp.dot(h_ref[...], wu_ref[...], preferred_element_type=F32)
    a = jnp.square(jnp.maximum(a, 0.0)).astype(BF16)
    part = jnp.dot(a, wd_ref[...], preferred_element_type=F32)

    @pl.when(j == 0)
    def _():
        o_ref[...] = part

    @pl.when(j > 0)
    def _():
        o_ref[...] += part

    @pl.when(j == pl.num_programs(1) - 1)
    def _():
        x2 = x1_ref[...] + o_ref[...]
        o_ref[...] = _rms(x2, g_ref[...]) if final_norm else x2


def _ffn(h2, w_up, w_down, x1, gain, final_norm, tm=1024, tf=512):
    s, d = x1.shape
    dff = w_up.shape[1]
    return pl.pallas_call(
        functools.partial(_ffn_kernel, final_norm=final_norm),
        grid=(s // tm, dff // tf),
        in_specs=[_resident((tm, d), lambda i, j: (i, 0)),
                  pl.BlockSpec((d, tf), lambda i, j: (0, j)),
                  pl.BlockSpec((tf, d), lambda i, j: (j, 0)),
                  _resident((tm, d), lambda i, j: (i, 0)),
                  _resident((1, d), lambda i, j: (0, 0))],
        out_specs=pl.BlockSpec((tm, d), lambda i, j: (i, 0)),
        out_shape=jax.ShapeDtypeStruct((s, d), F32),
        compiler_params=_params(("arbitrary", "arbitrary"), 56),
        name="ffn_relu2",
    )(h2, w_up, w_down, x1, gain.reshape(1, d))


def kernel(x, ln_mix, w_in, a_v_gain, a_spatial, a_spatial_bias, w_proj_a, w_proj_b, w_out,
           rel_bias, ln_mlp, w_up, w_down, ln_final):
    bsz, s, d = x.shape
    depth = ln_mix.shape[0]
    aw = A_GROUPS * A_GROUP_DIM
    bw = B_HEADS * B_HEAD_DIM
    assert w_in.shape[2] == 2 * aw + 3 * bw + 2 * d
    outs = []
    for b in range(bsz):
        xb = x[b]
        for l in range(depth):
            last = l == depth - 1
            w_in_l = w_in[l].astype(BF16)
            h = _norm(xb, ln_mix[l])
            zuv = _in_proj(h, w_in_l, 0, 2 * aw, "gelu")
            qkv_scale = jnp.concatenate([jnp.ones((bw,), F32), jnp.full((bw,), K_SCALE, F32), jnp.ones((bw,), F32)])
            zqkv = _in_proj(h, w_in_l, 2 * aw, 3 * bw, col_scale=qkv_scale)
            zg = _in_proj(h, w_in_l, 2 * aw + 3 * bw, 2 * d, "sigmoid")
            gaya = _gmlp(zuv, zg, a_v_gain[l], a_spatial[l], a_spatial_bias[l], w_proj_a[l].astype(BF16))
            attn = _moba(zqkv, rel_bias)
            x1, h2 = _merge(attn, zg, gaya, xb, w_proj_b[l].astype(BF16), w_out[l].astype(BF16), ln_mlp[l])
            gain = ln_final if last else jnp.ones((d,), F32)
            xb = _ffn(h2, w_up[l].astype(BF16), w_down[l].astype(BF16), x1, gain, final_norm=last)
        outs.append(xb)
    return jnp.stack(outs)
```

```python
import functools
import math

import jax
import jax.numpy as jnp
from jax import lax
from jax.experimental import pallas as pl
from jax.experimental.pallas import tpu as pltpu

A_GROUPS = 16
A_GROUP_DIM = 128
A_CHUNK = 128
B_HEADS = 16
B_HEAD_DIM = 128
MOBA_BLOCK = 256
MOBA_TOPK = 3
FAR_GROUP = 2
Q_STEP_BLOCKS = 2
REL_BUCKETS = 32
REL_MAX_DIST = 128
EPS = 1e-6
NEG = -1e30
LOG2E = math.log2(math.e)
NEG_LOG2 = NEG * LOG2E
FAR_MASKED = 2 * NEG_LOG2
REMOVED = -3e38
K_SCALE = (B_HEAD_DIM ** -0.5) * LOG2E
BF16_SUBLANES = 16

F32 = jnp.float32
BF16 = jnp.bfloat16

MIB = 1024 * 1024
NT_DIMS = (((1,), (1,)), ((), ()))


def _params(semantics, vmem_mib):
    return pltpu.CompilerParams(dimension_semantics=semantics, vmem_limit_bytes=vmem_mib * MIB)


def _resident(shape, index_map):
    return pl.BlockSpec(shape, index_map, pipeline_mode=pl.Buffered(1))


def _rms(xf, gain):
    return xf * lax.rsqrt(jnp.mean(xf * xf, axis=-1, keepdims=True) + EPS) * gain


def _norm_kernel(x_ref, g_ref, h_ref):
    h_ref[...] = _rms(x_ref[...], g_ref[...]).astype(h_ref.dtype)


def _norm(x, gain, tm=512):
    s, d = x.shape
    return pl.pallas_call(
        _norm_kernel,
        grid=(s // tm,),
        in_specs=[pl.BlockSpec((tm, d), lambda i: (i, 0)),
                  pl.BlockSpec((1, d), lambda i: (0, 0))],
        out_specs=pl.BlockSpec((tm, d), lambda i: (i, 0)),
        out_shape=jax.ShapeDtypeStruct((s, d), BF16),
        compiler_params=_params(("arbitrary",), 32),
        name="rms_norm",
    )(x, gain.reshape(1, d))


def _gelu(a):
    return 0.5 * a * (1.0 + lax.erf(a * math.sqrt(0.5)))


_EPILOGUES = {"gelu": _gelu, "sigmoid": jax.nn.sigmoid}


def _weight_tile(w_ref, wbf_ref):
    @pl.when(pl.program_id(1) == 0)
    def _():
        wbf_ref[...] = w_ref[...].astype(BF16)

    return wbf_ref[...]


def _proj_kernel(h_ref, w_ref, o_ref, wbf_ref, *, epilogue):
    acc = jnp.dot(h_ref[...], _weight_tile(w_ref, wbf_ref), preferred_element_type=F32)
    o_ref[...] = _EPILOGUES[epilogue](acc).astype(o_ref.dtype)


def _proj_scale_kernel(h_ref, w_ref, scale_ref, o_ref, wbf_ref):
    acc = jnp.dot(h_ref[...], _weight_tile(w_ref, wbf_ref), preferred_element_type=F32)
    o_ref[...] = (acc * scale_ref[...]).astype(o_ref.dtype)


def _in_proj(h, w, col0, ncols, epilogue=None, col_scale=None, tm=1024, tn=1024):
    s, d = h.shape
    jb = col0 // tn
    in_specs = [pl.BlockSpec((tm, d), lambda j, i: (i, 0)),
                pl.BlockSpec((d, tn), lambda j, i: (0, j + jb))]
    if col_scale is None:
        body, args, name = functools.partial(_proj_kernel, epilogue=epilogue), (h, w), epilogue
    else:
        in_specs.append(pl.BlockSpec((1, tn), lambda j, i: (0, j)))
        body, args, name = _proj_scale_kernel, (h, w, col_scale.reshape(1, ncols)), "scale"
    return pl.pallas_call(
        body,
        grid=(ncols // tn, s // tm),
        in_specs=in_specs,
        out_specs=pl.BlockSpec((tm, tn), lambda j, i: (i, j)),
        out_shape=jax.ShapeDtypeStruct((s, ncols), BF16),
        scratch_shapes=[pltpu.VMEM((d, tn), BF16)],
        compiler_params=_params(("arbitrary", "arbitrary"), 48),
        name="in_proj_" + name,
    )(*args)


def _gmlp_kernel(u_ref, v_ref, ga_ref, gain_ref, ws_ref, bt_ref, wpa_ref, o_ref, wm_ref, a_ref):
    tq = u_ref.shape[0]

    @pl.when(pl.program_id(0) == 0)
    def _():
        t = lax.broadcasted_iota(jnp.int32, (A_CHUNK, A_CHUNK), 0)
        s = lax.broadcasted_iota(jnp.int32, (A_CHUNK, A_CHUNK), 1)
        for g in range(A_GROUPS):
            wm_ref[g] = jnp.where(t >= s, ws_ref[g], 0.0).astype(BF16)

    def chunk(c, carry):
        r0 = pl.multiple_of(c * A_CHUNK, A_CHUNK)
        v = v_ref[pl.ds(r0, A_CHUNK), :].astype(F32)
        mu = jnp.mean(v, axis=-1, keepdims=True)
        d = v - mu
        var = jnp.mean(d * d, axis=-1, keepdims=True)
        vn = (d * lax.rsqrt(var + EPS) * gain_ref[...]).astype(BF16)
        for g in range(A_GROUPS):
            cols = slice(g * A_GROUP_DIM, (g + 1) * A_GROUP_DIM)
            sv = jnp.dot(wm_ref[g], vn[:, cols], preferred_element_type=F32) + bt_ref[:, g:g + 1]
            u = u_ref[pl.ds(r0, A_CHUNK), cols].astype(F32)
            a_ref[pl.ds(r0, A_CHUNK), cols] = (u * sv).astype(BF16)
        return carry

    lax.fori_loop(0, tq // A_CHUNK, chunk, 0)
    y = jnp.dot(a_ref[...], wpa_ref[...], preferred_element_type=F32)
    o_ref[...] = ga_ref[...].astype(F32) * y


def _gmlp(zuv, zg, gain, w_s, b_s, w_proj_a, tq=512):
    s = zuv.shape[0]
    aw = A_GROUPS * A_GROUP_DIM
    d = w_proj_a.shape[1]
    return pl.pallas_call(
        _gmlp_kernel,
        grid=(s // tq,),
        in_specs=[pl.BlockSpec((tq, aw), lambda i: (i, 0)),
                  pl.BlockSpec((tq, aw), lambda i: (i, 1)),
                  pl.BlockSpec((tq, d), lambda i: (i, 0)),
                  _resident((1, aw), lambda i: (0, 0)),
                  _resident((A_GROUPS, A_CHUNK, A_CHUNK), lambda i: (0, 0, 0)),
                  _resident((A_CHUNK, A_GROUPS), lambda i: (0, 0)),
                  _resident((aw, d), lambda i: (0, 0))],
        out_specs=pl.BlockSpec((tq, d), lambda i: (i, 0)),
        out_shape=jax.ShapeDtypeStruct((s, d), F32),
        scratch_shapes=[pltpu.VMEM((A_GROUPS, A_CHUNK, A_CHUNK), BF16),
                        pltpu.VMEM((tq, aw), BF16)],
        compiler_params=_params(("arbitrary",), 48),
        name="gmlp_mixer",
    )(zuv, zuv, zg, gain.reshape(1, aw), w_s, b_s.T, w_proj_a)


def _t5_bias_log2(dist, rb_ref, base):
    max_exact = REL_BUCKETS // 2
    n = jnp.maximum(dist, 0)
    nf = jnp.maximum(n, 1).astype(F32)
    large = max_exact + (jnp.log(nf / max_exact) / math.log(REL_MAX_DIST / max_exact)
                         * (REL_BUCKETS - max_exact)).astype(jnp.int32)
    large = jnp.minimum(large, REL_BUCKETS - 1)
    bucket = jnp.where(n < max_exact, n, large)
    out = jnp.zeros(dist.shape, F32)
    for b in range(REL_BUCKETS):
        out = jnp.where(bucket == b, rb_ref[base + b] * LOG2E, out)
    return out


def _moba_kernel(rb_ref, q_ref, k_ref, v_ref, o_ref,
                 kmean_ref, vt_ref, tile_ref, rows_ref, s0_ref, s1_ref, m_ref, acc_ref):
    L = MOBA_BLOCK
    G = FAR_GROUP
    QB = Q_STEP_BLOCKS
    W = QB * L
    dh = B_HEAD_DIM
    nb = q_ref.shape[0] // L
    nq = nb // QB
    log2_l = L.bit_length() - 1
    base = pl.program_id(0) * REL_BUCKETS
    far_bias = rb_ref[base + REL_BUCKETS - 1] * LOG2E

    def prep(n, carry):
        r0 = pl.multiple_of(n * L, L)
        kmean_ref[pl.ds(n, 1), :] = jnp.mean(k_ref[pl.ds(r0, L), :].astype(F32), axis=0, keepdims=True)
        vt_ref[n, 0:dh, :] = v_ref[pl.ds(r0, L), :].astype(F32).T.astype(BF16)
        vt_ref[n, dh:, :] = jnp.ones((vt_ref.shape[1] - dh, L), BF16)
        return carry

    lax.fori_loop(0, nb, prep, 0)

    kk = lax.broadcasted_iota(jnp.int32, (L, L), 0)
    qq = lax.broadcasted_iota(jnp.int32, (L, L), 1)
    d_own = qq - kk
    b_prev = _t5_bias_log2(d_own + L, rb_ref, base)
    b_own = jnp.where(d_own >= 0, _t5_bias_log2(d_own, rb_ref, base), NEG_LOG2)
    masked = jnp.full((L, L), NEG_LOG2, F32)
    zero = jnp.zeros((L, L), F32)
    kinds = [[[zero, zero], [zero, zero]], [[zero, zero], [b_prev, zero]], [[b_own, b_prev], [masked, b_own]]]
    for kind, blocks in enumerate(kinds):
        for r, tiles in enumerate(blocks):
            for c, tile in enumerate(tiles):
                tile_ref[kind, r * L:(r + 1) * L, c * L:(c + 1) * L] = tile

    kmean = kmean_ref[...].astype(BF16)

    def gate(u, carry):
        q_u = q_ref[pl.ds(pl.multiple_of(u * W, W), W), :]
        gs = lax.dot_general(kmean, q_u, NT_DIMS, preferred_element_type=F32) * (1.0 / K_SCALE)
        blk = lax.broadcasted_iota(jnp.int32, gs.shape, 0)
        own = QB * u + lax.shift_right_logical(lax.broadcasted_iota(jnp.int32, gs.shape, 1), log2_l)
        gs = jnp.where(blk < own, gs, NEG)
        sel = jnp.zeros(gs.shape, F32)
        for _ in range(MOBA_TOPK):
            top = jnp.max(gs, axis=0, keepdims=True)
            idx = jnp.min(jnp.where(gs == top, blk, nb), axis=0, keepdims=True)
            hit = blk == idx
            sel = jnp.where(hit, jnp.where(top > NEG / 2, 1.0, 0.0), sel)
            gs = jnp.where(hit, REMOVED, gs)
        picked = sel > 0.5
        rows_ref[u] = jnp.where(blk < own - 1, jnp.where(picked, far_bias, FAR_MASKED),
                                jnp.where(blk == own - 1, jnp.where(picked, 0.0, FAR_MASKED), 0.0))
        return carry

    lax.fori_loop(0, nq, gate, 0)

    slots = (s0_ref, s1_ref)

    def item(t, ua):
        first = t <= ua
        u = jnp.where(first, ua, nq - 1 - ua)
        g = jnp.where(first, t, t - ua - 1)
        return u, g, jnp.where(first, 0, 1), jnp.clip(g - (u - 2), 0, 2)

    def rows_of(u, g):
        return [rows_ref[u, pl.ds(g * G + b, 1), :] for b in range(G)]

    def produce(t, ua):
        u, g, _, kind = item(t, ua)
        q_u = q_ref[pl.ds(pl.multiple_of(u * W, W), W), :]
        k_g = k_ref[pl.ds(pl.multiple_of(g * (G * L), G * L), G * L), :]
        s_t = lax.dot_general(k_g, q_u, NT_DIMS, preferred_element_type=F32) + tile_ref[kind]
        slots[t % 2][...] = s_t
        rows = rows_of(u, g)
        return functools.reduce(jnp.maximum, [
            jnp.max(s_t[b * L:(b + 1) * L], axis=0, keepdims=True) + rows[b] for b in range(G)])

    def consume(t, ua, cand):
        u, g, w, _ = item(t, ua)
        rows = rows_of(u, g)
        s_ref = slots[t % 2]
        m_old = m_ref[w]
        m_new = jnp.maximum(m_old, cand)
        pv = None
        for b in range(G):
            p = jnp.exp2(s_ref[b * L:(b + 1) * L, :] + (rows[b] - m_new)).astype(BF16)
            d = jnp.dot(vt_ref[g * G + b], p, preferred_element_type=F32)
            pv = d if pv is None else pv + d
        acc_ref[w] = jnp.exp2(m_old - m_new) * acc_ref[w] + pv
        m_ref[w] = m_new

    def step_pair(ua, carry):
        m_ref[...] = jnp.full(m_ref.shape, NEG_LOG2, F32)
        acc_ref[...] = jnp.zeros(acc_ref.shape, F32)
        cand = produce(0, ua)
        for t in range(nq + 1):
            cand_next = produce(t + 1, ua) if t < nq else None
            consume(t, ua, cand)
            cand = cand_next
        for w, u in ((0, ua), (1, nq - 1 - ua)):
            acc = acc_ref[w]
            o_ref[pl.ds(pl.multiple_of(u * W, W), W), :] = (acc[:dh] / acc[dh:dh + 1]).T.astype(o_ref.dtype)
        return carry

    lax.fori_loop(0, nq // 2, step_pair, 0)


def _moba(zqkv, rel_bias):
    s = zqkv.shape[0]
    dh = B_HEAD_DIM
    L = MOBA_BLOCK
    nb = s // L
    assert Q_STEP_BLOCKS == 2 and FAR_GROUP == 2 and nb % (2 * Q_STEP_BLOCKS) == 0
    w = Q_STEP_BLOCKS * L
    rb = rel_bias.T.reshape(-1).astype(F32)
    return pl.pallas_call(
        _moba_kernel,
        grid=(B_HEADS,),
        in_specs=[pl.BlockSpec(memory_space=pltpu.SMEM),
                  pl.BlockSpec((s, dh), lambda h: (0, h)),
                  pl.BlockSpec((s, dh), lambda h: (0, B_HEADS + h)),
                  pl.BlockSpec((s, dh), lambda h: (0, 2 * B_HEADS + h))],
        out_specs=pl.BlockSpec((s, dh), lambda h: (0, h)),
        out_shape=jax.ShapeDtypeStruct((s, B_HEADS * dh), BF16),
        scratch_shapes=[pltpu.VMEM((nb, dh), F32),
                        pltpu.VMEM((nb, dh + BF16_SUBLANES, L), BF16),
                        pltpu.VMEM((3, FAR_GROUP * L, w), F32),
                        pltpu.VMEM((nb // Q_STEP_BLOCKS, nb, w), F32),
                        pltpu.VMEM((FAR_GROUP * L, w), F32),
                        pltpu.VMEM((FAR_GROUP * L, w), F32),
                        pltpu.VMEM((2, 1, w), F32),
                        pltpu.VMEM((2, dh + BF16_SUBLANES, w), F32)],
        compiler_params=_params(("arbitrary",), 48),
        name="moba_attention",
    )(rb, zqkv, zqkv, zqkv)


def _merge_kernel(attn_ref, gb_ref, gaya_ref, x_ref, wpb_ref, wout_ref, g_ref, x1_ref, h2_ref):
    y_b = jnp.dot(attn_ref[...], wpb_ref[...], preferred_element_type=F32)
    merged = gaya_ref[...] + gb_ref[...].astype(F32) * y_b
    x1 = x_ref[...] + jnp.dot(merged.astype(BF16), wout_ref[...], preferred_element_type=F32)
    x1_ref[...] = x1
    h2_ref[...] = _rms(x1, g_ref[...]).astype(h2_ref.dtype)


def _merge(attn, zg, gaya, x, w_proj_b, w_out, gain, tq=512):
    s, d = x.shape
    bw = attn.shape[1]
    return pl.pallas_call(
        _merge_kernel,
        grid=(s // tq,),
        in_specs=[pl.BlockSpec((tq, bw), lambda i: (i, 0)),
                  pl.BlockSpec((tq, d), lambda i: (i, 1)),
                  pl.BlockSpec((tq, d), lambda i: (i, 0)),
                  pl.BlockSpec((tq, d), lambda i: (i, 0)),
                  _resident((bw, d), lambda i: (0, 0)),
                  _resident((d, d), lambda i: (0, 0)),
                  _resident((1, d), lambda i: (0, 0))],
        out_specs=[pl.BlockSpec((tq, d), lambda i: (i, 0)),
                   pl.BlockSpec((tq, d), lambda i: (i, 0))],
        out_shape=[jax.ShapeDtypeStruct((s, d), F32),
                   jax.ShapeDtypeStruct((s, d), BF16)],
        compiler_params=_params(("arbitrary",), 56),
        name="merge_out_proj",
    )(attn, zg, gaya, x, w_proj_b, w_out, gain.reshape(1, d))


def _ffn_kernel(h_ref, wu_ref, wd_ref, x1_ref, g_ref, o_ref, *, final_norm):
    j = pl.program_id(1)

    @pl.when(j == 0)
    def _():
        o_ref[...] = x1_ref[...]

    a = jnp.dot(h_ref[...], wu_ref[...], preferred_element_type=F32)
    a = jnp.square(jnp.maximum(a, 0.0)).astype(BF16)
    o_ref[...] += jnp.dot(a, wd_ref[...], preferred_element_type=F32)

    if final_norm:
        @pl.when(j == pl.num_programs(1) - 1)
        def _():
            o_ref[...] = _rms(o_ref[...], g_ref[...])


def _ffn(h2, w_up, w_down, x1, gain, final_norm, tm=1024, tf=1024):
    s, d = x1.shape
    dff = w_up.shape[1]
    return pl.pallas_call(
        functools.partial(_ffn_kernel, final_norm=final_norm),
        grid=(s // tm, dff // tf),
        in_specs=[_resident((tm, d), lambda i, j: (i, 0)),
                  pl.BlockSpec((d, tf), lambda i, j: (0, j)),
                  pl.BlockSpec((tf, d), lambda i, j: (j, 0)),
                  _resident((tm, d), lambda i, j: (i, 0)),
                  _resident((1, d), lambda i, j: (0, 0))],
        out_specs=pl.BlockSpec((tm, d), lambda i, j: (i, 0)),
        out_shape=jax.ShapeDtypeStruct((s, d), F32),
        compiler_params=_params(("arbitrary", "arbitrary"), 56),
        name="ffn_relu2",
    )(h2, w_up, w_down, x1, gain.reshape(1, d))


def kernel(x, ln_mix, w_in, a_v_gain, a_spatial, a_spatial_bias, w_proj_a, w_proj_b, w_out,
           rel_bias, ln_mlp, w_up, w_down, ln_final):
    bsz, s, d = x.shape
    depth = ln_mix.shape[0]
    aw = A_GROUPS * A_GROUP_DIM
    bw = B_HEADS * B_HEAD_DIM
    assert w_in.shape[2] == 2 * aw + 3 * bw + 2 * d
    outs = []
    for b in range(bsz):
        xb = x[b]
        for l in range(depth):
            last = l == depth - 1
            w_in_l = w_in[l]
            h = _norm(xb, ln_mix[l])
            zuv = _in_proj(h, w_in_l, 0, 2 * aw, "gelu")
            qkv_scale = jnp.concatenate([jnp.ones((bw,), F32), jnp.full((bw,), K_SCALE, F32), jnp.ones((bw,), F32)])
            zqkv = _in_proj(h, w_in_l, 2 * aw, 3 * bw, col_scale=qkv_scale)
            zg = _in_proj(h, w_in_l, 2 * aw + 3 * bw, 2 * d, "sigmoid")
            gaya = _gmlp(zuv, zg, a_v_gain[l], a_spatial[l], a_spatial_bias[l], w_proj_a[l].astype(BF16))
            attn = _moba(zqkv, rel_bias)
            x1, h2 = _merge(attn, zg, gaya, xb, w_proj_b[l].astype(BF16), w_out[l].astype(BF16), ln_mlp[l])
            gain = ln_final if last else jnp.ones((d,), F32)
            xb = _ffn(h2, w_up[l].astype(BF16), w_down[l].astype(BF16), x1, gain, final_norm=last)
        outs.append(xb)
    return jnp.stack(outs)
```

```python
import functools
import math

import jax
import jax.numpy as jnp
from jax import lax
from jax.experimental import pallas as pl
from jax.experimental.pallas import tpu as pltpu

A_GROUPS = 16
A_GROUP_DIM = 128
A_CHUNK = 128
B_HEADS = 16
B_HEAD_DIM = 128
MOBA_BLOCK = 256
MOBA_TOPK = 3
FAR_GROUP = 2
Q_STEP_BLOCKS = 2
GATE_STEPS = 4
PREP_UNROLL = 4
REL_BUCKETS = 32
REL_MAX_DIST = 128
EPS = 1e-6
NEG = -1e30
LOG2E = math.log2(math.e)
NEG_LOG2 = NEG * LOG2E
FAR_MASKED = 2 * NEG_LOG2
REMOVED = -3e38
K_SCALE = (B_HEAD_DIM ** -0.5) * LOG2E
BF16_SUBLANES = 16

F32 = jnp.float32
BF16 = jnp.bfloat16

MIB = 1024 * 1024
NT_DIMS = (((1,), (1,)), ((), ()))


def _params(semantics, vmem_mib):
    return pltpu.CompilerParams(dimension_semantics=semantics, vmem_limit_bytes=vmem_mib * MIB)


def _resident(shape, index_map):
    return pl.BlockSpec(shape, index_map, pipeline_mode=pl.Buffered(1))


def _rms(xf, gain):
    return xf * lax.rsqrt(jnp.mean(xf * xf, axis=-1, keepdims=True) + EPS) * gain


def _norm_kernel(x_ref, g_ref, h_ref):
    h_ref[...] = _rms(x_ref[...], g_ref[...]).astype(h_ref.dtype)


def _norm(x, gain, tm=512):
    s, d = x.shape
    return pl.pallas_call(
        _norm_kernel,
        grid=(s // tm,),
        in_specs=[pl.BlockSpec((tm, d), lambda i: (i, 0)),
                  pl.BlockSpec((1, d), lambda i: (0, 0))],
        out_specs=pl.BlockSpec((tm, d), lambda i: (i, 0)),
        out_shape=jax.ShapeDtypeStruct((s, d), BF16),
        compiler_params=_params(("arbitrary",), 32),
        name="rms_norm",
    )(x, gain.reshape(1, d))


def _gelu(a):
    return 0.5 * a * (1.0 + lax.erf(a * math.sqrt(0.5)))


_EPILOGUES = {"gelu": _gelu, "sigmoid": jax.nn.sigmoid}


def _weight_tile(w_ref, wbf_ref):
    @pl.when(pl.program_id(1) == 0)
    def _():
        wbf_ref[...] = w_ref[...].astype(BF16)

    return wbf_ref[...]


def _proj_kernel(h_ref, w_ref, o_ref, wbf_ref, *, epilogue):
    acc = jnp.dot(h_ref[...], _weight_tile(w_ref, wbf_ref), preferred_element_type=F32)
    o_ref[...] = _EPILOGUES[epilogue](acc).astype(o_ref.dtype)


def _proj_scale_kernel(h_ref, w_ref, scale_ref, o_ref, wbf_ref):
    acc = jnp.dot(h_ref[...], _weight_tile(w_ref, wbf_ref), preferred_element_type=F32)
    o_ref[...] = (acc * scale_ref[...]).astype(o_ref.dtype)


def _in_proj(h, w, col0, ncols, epilogue=None, col_scale=None, tm=1024, tn=1024):
    s, d = h.shape
    jb = col0 // tn
    in_specs = [pl.BlockSpec((tm, d), lambda j, i: (i, 0)),
                pl.BlockSpec((d, tn), lambda j, i: (0, j + jb))]
    if col_scale is None:
        body, args, name = functools.partial(_proj_kernel, epilogue=epilogue), (h, w), epilogue
    else:
        in_specs.append(pl.BlockSpec((1, tn), lambda j, i: (0, j)))
        body, args, name = _proj_scale_kernel, (h, w, col_scale.reshape(1, ncols)), "scale"
    return pl.pallas_call(
        body,
        grid=(ncols // tn, s // tm),
        in_specs=in_specs,
        out_specs=pl.BlockSpec((tm, tn), lambda j, i: (i, j)),
        out_shape=jax.ShapeDtypeStruct((s, ncols), BF16),
        scratch_shapes=[pltpu.VMEM((d, tn), BF16)],
        compiler_params=_params(("arbitrary", "arbitrary"), 48),
        name="in_proj_" + name,
    )(*args)


def _gmlp_kernel(u_ref, v_ref, ga_ref, gain_ref, ws_ref, bt_ref, wpa_ref, o_ref, wm_ref, a_ref):
    tq = u_ref.shape[0]

    @pl.when(pl.program_id(0) == 0)
    def _():
        t = lax.broadcasted_iota(jnp.int32, (A_CHUNK, A_CHUNK), 0)
        s = lax.broadcasted_iota(jnp.int32, (A_CHUNK, A_CHUNK), 1)
        for g in range(A_GROUPS):
            wm_ref[g] = jnp.where(t >= s, ws_ref[g], 0.0).astype(BF16)

    def chunk(c, carry):
        r0 = pl.multiple_of(c * A_CHUNK, A_CHUNK)
        v = v_ref[pl.ds(r0, A_CHUNK), :].astype(F32)
        mu = jnp.mean(v, axis=-1, keepdims=True)
        d = v - mu
        var = jnp.mean(d * d, axis=-1, keepdims=True)
        vn = (d * lax.rsqrt(var + EPS) * gain_ref[...]).astype(BF16)
        for g in range(A_GROUPS):
            cols = slice(g * A_GROUP_DIM, (g + 1) * A_GROUP_DIM)
            sv = jnp.dot(wm_ref[g], vn[:, cols], preferred_element_type=F32) + bt_ref[:, g:g + 1]
            u = u_ref[pl.ds(r0, A_CHUNK), cols].astype(F32)
            a_ref[pl.ds(r0, A_CHUNK), cols] = (u * sv).astype(BF16)
        return carry

    lax.fori_loop(0, tq // A_CHUNK, chunk, 0)
    y = jnp.dot(a_ref[...], wpa_ref[...], preferred_element_type=F32)
    o_ref[...] = ga_ref[...].astype(F32) * y


def _gmlp(zuv, zg, gain, w_s, b_s, w_proj_a, tq=512):
    s = zuv.shape[0]
    aw = A_GROUPS * A_GROUP_DIM
    d = w_proj_a.shape[1]
    return pl.pallas_call(
        _gmlp_kernel,
        grid=(s // tq,),
        in_specs=[pl.BlockSpec((tq, aw), lambda i: (i, 0)),
                  pl.BlockSpec((tq, aw), lambda i: (i, 1)),
                  pl.BlockSpec((tq, d), lambda i: (i, 0)),
                  _resident((1, aw), lambda i: (0, 0)),
                  _resident((A_GROUPS, A_CHUNK, A_CHUNK), lambda i: (0, 0, 0)),
                  _resident((A_CHUNK, A_GROUPS), lambda i: (0, 0)),
                  _resident((aw, d), lambda i: (0, 0))],
        out_specs=pl.BlockSpec((tq, d), lambda i: (i, 0)),
        out_shape=jax.ShapeDtypeStruct((s, d), F32),
        scratch_shapes=[pltpu.VMEM((A_GROUPS, A_CHUNK, A_CHUNK), BF16),
                        pltpu.VMEM((tq, aw), BF16)],
        compiler_params=_params(("arbitrary",), 48),
        name="gmlp_mixer",
    )(zuv, zuv, zg, gain.reshape(1, aw), w_s, b_s.T, w_proj_a)


def _t5_bias_log2(dist, rb_ref, base):
    max_exact = REL_BUCKETS // 2
    n = jnp.maximum(dist, 0)
    nf = jnp.maximum(n, 1).astype(F32)
    large = max_exact + (jnp.log(nf / max_exact) / math.log(REL_MAX_DIST / max_exact)
                         * (REL_BUCKETS - max_exact)).astype(jnp.int32)
    large = jnp.minimum(large, REL_BUCKETS - 1)
    bucket = jnp.where(n < max_exact, n, large)
    out = jnp.zeros(dist.shape, F32)
    for b in range(REL_BUCKETS):
        out = jnp.where(bucket == b, rb_ref[base + b] * LOG2E, out)
    return out


def _moba_kernel(rb_ref, q_ref, k_ref, v_ref, o_ref,
                 kmean_ref, vt_ref, tile_ref, rows_ref, s0_ref, s1_ref, m_ref, acc_ref):
    L = MOBA_BLOCK
    G = FAR_GROUP
    QB = Q_STEP_BLOCKS
    W = QB * L
    dh = B_HEAD_DIM
    nb = q_ref.shape[0] // L
    nq = nb // QB
    log2_l = L.bit_length() - 1
    base = pl.program_id(0) * REL_BUCKETS
    far_bias = rb_ref[base + REL_BUCKETS - 1] * LOG2E

    def prep(it, carry):
        for i in range(PREP_UNROLL):
            n = it * PREP_UNROLL + i
            r0 = pl.multiple_of(n * L, L)
            kmean_ref[pl.ds(n, 1), :] = jnp.mean(k_ref[pl.ds(r0, L), :].astype(F32), axis=0, keepdims=True)
            vt_ref[n, 0:dh, :] = v_ref[pl.ds(r0, L), :].astype(F32).T.astype(BF16)
            vt_ref[n, dh:, :] = jnp.ones((vt_ref.shape[1] - dh, L), BF16)
        return carry

    lax.fori_loop(0, nb // PREP_UNROLL, prep, 0)

    kk = lax.broadcasted_iota(jnp.int32, (L, L), 0)
    qq = lax.broadcasted_iota(jnp.int32, (L, L), 1)
    d_own = qq - kk
    b_prev = _t5_bias_log2(d_own + L, rb_ref, base)
    b_own = jnp.where(d_own >= 0, _t5_bias_log2(d_own, rb_ref, base), NEG_LOG2)
    masked = jnp.full((L, L), NEG_LOG2, F32)
    zero = jnp.zeros((L, L), F32)
    kinds = [[[zero, zero], [zero, zero]], [[zero, zero], [b_prev, zero]], [[b_own, b_prev], [masked, b_own]]]
    for kind, blocks in enumerate(kinds):
        for r, tiles in enumerate(blocks):
            for c, tile in enumerate(tiles):
                tile_ref[kind, r * L:(r + 1) * L, c * L:(c + 1) * L] = tile

    kmean = kmean_ref[...].astype(BF16)

    def gate(it, carry):
        u0 = it * GATE_STEPS
        q_c = q_ref[pl.ds(pl.multiple_of(u0 * W, GATE_STEPS * W), GATE_STEPS * W), :]
        gs = lax.dot_general(kmean, q_c, NT_DIMS, preferred_element_type=F32) * (1.0 / K_SCALE)
        blk = lax.broadcasted_iota(jnp.int32, gs.shape, 0)
        own = QB * u0 + lax.shift_right_logical(lax.broadcasted_iota(jnp.int32, gs.shape, 1), log2_l)
        gs = jnp.where(blk < own, gs, NEG)
        sel = jnp.zeros(gs.shape, F32)
        for _ in range(MOBA_TOPK):
            top = jnp.max(gs, axis=0, keepdims=True)
            idx = jnp.min(jnp.where(gs == top, blk, nb), axis=0, keepdims=True)
            hit = blk == idx
            sel = jnp.where(hit, jnp.where(top > NEG / 2, 1.0, 0.0), sel)
            gs = jnp.where(hit, REMOVED, gs)
        picked = sel > 0.5
        rows = jnp.where(blk < own - 1, jnp.where(picked, far_bias, FAR_MASKED),
                         jnp.where(blk == own - 1, jnp.where(picked, 0.0, FAR_MASKED), 0.0))
        for i in range(GATE_STEPS):
            rows_ref[u0 + i] = rows[:, i * W:(i + 1) * W]
        return carry

    lax.fori_loop(0, nq // GATE_STEPS, gate, 0)

    slots = (s0_ref, s1_ref)

    def item(t, ua):
        first = t <= ua
        u = jnp.where(first, ua, nq - 1 - ua)
        g = jnp.where(first, t, t - ua - 1)
        return u, g, jnp.where(first, 0, 1), jnp.clip(g - (u - 2), 0, 2)

    def rows_of(u, g):
        return [rows_ref[u, pl.ds(g * G + b, 1), :] for b in range(G)]

    def produce(t, ua):
        u, g, _, kind = item(t, ua)
        q_u = q_ref[pl.ds(pl.multiple_of(u * W, W), W), :]
        k_g = k_ref[pl.ds(pl.multiple_of(g * (G * L), G * L), G * L), :]
        s_t = lax.dot_general(k_g, q_u, NT_DIMS, preferred_element_type=F32) + tile_ref[kind]
        slots[t % 2][...] = s_t
        rows = rows_of(u, g)
        return functools.reduce(jnp.maximum, [
            jnp.max(s_t[b * L:(b + 1) * L], axis=0, keepdims=True) + rows[b] for b in range(G)])

    def consume(t, ua, cand):
        u, g, w, _ = item(t, ua)
        rows = rows_of(u, g)
        s_ref = slots[t % 2]
        st = (ua & 1) * 2 + w
        m_old = m_ref[st]
        m_new = jnp.maximum(m_old, cand)
        pv = None
        for b in range(G):
            p = jnp.exp2(s_ref[b * L:(b + 1) * L, :] + (rows[b] - m_new)).astype(BF16)
            d = jnp.dot(vt_ref[g * G + b], p, preferred_element_type=F32)
            pv = d if pv is None else pv + d
        acc_ref[st] = jnp.exp2(m_old - m_new) * acc_ref[st] + pv
        m_ref[st] = m_new

    def finish(ua, parity):
        for w, u in ((0, ua), (1, nq - 1 - ua)):
            acc = acc_ref[parity * 2 + w]
            o_ref[pl.ds(pl.multiple_of(u * W, W), W), :] = (acc[:dh] / acc[dh:dh + 1]).T.astype(o_ref.dtype)

    acc_ref[2:4] = jnp.ones((2,) + acc_ref.shape[1:], F32)

    def step_pair(ua, carry):
        parity = ua & 1
        m_ref[pl.ds(parity * 2, 2)] = jnp.full((2,) + m_ref.shape[1:], NEG_LOG2, F32)
        acc_ref[pl.ds(parity * 2, 2)] = jnp.zeros((2,) + acc_ref.shape[1:], F32)
        cand = produce(0, ua)
        finish(jnp.maximum(ua - 1, 0), 1 - parity)
        for t in range(nq + 1):
            cand_next = produce(t + 1, ua) if t < nq else None
            consume(t, ua, cand)
            cand = cand_next
        return carry

    lax.fori_loop(0, nq // 2, step_pair, 0)
    finish(nq // 2 - 1, (nq // 2 - 1) & 1)


def _moba(zqkv, rel_bias):
    s = zqkv.shape[0]
    dh = B_HEAD_DIM
    L = MOBA_BLOCK
    nb = s // L
    assert Q_STEP_BLOCKS == 2 and FAR_GROUP == 2 and nb % (2 * Q_STEP_BLOCKS) == 0
    assert nb % PREP_UNROLL == 0 and (nb // Q_STEP_BLOCKS) % GATE_STEPS == 0
    w = Q_STEP_BLOCKS * L
    rb = rel_bias.T.reshape(-1).astype(F32)
    return pl.pallas_call(
        _moba_kernel,
        grid=(B_HEADS,),
        in_specs=[pl.BlockSpec(memory_space=pltpu.SMEM),
                  pl.BlockSpec((s, dh), lambda h: (0, h)),
                  pl.BlockSpec((s, dh), lambda h: (0, B_HEADS + h)),
                  pl.BlockSpec((s, dh), lambda h: (0, 2 * B_HEADS + h))],
        out_specs=pl.BlockSpec((s, dh), lambda h: (0, h)),
        out_shape=jax.ShapeDtypeStruct((s, B_HEADS * dh), BF16),
        scratch_shapes=[pltpu.VMEM((nb, dh), F32),
                        pltpu.VMEM((nb, dh + BF16_SUBLANES, L), BF16),
                        pltpu.VMEM((3, FAR_GROUP * L, w), F32),
                        pltpu.VMEM((nb // Q_STEP_BLOCKS, nb, w), F32),
                        pltpu.VMEM((FAR_GROUP * L, w), F32),
                        pltpu.VMEM((FAR_GROUP * L, w), F32),
                        pltpu.VMEM((4, 1, w), F32),
                        pltpu.VMEM((4, dh + BF16_SUBLANES, w), F32)],
        compiler_params=_params(("arbitrary",), 48),
        name="moba_attention",
    )(rb, zqkv, zqkv, zqkv)


def _merge_kernel(attn_ref, gb_ref, gaya_ref, x_ref, wpb_ref, wout_ref, g_ref, x1_ref, h2_ref):
    y_b = jnp.dot(attn_ref[...], wpb_ref[...], preferred_element_type=F32)
    merged = gaya_ref[...] + gb_ref[...].astype(F32) * y_b
    x1 = x_ref[...] + jnp.dot(merged.astype(BF16), wout_ref[...], preferred_element_type=F32)
    x1_ref[...] = x1
    h2_ref[...] = _rms(x1, g_ref[...]).astype(h2_ref.dtype)


def _merge(attn, zg, gaya, x, w_proj_b, w_out, gain, tq=512):
    s, d = x.shape
    bw = attn.shape[1]
    return pl.pallas_call(
        _merge_kernel,
        grid=(s // tq,),
        in_specs=[pl.BlockSpec((tq, bw), lambda i: (i, 0)),
                  pl.BlockSpec((tq, d), lambda i: (i, 1)),
                  pl.BlockSpec((tq, d), lambda i: (i, 0)),
                  pl.BlockSpec((tq, d), lambda i: (i, 0)),
                  _resident((bw, d), lambda i: (0, 0)),
                  _resident((d, d), lambda i: (0, 0)),
                  _resident((1, d), lambda i: (0, 0))],
        out_specs=[pl.BlockSpec((tq, d), lambda i: (i, 0)),
                   pl.BlockSpec((tq, d), lambda i: (i, 0))],
        out_shape=[jax.ShapeDtypeStruct((s, d), F32),
                   jax.ShapeDtypeStruct((s, d), BF16)],
        compiler_params=_params(("arbitrary",), 56),
        name="merge_out_proj",
    )(attn, zg, gaya, x, w_proj_b, w_out, gain.reshape(1, d))


def _ffn_kernel(h_ref, wu_ref, wd_ref, x1_ref, g_ref, o_ref, *, final_norm):
    j = pl.program_id(1)

    @pl.when(j == 0)
    def _():
        o_ref[...] = x1_ref[...]

    a = jnp.dot(h_ref[...], wu_ref[...], preferred_element_type=F32)
    a = jnp.square(jnp.maximum(a, 0.0)).astype(BF16)
    o_ref[...] += jnp.dot(a, wd_ref[...], preferred_element_type=F32)

    if final_norm:
        @pl.when(j == pl.num_programs(1) - 1)
        def _():
            o_ref[...] = _rms(o_ref[...], g_ref[...])


def _ffn(h2, w_up, w_down, x1, gain, final_norm, tm=1024, tf=1024):
    s, d = x1.shape
    dff = w_up.shape[1]
    return pl.pallas_call(
        functools.partial(_ffn_kernel, final_norm=final_norm),
        grid=(s // tm, dff // tf),
        in_specs=[_resident((tm, d), lambda i, j: (i, 0)),
                  pl.BlockSpec((d, tf), lambda i, j: (0, j)),
                  pl.BlockSpec((tf, d), lambda i, j: (j, 0)),
                  _resident((tm, d), lambda i, j: (i, 0)),
                  _resident((1, d), lambda i, j: (0, 0))],
        out_specs=pl.BlockSpec((tm, d), lambda i, j: (i, 0)),
        out_shape=jax.ShapeDtypeStruct((s, d), F32),
        compiler_params=_params(("arbitrary", "arbitrary"), 56),
        name="ffn_relu2",
    )(h2, w_up, w_down, x1, gain.reshape(1, d))


def kernel(x, ln_mix, w_in, a_v_gain, a_spatial, a_spatial_bias, w_proj_a, w_proj_b, w_out,
           rel_bias, ln_mlp, w_up, w_down, ln_final):
    bsz, s, d = x.shape
    depth = ln_mix.shape[0]
    aw = A_GROUPS * A_GROUP_DIM
    bw = B_HEADS * B_HEAD_DIM
    assert w_in.shape[2] == 2 * aw + 3 * bw + 2 * d
    outs = []
    for b in range(bsz):
        xb = x[b]
        for l in range(depth):
            last = l == depth - 1
            w_in_l = w_in[l]
            h = _norm(xb, ln_mix[l])
            zuv = _in_proj(h, w_in_l, 0, 2 * aw, "gelu")
            qkv_scale = jnp.concatenate([jnp.ones((bw,), F32), jnp.full((bw,), K_SCALE, F32), jnp.ones((bw,), F32)])
            zqkv = _in_proj(h, w_in_l, 2 * aw, 3 * bw, col_scale=qkv_scale)
            zg = _in_proj(h, w_in_l, 2 * aw + 3 * bw, 2 * d, "sigmoid")
            gaya = _gmlp(zuv, zg, a_v_gain[l], a_spatial[l], a_spatial_bias[l], w_proj_a[l].astype(BF16))
            attn = _moba(zqkv, rel_bias)
            x1, h2 = _merge(attn, zg, gaya, xb, w_proj_b[l].astype(BF16), w_out[l].astype(BF16), ln_mlp[l])
            gain = ln_final if last else jnp.ones((d,), F32)
            xb = _ffn(h2, w_up[l].astype(BF16), w_down[l].astype(BF16), x1, gain, final_norm=last)
        outs.append(xb)
    return jnp.stack(outs)
```

```python
import functools
import math

import jax
import jax.numpy as jnp
from jax import lax
from jax.experimental import pallas as pl
from jax.experimental.pallas import tpu as pltpu

A_GROUPS = 16
A_GROUP_DIM = 128
A_CHUNK = 128
B_HEADS = 16
B_HEAD_DIM = 128
MOBA_BLOCK = 256
MOBA_TOPK = 3
FAR_GROUP = 2
Q_STEP_BLOCKS = 2
SCORE_SLOTS = 2
GATE_STEPS = 4
PREP_UNROLL = 4
REL_BUCKETS = 32
REL_MAX_DIST = 128
EPS = 1e-6
NEG = -1e30
LOG2E = math.log2(math.e)
NEG_LOG2 = NEG * LOG2E
FAR_MASKED = 2 * NEG_LOG2
REMOVED = -3e38
K_SCALE = (B_HEAD_DIM ** -0.5) * LOG2E
BF16_SUBLANES = 16

F32 = jnp.float32
BF16 = jnp.bfloat16

MIB = 1024 * 1024
NT_DIMS = (((1,), (1,)), ((), ()))


def _params(semantics, vmem_mib):
    return pltpu.CompilerParams(dimension_semantics=semantics, vmem_limit_bytes=vmem_mib * MIB)


def _resident(shape, index_map):
    return pl.BlockSpec(shape, index_map, pipeline_mode=pl.Buffered(1))


def _rms(xf, gain):
    return xf * lax.rsqrt(jnp.mean(xf * xf, axis=-1, keepdims=True) + EPS) * gain


def _norm_kernel(x_ref, g_ref, h_ref):
    h_ref[...] = _rms(x_ref[...], g_ref[...]).astype(h_ref.dtype)


def _norm(x, gain, tm=512):
    s, d = x.shape
    return pl.pallas_call(
        _norm_kernel,
        grid=(s // tm,),
        in_specs=[pl.BlockSpec((tm, d), lambda i: (i, 0)),
                  pl.BlockSpec((1, d), lambda i: (0, 0))],
        out_specs=pl.BlockSpec((tm, d), lambda i: (i, 0)),
        out_shape=jax.ShapeDtypeStruct((s, d), BF16),
        compiler_params=_params(("arbitrary",), 32),
        name="rms_norm",
    )(x, gain.reshape(1, d))


def _gelu(a):
    return 0.5 * a * (1.0 + lax.erf(a * math.sqrt(0.5)))


_EPILOGUES = {"gelu": _gelu, "sigmoid": jax.nn.sigmoid}


def _weight_tile(w_ref, wbf_ref):
    @pl.when(pl.program_id(1) == 0)
    def _():
        wbf_ref[...] = w_ref[...].astype(BF16)

    return wbf_ref[...]


def _proj_kernel(h_ref, w_ref, o_ref, wbf_ref, *, epilogue):
    acc = jnp.dot(h_ref[...], _weight_tile(w_ref, wbf_ref), preferred_element_type=F32)
    o_ref[...] = _EPILOGUES[epilogue](acc).astype(o_ref.dtype)


def _proj_scale_kernel(h_ref, w_ref, scale_ref, o_ref, wbf_ref):
    acc = jnp.dot(h_ref[...], _weight_tile(w_ref, wbf_ref), preferred_element_type=F32)
    o_ref[...] = (acc * scale_ref[...]).astype(o_ref.dtype)


def _in_proj(h, w, col0, ncols, epilogue=None, col_scale=None, tm=1024, tn=1024):
    s, d = h.shape
    jb = col0 // tn
    in_specs = [pl.BlockSpec((tm, d), lambda j, i: (i, 0)),
                pl.BlockSpec((d, tn), lambda j, i: (0, j + jb))]
    if col_scale is None:
        body, args, name = functools.partial(_proj_kernel, epilogue=epilogue), (h, w), epilogue
    else:
        in_specs.append(pl.BlockSpec((1, tn), lambda j, i: (0, j)))
        body, args, name = _proj_scale_kernel, (h, w, col_scale.reshape(1, ncols)), "scale"
    return pl.pallas_call(
        body,
        grid=(ncols // tn, s // tm),
        in_specs=in_specs,
        out_specs=pl.BlockSpec((tm, tn), lambda j, i: (i, j)),
        out_shape=jax.ShapeDtypeStruct((s, ncols), BF16),
        scratch_shapes=[pltpu.VMEM((d, tn), BF16)],
        compiler_params=_params(("arbitrary", "arbitrary"), 48),
        name="in_proj_" + name,
    )(*args)


def _gmlp_kernel(u_ref, v_ref, ga_ref, gain_ref, ws_ref, bt_ref, wpa_ref, o_ref, wm_ref, a_ref):
    tq = u_ref.shape[0]

    @pl.when(pl.program_id(0) == 0)
    def _():
        t = lax.broadcasted_iota(jnp.int32, (A_CHUNK, A_CHUNK), 0)
        s = lax.broadcasted_iota(jnp.int32, (A_CHUNK, A_CHUNK), 1)
        for g in range(A_GROUPS):
            wm_ref[g] = jnp.where(t >= s, ws_ref[g], 0.0).astype(BF16)

    def chunk(c, carry):
        r0 = pl.multiple_of(c * A_CHUNK, A_CHUNK)
        v = v_ref[pl.ds(r0, A_CHUNK), :].astype(F32)
        mu = jnp.mean(v, axis=-1, keepdims=True)
        d = v - mu
        var = jnp.mean(d * d, axis=-1, keepdims=True)
        vn = (d * lax.rsqrt(var + EPS) * gain_ref[...]).astype(BF16)
        for g in range(A_GROUPS):
            cols = slice(g * A_GROUP_DIM, (g + 1) * A_GROUP_DIM)
            sv = jnp.dot(wm_ref[g], vn[:, cols], preferred_element_type=F32) + bt_ref[:, g:g + 1]
            u = u_ref[pl.ds(r0, A_CHUNK), cols].astype(F32)
            a_ref[pl.ds(r0, A_CHUNK), cols] = (u * sv).astype(BF16)
        return carry

    lax.fori_loop(0, tq // A_CHUNK, chunk, 0)
    y = jnp.dot(a_ref[...], wpa_ref[...], preferred_element_type=F32)
    o_ref[...] = ga_ref[...].astype(F32) * y


def _gmlp(zuv, zg, gain, w_s, b_s, w_proj_a, tq=512):
    s = zuv.shape[0]
    aw = A_GROUPS * A_GROUP_DIM
    d = w_proj_a.shape[1]
    return pl.pallas_call(
        _gmlp_kernel,
        grid=(s // tq,),
        in_specs=[pl.BlockSpec((tq, aw), lambda i: (i, 0)),
                  pl.BlockSpec((tq, aw), lambda i: (i, 1)),
                  pl.BlockSpec((tq, d), lambda i: (i, 0)),
                  _resident((1, aw), lambda i: (0, 0)),
                  _resident((A_GROUPS, A_CHUNK, A_CHUNK), lambda i: (0, 0, 0)),
                  _resident((A_CHUNK, A_GROUPS), lambda i: (0, 0)),
                  _resident((aw, d), lambda i: (0, 0))],
        out_specs=pl.BlockSpec((tq, d), lambda i: (i, 0)),
        out_shape=jax.ShapeDtypeStruct((s, d), F32),
        scratch_shapes=[pltpu.VMEM((A_GROUPS, A_CHUNK, A_CHUNK), BF16),
                        pltpu.VMEM((tq, aw), BF16)],
        compiler_params=_params(("arbitrary",), 48),
        name="gmlp_mixer",
    )(zuv, zuv, zg, gain.reshape(1, aw), w_s, b_s.T, w_proj_a)


def _t5_bias_log2(dist, rb_ref, base):
    max_exact = REL_BUCKETS // 2
    n = jnp.maximum(dist, 0)
    nf = jnp.maximum(n, 1).astype(F32)
    large = max_exact + (jnp.log(nf / max_exact) / math.log(REL_MAX_DIST / max_exact)
                         * (REL_BUCKETS - max_exact)).astype(jnp.int32)
    large = jnp.minimum(large, REL_BUCKETS - 1)
    bucket = jnp.where(n < max_exact, n, large)
    out = jnp.zeros(dist.shape, F32)
    for b in range(REL_BUCKETS):
        out = jnp.where(bucket == b, rb_ref[base + b] * LOG2E, out)
    return out


def _moba_kernel(rb_ref, q_ref, k_ref, v_ref, o_ref,
                 kmean_ref, vt_ref, tile_ref, rows_ref, m_ref, acc_ref, *slots):
    L = MOBA_BLOCK
    G = FAR_GROUP
    QB = Q_STEP_BLOCKS
    W = QB * L
    dh = B_HEAD_DIM
    nb = q_ref.shape[0] // L
    nq = nb // QB
    log2_l = L.bit_length() - 1
    base = pl.program_id(0) * REL_BUCKETS
    far_bias = rb_ref[base + REL_BUCKETS - 1] * LOG2E

    def prep(it, carry):
        for i in range(PREP_UNROLL):
            n = it * PREP_UNROLL + i
            r0 = pl.multiple_of(n * L, L)
            kmean_ref[pl.ds(n, 1), :] = jnp.mean(k_ref[pl.ds(r0, L), :].astype(F32), axis=0, keepdims=True)
            vt_ref[n, 0:dh, :] = v_ref[pl.ds(r0, L), :].astype(F32).T.astype(BF16)
            vt_ref[n, dh:, :] = jnp.ones((vt_ref.shape[1] - dh, L), BF16)
        return carry

    lax.fori_loop(0, nb // PREP_UNROLL, prep, 0)

    kk = lax.broadcasted_iota(jnp.int32, (L, L), 0)
    qq = lax.broadcasted_iota(jnp.int32, (L, L), 1)
    d_own = qq - kk
    b_prev = _t5_bias_log2(d_own + L, rb_ref, base)
    b_own = jnp.where(d_own >= 0, _t5_bias_log2(d_own, rb_ref, base), NEG_LOG2)
    masked = jnp.full((L, L), NEG_LOG2, F32)
    zero = jnp.zeros((L, L), F32)
    kinds = [[[zero, zero], [zero, zero]], [[zero, zero], [b_prev, zero]], [[b_own, b_prev], [masked, b_own]]]
    for kind, blocks in enumerate(kinds):
        for r, tiles in enumerate(blocks):
            for c, tile in enumerate(tiles):
                tile_ref[kind, r * L:(r + 1) * L, c * L:(c + 1) * L] = tile

    kmean = kmean_ref[...].astype(BF16)

    def gate(it, carry):
        u0 = it * GATE_STEPS
        q_c = q_ref[pl.ds(pl.multiple_of(u0 * W, GATE_STEPS * W), GATE_STEPS * W), :]
        gs = lax.dot_general(kmean, q_c, NT_DIMS, preferred_element_type=F32) * (1.0 / K_SCALE)
        blk = lax.broadcasted_iota(jnp.int32, gs.shape, 0)
        own = QB * u0 + lax.shift_right_logical(lax.broadcasted_iota(jnp.int32, gs.shape, 1), log2_l)
        gs = jnp.where(blk < own, gs, NEG)
        sel = jnp.zeros(gs.shape, F32)
        for _ in range(MOBA_TOPK):
            top = jnp.max(gs, axis=0, keepdims=True)
            idx = jnp.min(jnp.where(gs == top, blk, nb), axis=0, keepdims=True)
            hit = blk == idx
            sel = jnp.where(hit, jnp.where(top > NEG / 2, 1.0, 0.0), sel)
            gs = jnp.where(hit, REMOVED, gs)
        picked = sel > 0.5
        rows = jnp.where(blk < own - 1, jnp.where(picked, far_bias, FAR_MASKED),
                         jnp.where(blk == own - 1, jnp.where(picked, 0.0, FAR_MASKED), 0.0))
        for i in range(GATE_STEPS):
            rows_ref[u0 + i] = rows[:, i * W:(i + 1) * W]
        return carry

    lax.fori_loop(0, nq // GATE_STEPS, gate, 0)

    def item(t, ua):
        first = t <= ua
        u = jnp.where(first, ua, nq - 1 - ua)
        g = jnp.where(first, t, t - ua - 1)
        return u, g, jnp.where(first, 0, 1), jnp.clip(g - (u - 2), 0, 2)

    def rows_of(u, g):
        return [rows_ref[u, pl.ds(g * G + b, 1), :] for b in range(G)]

    def produce(t, ua):
        u, g, _, kind = item(t, ua)
        q_u = q_ref[pl.ds(pl.multiple_of(u * W, W), W), :]
        k_g = k_ref[pl.ds(pl.multiple_of(g * (G * L), G * L), G * L), :]
        s_t = lax.dot_general(k_g, q_u, NT_DIMS, preferred_element_type=F32) + tile_ref[kind]
        slots[t % len(slots)][...] = s_t
        rows = rows_of(u, g)
        return functools.reduce(jnp.maximum, [
            jnp.max(s_t[b * L:(b + 1) * L], axis=0, keepdims=True) + rows[b] for b in range(G)])

    def consume(t, ua, cand):
        u, g, w, _ = item(t, ua)
        rows = rows_of(u, g)
        s_ref = slots[t % len(slots)]
        st = (ua & 1) * 2 + w
        m_old = m_ref[st]
        m_new = jnp.maximum(m_old, cand)
        pv = None
        for b in range(G):
            p = jnp.exp2(s_ref[b * L:(b + 1) * L, :] + (rows[b] - m_new)).astype(BF16)
            d = jnp.dot(vt_ref[g * G + b], p, preferred_element_type=F32)
            pv = d if pv is None else pv + d
        acc_ref[st] = jnp.exp2(m_old - m_new) * acc_ref[st] + pv
        m_ref[st] = m_new

    def finish(ua, parity):
        for w, u in ((0, ua), (1, nq - 1 - ua)):
            acc = acc_ref[parity * 2 + w]
            o_ref[pl.ds(pl.multiple_of(u * W, W), W), :] = (acc[:dh] / acc[dh:dh + 1]).T.astype(o_ref.dtype)

    acc_ref[2:4] = jnp.ones((2,) + acc_ref.shape[1:], F32)

    def step_pair(ua, carry):
        parity = ua & 1
        m_ref[pl.ds(parity * 2, 2)] = jnp.full((2,) + m_ref.shape[1:], NEG_LOG2, F32)
        acc_ref[pl.ds(parity * 2, 2)] = jnp.zeros((2,) + acc_ref.shape[1:], F32)
        cand = produce(0, ua)
        finish(jnp.maximum(ua - 1, 0), 1 - parity)
        for t in range(nq + 1):
            cand_next = produce(t + 1, ua) if t < nq else None
            consume(t, ua, cand)
            cand = cand_next
        return carry

    lax.fori_loop(0, nq // 2, step_pair, 0)
    finish(nq // 2 - 1, (nq // 2 - 1) & 1)


def _moba(zqkv, rel_bias):
    s = zqkv.shape[0]
    dh = B_HEAD_DIM
    L = MOBA_BLOCK
    nb = s // L
    assert Q_STEP_BLOCKS == 2 and FAR_GROUP == 2 and nb % (2 * Q_STEP_BLOCKS) == 0
    assert nb % PREP_UNROLL == 0 and (nb // Q_STEP_BLOCKS) % GATE_STEPS == 0
    w = Q_STEP_BLOCKS * L
    rb = rel_bias.T.reshape(-1).astype(F32)
    return pl.pallas_call(
        _moba_kernel,
        grid=(B_HEADS,),
        in_specs=[pl.BlockSpec(memory_space=pltpu.SMEM),
                  pl.BlockSpec((s, dh), lambda h: (0, h)),
                  pl.BlockSpec((s, dh), lambda h: (0, B_HEADS + h)),
                  pl.BlockSpec((s, dh), lambda h: (0, 2 * B_HEADS + h))],
        out_specs=pl.BlockSpec((s, dh), lambda h: (0, h)),
        out_shape=jax.ShapeDtypeStruct((s, B_HEADS * dh), BF16),
        scratch_shapes=[pltpu.VMEM((nb, dh), F32),
                        pltpu.VMEM((nb, dh + BF16_SUBLANES, L), BF16),
                        pltpu.VMEM((3, FAR_GROUP * L, w), F32),
                        pltpu.VMEM((nb // Q_STEP_BLOCKS, nb, w), F32),
                        pltpu.VMEM((4, 1, w), F32),
                        pltpu.VMEM((4, dh + BF16_SUBLANES, w), F32)]
        + [pltpu.VMEM((FAR_GROUP * L, w), F32)] * SCORE_SLOTS,

        compiler_params=_params(("arbitrary",), 48),
        name="moba_attention",
    )(rb, zqkv, zqkv, zqkv)


def _merge_kernel(attn_ref, gb_ref, gaya_ref, x_ref, wpb_ref, wout_ref, g_ref, x1_ref, h2_ref):
    y_b = jnp.dot(attn_ref[...], wpb_ref[...], preferred_element_type=F32)
    merged = gaya_ref[...] + gb_ref[...].astype(F32) * y_b
    x1 = x_ref[...] + jnp.dot(merged.astype(BF16), wout_ref[...], preferred_element_type=F32)
    x1_ref[...] = x1
    h2_ref[...] = _rms(x1, g_ref[...]).astype(h2_ref.dtype)


def _merge(attn, zg, gaya, x, w_proj_b, w_out, gain, tq=512):
    s, d = x.shape
    bw = attn.shape[1]
    return pl.pallas_call(
        _merge_kernel,
        grid=(s // tq,),
        in_specs=[pl.BlockSpec((tq, bw), lambda i: (i, 0)),
                  pl.BlockSpec((tq, d), lambda i: (i, 1)),
                  pl.BlockSpec((tq, d), lambda i: (i, 0)),
                  pl.BlockSpec((tq, d), lambda i: (i, 0)),
                  _resident((bw, d), lambda i: (0, 0)),
                  _resident((d, d), lambda i: (0, 0)),
                  _resident((1, d), lambda i: (0, 0))],
        out_specs=[pl.BlockSpec((tq, d), lambda i: (i, 0)),
                   pl.BlockSpec((tq, d), lambda i: (i, 0))],
        out_shape=[jax.ShapeDtypeStruct((s, d), F32),
                   jax.ShapeDtypeStruct((s, d), BF16)],
        compiler_params=_params(("arbitrary",), 56),
        name="merge_out_proj",
    )(attn, zg, gaya, x, w_proj_b, w_out, gain.reshape(1, d))


def _ffn_kernel(h_ref, wu_ref, wd_ref, x1_ref, g_ref, o_ref, *, final_norm):
    j = pl.program_id(1)

    @pl.when(j == 0)
    def _():
        o_ref[...] = x1_ref[...]

    a = jnp.dot(h_ref[...], wu_ref[...].astype(BF16), preferred_element_type=F32)
    a = jnp.square(jnp.maximum(a, 0.0)).astype(BF16)
    o_ref[...] += jnp.dot(a, wd_ref[...].astype(BF16), preferred_element_type=F32)

    if final_norm:
        @pl.when(j == pl.num_programs(1) - 1)
        def _():
            o_ref[...] = _rms(o_ref[...], g_ref[...])


def _ffn(h2, w_up, w_down, x1, gain, final_norm, tm=1024, tf=512):
    s, d = x1.shape
    dff = w_up.shape[1]
    return pl.pallas_call(
        functools.partial(_ffn_kernel, final_norm=final_norm),
        grid=(s // tm, dff // tf),
        in_specs=[_resident((tm, d), lambda i, j: (i, 0)),
                  pl.BlockSpec((d, tf), lambda i, j: (0, j)),
                  pl.BlockSpec((tf, d), lambda i, j: (j, 0)),
                  _resident((tm, d), lambda i, j: (i, 0)),
                  _resident((1, d), lambda i, j: (0, 0))],
        out_specs=pl.BlockSpec((tm, d), lambda i, j: (i, 0)),
        out_shape=jax.ShapeDtypeStruct((s, d), F32),
        compiler_params=_params(("arbitrary", "arbitrary"), 56),
        name="ffn_relu2",
    )(h2, w_up, w_down, x1, gain.reshape(1, d))


def kernel(x, ln_mix, w_in, a_v_gain, a_spatial, a_spatial_bias, w_proj_a, w_proj_b, w_out,
           rel_bias, ln_mlp, w_up, w_down, ln_final):
    bsz, s, d = x.shape
    depth = ln_mix.shape[0]
    aw = A_GROUPS * A_GROUP_DIM
    bw = B_HEADS * B_HEAD_DIM
    assert w_in.shape[2] == 2 * aw + 3 * bw + 2 * d
    outs = []
    for b in range(bsz):
        xb = x[b]
        for l in range(depth):
            last = l == depth - 1
            w_in_l = w_in[l]
            h = _norm(xb, ln_mix[l])
            zuv = _in_proj(h, w_in_l, 0, 2 * aw, "gelu")
            qkv_scale = jnp.concatenate([jnp.ones((bw,), F32), jnp.full((bw,), K_SCALE, F32), jnp.ones((bw,), F32)])
            zqkv = _in_proj(h, w_in_l, 2 * aw, 3 * bw, col_scale=qkv_scale)
            zg = _in_proj(h, w_in_l, 2 * aw + 3 * bw, 2 * d, "sigmoid")
            gaya = _gmlp(zuv, zg, a_v_gain[l], a_spatial[l], a_spatial_bias[l], w_proj_a[l].astype(BF16))
            attn = _moba(zqkv, rel_bias)
            x1, h2 = _merge(attn, zg, gaya, xb, w_proj_b[l].astype(BF16), w_out[l].astype(BF16), ln_mlp[l])
            gain = ln_final if last else jnp.ones((d,), F32)
            xb = _ffn(h2, w_up[l], w_down[l], x1, gain, final_norm=last)
        outs.append(xb)
    return jnp.stack(outs)
```

```python
import functools
import math

import jax
import jax.numpy as jnp
from jax import lax
from jax.experimental import pallas as pl
from jax.experimental.pallas import tpu as pltpu

A_GROUPS = 16
A_GROUP_DIM = 128
A_CHUNK = 128
B_HEADS = 16
B_HEAD_DIM = 128
MOBA_BLOCK = 256
MOBA_TOPK = 3
FAR_GROUP = 2
Q_STEP_BLOCKS = 2
PAIRS_PER_ITER = 2
SCORE_SLOTS = 2
GATE_STEPS = 4
PREP_UNROLL = 4
REL_BUCKETS = 32
REL_MAX_DIST = 128
EPS = 1e-6
NEG = -1e30
LOG2E = math.log2(math.e)
NEG_LOG2 = NEG * LOG2E
FAR_MASKED = 2 * NEG_LOG2
REMOVED = -3e38
K_SCALE = (B_HEAD_DIM ** -0.5) * LOG2E
BF16_SUBLANES = 16

F32 = jnp.float32
BF16 = jnp.bfloat16

MIB = 1024 * 1024
NT_DIMS = (((1,), (1,)), ((), ()))


def _params(semantics, vmem_mib):
    return pltpu.CompilerParams(dimension_semantics=semantics, vmem_limit_bytes=vmem_mib * MIB)


def _resident(shape, index_map):
    return pl.BlockSpec(shape, index_map, pipeline_mode=pl.Buffered(1))


def _rms(xf, gain):
    return xf * lax.rsqrt(jnp.mean(xf * xf, axis=-1, keepdims=True) + EPS) * gain


def _norm_kernel(x_ref, g_ref, h_ref):
    h_ref[...] = _rms(x_ref[...], g_ref[...]).astype(h_ref.dtype)


def _norm(x, gain, tm=512):
    s, d = x.shape
    return pl.pallas_call(
        _norm_kernel,
        grid=(s // tm,),
        in_specs=[pl.BlockSpec((tm, d), lambda i: (i, 0)),
                  pl.BlockSpec((1, d), lambda i: (0, 0))],
        out_specs=pl.BlockSpec((tm, d), lambda i: (i, 0)),
        out_shape=jax.ShapeDtypeStruct((s, d), BF16),
        compiler_params=_params(("arbitrary",), 32),
        name="rms_norm",
    )(x, gain.reshape(1, d))


def _gelu(a):
    return 0.5 * a * (1.0 + lax.erf(a * math.sqrt(0.5)))


_EPILOGUES = {"gelu": _gelu, "sigmoid": jax.nn.sigmoid}


def _weight_tile(w_ref, wbf_ref):
    @pl.when(pl.program_id(1) == 0)
    def _():
        wbf_ref[...] = w_ref[...].astype(BF16)

    return wbf_ref[...]


def _proj_kernel(h_ref, w_ref, o_ref, wbf_ref, *, epilogue):
    acc = jnp.dot(h_ref[...], _weight_tile(w_ref, wbf_ref), preferred_element_type=F32)
    o_ref[...] = _EPILOGUES[epilogue](acc).astype(o_ref.dtype)


def _proj_scale_kernel(h_ref, w_ref, scale_ref, o_ref, wbf_ref):
    acc = jnp.dot(h_ref[...], _weight_tile(w_ref, wbf_ref), preferred_element_type=F32)
    o_ref[...] = (acc * scale_ref[...]).astype(o_ref.dtype)


def _in_proj(h, w, col0, ncols, epilogue=None, col_scale=None, tm=1024, tn=1024):
    s, d = h.shape
    jb = col0 // tn
    in_specs = [pl.BlockSpec((tm, d), lambda j, i: (i, 0)),
                pl.BlockSpec((d, tn), lambda j, i: (0, j + jb))]
    if col_scale is None:
        body, args, name = functools.partial(_proj_kernel, epilogue=epilogue), (h, w), epilogue
    else:
        in_specs.append(pl.BlockSpec((1, tn), lambda j, i: (0, j)))
        body, args, name = _proj_scale_kernel, (h, w, col_scale.reshape(1, ncols)), "scale"
    return pl.pallas_call(
        body,
        grid=(ncols // tn, s // tm),
        in_specs=in_specs,
        out_specs=pl.BlockSpec((tm, tn), lambda j, i: (i, j)),
        out_shape=jax.ShapeDtypeStruct((s, ncols), BF16),
        scratch_shapes=[pltpu.VMEM((d, tn), BF16)],
        compiler_params=_params(("arbitrary", "arbitrary"), 48),
        name="in_proj_" + name,
    )(*args)


def _gmlp_kernel(u_ref, v_ref, ga_ref, gain_ref, ws_ref, bt_ref, wpa_ref, o_ref, wm_ref, a_ref):
    tq = u_ref.shape[0]

    @pl.when(pl.program_id(0) == 0)
    def _():
        t = lax.broadcasted_iota(jnp.int32, (A_CHUNK, A_CHUNK), 0)
        s = lax.broadcasted_iota(jnp.int32, (A_CHUNK, A_CHUNK), 1)
        for g in range(A_GROUPS):
            wm_ref[g] = jnp.where(t >= s, ws_ref[g], 0.0).astype(BF16)

    def chunk(c, carry):
        r0 = pl.multiple_of(c * A_CHUNK, A_CHUNK)
        v = v_ref[pl.ds(r0, A_CHUNK), :].astype(F32)
        mu = jnp.mean(v, axis=-1, keepdims=True)
        d = v - mu
        var = jnp.mean(d * d, axis=-1, keepdims=True)
        vn = (d * lax.rsqrt(var + EPS) * gain_ref[...]).astype(BF16)
        for g in range(A_GROUPS):
            cols = slice(g * A_GROUP_DIM, (g + 1) * A_GROUP_DIM)
            sv = jnp.dot(wm_ref[g], vn[:, cols], preferred_element_type=F32) + bt_ref[:, g:g + 1]
            u = u_ref[pl.ds(r0, A_CHUNK), cols].astype(F32)
            a_ref[pl.ds(r0, A_CHUNK), cols] = (u * sv).astype(BF16)
        return carry

    lax.fori_loop(0, tq // A_CHUNK, chunk, 0)
    y = jnp.dot(a_ref[...], wpa_ref[...], preferred_element_type=F32)
    o_ref[...] = ga_ref[...].astype(F32) * y


def _gmlp(zuv, zg, gain, w_s, b_s, w_proj_a, tq=512):
    s = zuv.shape[0]
    aw = A_GROUPS * A_GROUP_DIM
    d = w_proj_a.shape[1]
    return pl.pallas_call(
        _gmlp_kernel,
        grid=(s // tq,),
        in_specs=[pl.BlockSpec((tq, aw), lambda i: (i, 0)),
                  pl.BlockSpec((tq, aw), lambda i: (i, 1)),
                  pl.BlockSpec((tq, d), lambda i: (i, 0)),
                  _resident((1, aw), lambda i: (0, 0)),
                  _resident((A_GROUPS, A_CHUNK, A_CHUNK), lambda i: (0, 0, 0)),
                  _resident((A_CHUNK, A_GROUPS), lambda i: (0, 0)),
                  _resident((aw, d), lambda i: (0, 0))],
        out_specs=pl.BlockSpec((tq, d), lambda i: (i, 0)),
        out_shape=jax.ShapeDtypeStruct((s, d), F32),
        scratch_shapes=[pltpu.VMEM((A_GROUPS, A_CHUNK, A_CHUNK), BF16),
                        pltpu.VMEM((tq, aw), BF16)],
        compiler_params=_params(("arbitrary",), 48),
        name="gmlp_mixer",
    )(zuv, zuv, zg, gain.reshape(1, aw), w_s, b_s.T, w_proj_a)


def _t5_bias_log2(dist, rb_ref, base):
    max_exact = REL_BUCKETS // 2
    n = jnp.maximum(dist, 0)
    nf = jnp.maximum(n, 1).astype(F32)
    large = max_exact + (jnp.log(nf / max_exact) / math.log(REL_MAX_DIST / max_exact)
                         * (REL_BUCKETS - max_exact)).astype(jnp.int32)
    large = jnp.minimum(large, REL_BUCKETS - 1)
    bucket = jnp.where(n < max_exact, n, large)
    out = jnp.zeros(dist.shape, F32)
    for b in range(REL_BUCKETS):
        out = jnp.where(bucket == b, rb_ref[base + b] * LOG2E, out)
    return out


def _moba_kernel(rb_ref, q_ref, k_ref, v_ref, o_ref,
                 kmean_ref, vt_ref, tile_ref, rows_ref, m_ref, acc_ref, *slots):
    L = MOBA_BLOCK
    G = FAR_GROUP
    QB = Q_STEP_BLOCKS
    W = QB * L
    dh = B_HEAD_DIM
    nb = q_ref.shape[0] // L
    nq = nb // QB
    log2_l = L.bit_length() - 1
    base = pl.program_id(0) * REL_BUCKETS
    far_bias = rb_ref[base + REL_BUCKETS - 1] * LOG2E

    def prep(it, carry):
        for i in range(PREP_UNROLL):
            n = it * PREP_UNROLL + i
            r0 = pl.multiple_of(n * L, L)
            kmean_ref[pl.ds(n, 1), :] = jnp.mean(k_ref[pl.ds(r0, L), :].astype(F32), axis=0, keepdims=True)
            vt_ref[n, 0:dh, :] = v_ref[pl.ds(r0, L), :].astype(F32).T.astype(BF16)
            vt_ref[n, dh:, :] = jnp.ones((vt_ref.shape[1] - dh, L), BF16)
        return carry

    lax.fori_loop(0, nb // PREP_UNROLL, prep, 0)

    kk = lax.broadcasted_iota(jnp.int32, (L, L), 0)
    qq = lax.broadcasted_iota(jnp.int32, (L, L), 1)
    d_own = qq - kk
    b_prev = _t5_bias_log2(d_own + L, rb_ref, base)
    b_own = jnp.where(d_own >= 0, _t5_bias_log2(d_own, rb_ref, base), NEG_LOG2)
    masked = jnp.full((L, L), NEG_LOG2, F32)
    zero = jnp.zeros((L, L), F32)
    kinds = [[[zero, zero], [zero, zero]], [[zero, zero], [b_prev, zero]], [[b_own, b_prev], [masked, b_own]]]
    for kind, blocks in enumerate(kinds):
        for r, tiles in enumerate(blocks):
            for c, tile in enumerate(tiles):
                tile_ref[kind, r * L:(r + 1) * L, c * L:(c + 1) * L] = tile

    kmean = kmean_ref[...].astype(BF16)

    def gate(it, carry):
        u0 = it * GATE_STEPS
        q_c = q_ref[pl.ds(pl.multiple_of(u0 * W, GATE_STEPS * W), GATE_STEPS * W), :]
        gs = lax.dot_general(kmean, q_c, NT_DIMS, preferred_element_type=F32) * (1.0 / K_SCALE)
        blk = lax.broadcasted_iota(jnp.int32, gs.shape, 0)
        own = QB * u0 + lax.shift_right_logical(lax.broadcasted_iota(jnp.int32, gs.shape, 1), log2_l)
        gs = jnp.where(blk < own, gs, NEG)
        sel = jnp.zeros(gs.shape, F32)
        for _ in range(MOBA_TOPK):
            top = jnp.max(gs, axis=0, keepdims=True)
            idx = jnp.min(jnp.where(gs == top, blk, nb), axis=0, keepdims=True)
            hit = blk == idx
            sel = jnp.where(hit, jnp.where(top > NEG / 2, 1.0, 0.0), sel)
            gs = jnp.where(hit, REMOVED, gs)
        picked = sel > 0.5
        rows = jnp.where(blk < own - 1, jnp.where(picked, far_bias, FAR_MASKED),
                         jnp.where(blk == own - 1, jnp.where(picked, 0.0, FAR_MASKED), 0.0))
        for i in range(GATE_STEPS):
            rows_ref[u0 + i] = rows[:, i * W:(i + 1) * W]
        return carry

    lax.fori_loop(0, nq // GATE_STEPS, gate, 0)

    def item(t, ua):
        first = t <= ua
        u = jnp.where(first, ua, nq - 1 - ua)
        g = jnp.where(first, t, t - ua - 1)
        return u, g, jnp.where(first, 0, 1), jnp.clip(g - (u - 2), 0, 2)

    def rows_of(u, g):
        return [rows_ref[u, pl.ds(g * G + b, 1), :] for b in range(G)]

    def produce(n, t, ua):
        u, g, _, kind = item(t, ua)
        q_u = q_ref[pl.ds(pl.multiple_of(u * W, W), W), :]
        k_g = k_ref[pl.ds(pl.multiple_of(g * (G * L), G * L), G * L), :]
        s_t = lax.dot_general(k_g, q_u, NT_DIMS, preferred_element_type=F32) + tile_ref[kind]
        slots[n % len(slots)][...] = s_t
        rows = rows_of(u, g)
        return functools.reduce(jnp.maximum, [
            jnp.max(s_t[b * L:(b + 1) * L], axis=0, keepdims=True) + rows[b] for b in range(G)])

    def consume(n, t, ua, pi, cand):
        u, g, w, _ = item(t, ua)
        rows = rows_of(u, g)
        s_ref = slots[n % len(slots)]
        st = pi * 2 + w
        m_old = m_ref[st]
        m_new = jnp.maximum(m_old, cand)
        pv = None
        for b in range(G):
            p = jnp.exp2(s_ref[b * L:(b + 1) * L, :] + (rows[b] - m_new)).astype(BF16)
            d = jnp.dot(vt_ref[g * G + b], p, preferred_element_type=F32)
            pv = d if pv is None else pv + d
        acc_ref[st] = jnp.exp2(m_old - m_new) * acc_ref[st] + pv
        m_ref[st] = m_new

    def finish(ua, pi):
        for w, u in ((0, ua), (1, nq - 1 - ua)):
            acc = acc_ref[pi * 2 + w]
            o_ref[pl.ds(pl.multiple_of(u * W, W), W), :] = (acc[:dh] / acc[dh:dh + 1]).T.astype(o_ref.dtype)

    def stream(it, carry):
        m_ref[...] = jnp.full(m_ref.shape, NEG_LOG2, F32)
        acc_ref[...] = jnp.zeros(acc_ref.shape, F32)
        seq = [(pi, t) for pi in range(PAIRS_PER_ITER) for t in range(nq + 1)]
        first_step = lambda pi: it * PAIRS_PER_ITER + pi
        cand = produce(0, seq[0][1], first_step(seq[0][0]))
        for n, (pi, t) in enumerate(seq):
            if n + 1 < len(seq):
                cand_next = produce(n + 1, seq[n + 1][1], first_step(seq[n + 1][0]))
            consume(n, t, first_step(pi), pi, cand)
            cand = cand_next
            if t == nq:
                finish(first_step(pi), pi)
        return carry

    lax.fori_loop(0, nq // 2 // PAIRS_PER_ITER, stream, 0)


def _moba(zqkv, rel_bias):
    s = zqkv.shape[0]
    dh = B_HEAD_DIM
    L = MOBA_BLOCK
    nb = s // L
    assert Q_STEP_BLOCKS == 2 and FAR_GROUP == 2 and nb % (2 * Q_STEP_BLOCKS) == 0
    assert nb % PREP_UNROLL == 0 and (nb // Q_STEP_BLOCKS) % GATE_STEPS == 0
    assert (nb // Q_STEP_BLOCKS // 2) % PAIRS_PER_ITER == 0
    w = Q_STEP_BLOCKS * L
    rb = rel_bias.T.reshape(-1).astype(F32)
    return pl.pallas_call(
        _moba_kernel,
        grid=(B_HEADS,),
        in_specs=[pl.BlockSpec(memory_space=pltpu.SMEM),
                  pl.BlockSpec((s, dh), lambda h: (0, h)),
                  pl.BlockSpec((s, dh), lambda h: (0, B_HEADS + h)),
                  pl.BlockSpec((s, dh), lambda h: (0, 2 * B_HEADS + h))],
        out_specs=pl.BlockSpec((s, dh), lambda h: (0, h)),
        out_shape=jax.ShapeDtypeStruct((s, B_HEADS * dh), BF16),
        scratch_shapes=[pltpu.VMEM((nb, dh), F32),
                        pltpu.VMEM((nb, dh + BF16_SUBLANES, L), BF16),
                        pltpu.VMEM((3, FAR_GROUP * L, w), F32),
                        pltpu.VMEM((nb // Q_STEP_BLOCKS, nb, w), F32),
                        pltpu.VMEM((2 * PAIRS_PER_ITER, 1, w), F32),
                        pltpu.VMEM((2 * PAIRS_PER_ITER, dh + BF16_SUBLANES, w), F32)]
        + [pltpu.VMEM((FAR_GROUP * L, w), F32)] * SCORE_SLOTS,

        compiler_params=_params(("arbitrary",), 48),
        name="moba_attention",
    )(rb, zqkv, zqkv, zqkv)


def _merge_kernel(attn_ref, gb_ref, gaya_ref, x_ref, wpb_ref, wout_ref, g_ref, x1_ref, h2_ref):
    y_b = jnp.dot(attn_ref[...], wpb_ref[...], preferred_element_type=F32)
    merged = gaya_ref[...] + gb_ref[...].astype(F32) * y_b
    x1 = x_ref[...] + jnp.dot(merged.astype(BF16), wout_ref[...], preferred_element_type=F32)
    x1_ref[...] = x1
    h2_ref[...] = _rms(x1, g_ref[...]).astype(h2_ref.dtype)


def _merge(attn, zg, gaya, x, w_proj_b, w_out, gain, tq=512):
    s, d = x.shape
    bw = attn.shape[1]
    return pl.pallas_call(
        _merge_kernel,
        grid=(s // tq,),
        in_specs=[pl.BlockSpec((tq, bw), lambda i: (i, 0)),
                  pl.BlockSpec((tq, d), lambda i: (i, 1)),
                  pl.BlockSpec((tq, d), lambda i: (i, 0)),
                  pl.BlockSpec((tq, d), lambda i: (i, 0)),
                  _resident((bw, d), lambda i: (0, 0)),
                  _resident((d, d), lambda i: (0, 0)),
                  _resident((1, d), lambda i: (0, 0))],
        out_specs=[pl.BlockSpec((tq, d), lambda i: (i, 0)),
                   pl.BlockSpec((tq, d), lambda i: (i, 0))],
        out_shape=[jax.ShapeDtypeStruct((s, d), F32),
                   jax.ShapeDtypeStruct((s, d), BF16)],
        compiler_params=_params(("arbitrary",), 56),
        name="merge_out_proj",
    )(attn, zg, gaya, x, w_proj_b, w_out, gain.reshape(1, d))


def _ffn_kernel(h_ref, wu_ref, wd_ref, x1_ref, g_ref, o_ref, *, final_norm):
    j = pl.program_id(1)

    @pl.when(j == 0)
    def _():
        o_ref[...] = x1_ref[...]

    a = jnp.dot(h_ref[...], wu_ref[...].astype(BF16), preferred_element_type=F32)
    a = jnp.square(jnp.maximum(a, 0.0)).astype(BF16)
    o_ref[...] += jnp.dot(a, wd_ref[...].astype(BF16), preferred_element_type=F32)

    if final_norm:
        @pl.when(j == pl.num_programs(1) - 1)
        def _():
            o_ref[...] = _rms(o_ref[...], g_ref[...])


def _ffn(h2, w_up, w_down, x1, gain, final_norm, tm=1024, tf=512):
    s, d = x1.shape
    dff = w_up.shape[1]
    return pl.pallas_call(
        functools.partial(_ffn_kernel, final_norm=final_norm),
        grid=(s // tm, dff // tf),
        in_specs=[_resident((tm, d), lambda i, j: (i, 0)),
                  pl.BlockSpec((d, tf), lambda i, j: (0, j)),
                  pl.BlockSpec((tf, d), lambda i, j: (j, 0)),
                  _resident((tm, d), lambda i, j: (i, 0)),
                  _resident((1, d), lambda i, j: (0, 0))],
        out_specs=pl.BlockSpec((tm, d), lambda i, j: (i, 0)),
        out_shape=jax.ShapeDtypeStruct((s, d), F32),
        compiler_params=_params(("arbitrary", "arbitrary"), 56),
        name="ffn_relu2",
    )(h2, w_up, w_down, x1, gain.reshape(1, d))


def kernel(x, ln_mix, w_in, a_v_gain, a_spatial, a_spatial_bias, w_proj_a, w_proj_b, w_out,
           rel_bias, ln_mlp, w_up, w_down, ln_final):
    bsz, s, d = x.shape
    depth = ln_mix.shape[0]
    aw = A_GROUPS * A_GROUP_DIM
    bw = B_HEADS * B_HEAD_DIM
    assert w_in.shape[2] == 2 * aw + 3 * bw + 2 * d
    outs = []
    for b in range(bsz):
        xb = x[b]
        for l in range(depth):
            last = l == depth - 1
            w_in_l = w_in[l]
            h = _norm(xb, ln_mix[l])
            zuv = _in_proj(h, w_in_l, 0, 2 * aw, "gelu")
            qkv_scale = jnp.concatenate([jnp.ones((bw,), F32), jnp.full((bw,), K_SCALE, F32), jnp.ones((bw,), F32)])
            zqkv = _in_proj(h, w_in_l, 2 * aw, 3 * bw, col_scale=qkv_scale)
            zg = _in_proj(h, w_in_l, 2 * aw + 3 * bw, 2 * d, "sigmoid")
            gaya = _gmlp(zuv, zg, a_v_gain[l], a_spatial[l], a_spatial_bias[l], w_proj_a[l].astype(BF16))
            attn = _moba(zqkv, rel_bias)
            x1, h2 = _merge(attn, zg, gaya, xb, w_proj_b[l].astype(BF16), w_out[l].astype(BF16), ln_mlp[l])
            gain = ln_final if last else jnp.ones((d,), F32)
            xb = _ffn(h2, w_up[l], w_down[l], x1, gain, final_norm=last)
        outs.append(xb)
    return jnp.stack(outs)
```

```python
import functools
import math

import jax
import jax.numpy as jnp
from jax import lax
from jax.experimental import pallas as pl
from jax.experimental.pallas import tpu as pltpu

A_GROUPS = 16
A_GROUP_DIM = 128
A_CHUNK = 128
B_HEADS = 16
B_HEAD_DIM = 128
MOBA_BLOCK = 256
MOBA_TOPK = 3
FAR_GROUP = 2
Q_STEP_BLOCKS = 2
PAIRS_PER_ITER = 2
SCORE_SLOTS = 2
GATE_STEPS = 4
PREP_UNROLL = 4
REL_BUCKETS = 32
REL_MAX_DIST = 128
EPS = 1e-6
NEG = -1e30
LOG2E = math.log2(math.e)
NEG_LOG2 = NEG * LOG2E
FAR_MASKED = 2 * NEG_LOG2
REMOVED = -3e38
K_SCALE = (B_HEAD_DIM ** -0.5) * LOG2E
BF16_SUBLANES = 16

F32 = jnp.float32
BF16 = jnp.bfloat16

MIB = 1024 * 1024
NT_DIMS = (((1,), (1,)), ((), ()))

V7X_VMEM_MIB = 64
NORM_ROWS = 512
PROJ_ROWS = 1024
PROJ_COLS = 1024
MIX_ROWS = 512
FFN_ROWS = 1024
FFN_COLS = 512


def _params(semantics, window_bytes):
    limit = min(window_bytes + V7X_VMEM_MIB * MIB // 4, V7X_VMEM_MIB * MIB * 7 // 8)
    return pltpu.CompilerParams(dimension_semantics=semantics, vmem_limit_bytes=limit)


def _nbytes(shape, dtype, buffers=1):
    return math.prod(shape) * jnp.dtype(dtype).itemsize * buffers


def _resident(shape, index_map):
    return pl.BlockSpec(shape, index_map, pipeline_mode=pl.Buffered(1))


def _rms(xf, gain):
    return xf * lax.rsqrt(jnp.mean(xf * xf, axis=-1, keepdims=True) + EPS) * gain


def _norm_kernel(x_ref, g_ref, h_ref):
    h_ref[...] = _rms(x_ref[...], g_ref[...]).astype(h_ref.dtype)


def _norm(x, gain, tm=NORM_ROWS):
    s, d = x.shape
    windows = _nbytes((tm, d), F32, 2) + _nbytes((tm, d), BF16, 2)
    return pl.pallas_call(
        _norm_kernel,
        grid=(s // tm,),
        in_specs=[pl.BlockSpec((tm, d), lambda i: (i, 0)),
                  pl.BlockSpec((1, d), lambda i: (0, 0))],
        out_specs=pl.BlockSpec((tm, d), lambda i: (i, 0)),
        out_shape=jax.ShapeDtypeStruct((s, d), BF16),
        compiler_params=_params(("arbitrary",), windows),
        name="rms_norm",
    )(x, gain.reshape(1, d))


def _gelu(a):
    return 0.5 * a * (1.0 + lax.erf(a * math.sqrt(0.5)))


_EPILOGUES = {"gelu": _gelu, "sigmoid": jax.nn.sigmoid}


def _weight_tile(w_ref, wbf_ref):
    @pl.when(pl.program_id(1) == 0)
    def _():
        wbf_ref[...] = w_ref[...].astype(BF16)

    return wbf_ref[...]


def _proj_kernel(h_ref, w_ref, o_ref, wbf_ref, *, epilogue):
    acc = jnp.dot(h_ref[...], _weight_tile(w_ref, wbf_ref), preferred_element_type=F32)
    o_ref[...] = _EPILOGUES[epilogue](acc).astype(o_ref.dtype)


def _proj_scale_kernel(h_ref, w_ref, scale_ref, o_ref, wbf_ref):
    acc = jnp.dot(h_ref[...], _weight_tile(w_ref, wbf_ref), preferred_element_type=F32)
    o_ref[...] = (acc * scale_ref[...]).astype(o_ref.dtype)


def _in_proj(h, w, col0, ncols, epilogue=None, col_scale=None, tm=PROJ_ROWS, tn=PROJ_COLS):
    s, d = h.shape
    jb = col0 // tn
    in_specs = [pl.BlockSpec((tm, d), lambda j, i: (i, 0)),
                pl.BlockSpec((d, tn), lambda j, i: (0, j + jb))]
    if col_scale is None:
        body, args, name = functools.partial(_proj_kernel, epilogue=epilogue), (h, w), epilogue
    else:
        in_specs.append(pl.BlockSpec((1, tn), lambda j, i: (0, j)))
        body, args, name = _proj_scale_kernel, (h, w, col_scale.reshape(1, ncols)), "scale"
    windows = (_nbytes((tm, d), BF16, 2) + _nbytes((d, tn), F32, 2) + _nbytes((d, tn), BF16)
               + _nbytes((tm, tn), BF16, 2))
    return pl.pallas_call(
        body,
        grid=(ncols // tn, s // tm),
        in_specs=in_specs,
        out_specs=pl.BlockSpec((tm, tn), lambda j, i: (i, j)),
        out_shape=jax.ShapeDtypeStruct((s, ncols), BF16),
        scratch_shapes=[pltpu.VMEM((d, tn), BF16)],
        compiler_params=_params(("arbitrary", "arbitrary"), windows),
        name="in_proj_" + name,
    )(*args)


def _gmlp_kernel(u_ref, v_ref, ga_ref, gain_ref, ws_ref, bt_ref, wpa_ref, o_ref, wm_ref, a_ref):
    tq = u_ref.shape[0]

    @pl.when(pl.program_id(0) == 0)
    def _():
        t = lax.broadcasted_iota(jnp.int32, (A_CHUNK, A_CHUNK), 0)
        s = lax.broadcasted_iota(jnp.int32, (A_CHUNK, A_CHUNK), 1)
        for g in range(A_GROUPS):
            wm_ref[g] = jnp.where(t >= s, ws_ref[g], 0.0).astype(BF16)

    def chunk(c, carry):
        r0 = pl.multiple_of(c * A_CHUNK, A_CHUNK)
        v = v_ref[pl.ds(r0, A_CHUNK), :].astype(F32)
        mu = jnp.mean(v, axis=-1, keepdims=True)
        d = v - mu
        var = jnp.mean(d * d, axis=-1, keepdims=True)
        vn = (d * lax.rsqrt(var + EPS) * gain_ref[...]).astype(BF16)
        for g in range(A_GROUPS):
            cols = slice(g * A_GROUP_DIM, (g + 1) * A_GROUP_DIM)
            sv = jnp.dot(wm_ref[g], vn[:, cols], preferred_element_type=F32) + bt_ref[:, g:g + 1]
            u = u_ref[pl.ds(r0, A_CHUNK), cols].astype(F32)
            a_ref[pl.ds(r0, A_CHUNK), cols] = (u * sv).astype(BF16)
        return carry

    lax.fori_loop(0, tq // A_CHUNK, chunk, 0)
    y = jnp.dot(a_ref[...], wpa_ref[...], preferred_element_type=F32)
    o_ref[...] = ga_ref[...].astype(F32) * y


def _gmlp(zuv, zg, gain, w_s, b_s, w_proj_a, tq=MIX_ROWS):
    s = zuv.shape[0]
    aw = A_GROUPS * A_GROUP_DIM
    d = w_proj_a.shape[1]
    windows = (_nbytes((tq, aw), BF16, 4) + _nbytes((tq, d), BF16, 2) + _nbytes((tq, d), F32, 2)
               + _nbytes((aw, d), BF16) + _nbytes((A_GROUPS, A_CHUNK, A_CHUNK), F32)
               + _nbytes((A_GROUPS, A_CHUNK, A_CHUNK), BF16) + _nbytes((tq, aw), BF16))
    return pl.pallas_call(
        _gmlp_kernel,
        grid=(s // tq,),
        in_specs=[pl.BlockSpec((tq, aw), lambda i: (i, 0)),
                  pl.BlockSpec((tq, aw), lambda i: (i, 1)),
                  pl.BlockSpec((tq, d), lambda i: (i, 0)),
                  _resident((1, aw), lambda i: (0, 0)),
                  _resident((A_GROUPS, A_CHUNK, A_CHUNK), lambda i: (0, 0, 0)),
                  _resident((A_CHUNK, A_GROUPS), lambda i: (0, 0)),
                  _resident((aw, d), lambda i: (0, 0))],
        out_specs=pl.BlockSpec((tq, d), lambda i: (i, 0)),
        out_shape=jax.ShapeDtypeStruct((s, d), F32),
        scratch_shapes=[pltpu.VMEM((A_GROUPS, A_CHUNK, A_CHUNK), BF16),
                        pltpu.VMEM((tq, aw), BF16)],
        compiler_params=_params(("arbitrary",), windows),
        name="gmlp_mixer",
    )(zuv, zuv, zg, gain.reshape(1, aw), w_s, b_s.T, w_proj_a)


def _t5_bias_log2(dist, rb_ref, base):
    max_exact = REL_BUCKETS // 2
    n = jnp.maximum(dist, 0)
    nf = jnp.maximum(n, 1).astype(F32)
    large = max_exact + (jnp.log(nf / max_exact) / math.log(REL_MAX_DIST / max_exact)
                         * (REL_BUCKETS - max_exact)).astype(jnp.int32)
    large = jnp.minimum(large, REL_BUCKETS - 1)
    bucket = jnp.where(n < max_exact, n, large)
    out = jnp.zeros(dist.shape, F32)
    for b in range(REL_BUCKETS):
        out = jnp.where(bucket == b, rb_ref[base + b] * LOG2E, out)
    return out


def _moba_kernel(rb_ref, q_ref, k_ref, v_ref, o_ref,
                 kmean_ref, vt_ref, tile_ref, rows_ref, m_ref, acc_ref, *slots):
    L = MOBA_BLOCK
    G = FAR_GROUP
    QB = Q_STEP_BLOCKS
    W = QB * L
    dh = B_HEAD_DIM
    nb = q_ref.shape[0] // L
    nq = nb // QB
    log2_l = L.bit_length() - 1
    base = pl.program_id(0) * REL_BUCKETS
    far_bias = rb_ref[base + REL_BUCKETS - 1] * LOG2E

    def prep(it, carry):
        for i in range(PREP_UNROLL):
            n = it * PREP_UNROLL + i
            r0 = pl.multiple_of(n * L, L)
            kmean_ref[pl.ds(n, 1), :] = jnp.mean(k_ref[pl.ds(r0, L), :].astype(F32), axis=0, keepdims=True)
            vt_ref[n, 0:dh, :] = v_ref[pl.ds(r0, L), :].astype(F32).T.astype(BF16)
            vt_ref[n, dh:, :] = jnp.ones((vt_ref.shape[1] - dh, L), BF16)
        return carry

    lax.fori_loop(0, nb // PREP_UNROLL, prep, 0)

    dist = lax.broadcasted_iota(jnp.int32, (1, 2 * L), 1) - L

    def toeplitz(row):
        return pltpu.roll(jnp.broadcast_to(row, (L, 2 * L)), L, 1, stride=1, stride_axis=0)[:, :L]

    b_prev = toeplitz(_t5_bias_log2(dist + L, rb_ref, base))
    b_own = toeplitz(jnp.where(dist >= 0, _t5_bias_log2(dist, rb_ref, base), NEG_LOG2))
    masked = jnp.full((L, L), NEG_LOG2, F32)
    zero = jnp.zeros((L, L), F32)
    kinds = [[[zero, zero], [zero, zero]], [[zero, zero], [b_prev, zero]], [[b_own, b_prev], [masked, b_own]]]
    for kind, blocks in enumerate(kinds):
        for r, tiles in enumerate(blocks):
            for c, tile in enumerate(tiles):
                tile_ref[kind, r * L:(r + 1) * L, c * L:(c + 1) * L] = tile

    kmean = kmean_ref[...].astype(BF16)

    def gate(it, carry):
        u0 = it * GATE_STEPS
        q_c = q_ref[pl.ds(pl.multiple_of(u0 * W, GATE_STEPS * W), GATE_STEPS * W), :]
        gs = lax.dot_general(kmean, q_c, NT_DIMS, preferred_element_type=F32) * (1.0 / K_SCALE)
        blk = lax.broadcasted_iota(jnp.int32, gs.shape, 0)
        own = QB * u0 + lax.shift_right_logical(lax.broadcasted_iota(jnp.int32, gs.shape, 1), log2_l)
        gs = jnp.where(blk < own, gs, NEG)
        sel = jnp.zeros(gs.shape, F32)
        for _ in range(MOBA_TOPK):
            top = jnp.max(gs, axis=0, keepdims=True)
            idx = jnp.min(jnp.where(gs == top, blk, nb), axis=0, keepdims=True)
            hit = blk == idx
            sel = jnp.where(hit, jnp.where(top > NEG / 2, 1.0, 0.0), sel)
            gs = jnp.where(hit, REMOVED, gs)
        picked = sel > 0.5
        rows = jnp.where(blk < own - 1, jnp.where(picked, far_bias, FAR_MASKED),
                         jnp.where(blk == own - 1, jnp.where(picked, 0.0, FAR_MASKED), 0.0))
        for i in range(GATE_STEPS):
            rows_ref[u0 + i] = rows[:, i * W:(i + 1) * W]
        return carry

    lax.fori_loop(0, nq // GATE_STEPS, gate, 0)

    def item(t, ua):
        first = t <= ua
        u = jnp.where(first, ua, nq - 1 - ua)
        g = jnp.where(first, t, t - ua - 1)
        return u, g, jnp.where(first, 0, 1), jnp.clip(g - (u - 2), 0, 2)

    def rows_of(u, g):
        return [rows_ref[u, pl.ds(g * G + b, 1), :] for b in range(G)]

    def produce(n, t, ua):
        u, g, _, kind = item(t, ua)
        q_u = q_ref[pl.ds(pl.multiple_of(u * W, W), W), :]
        k_g = k_ref[pl.ds(pl.multiple_of(g * (G * L), G * L), G * L), :]
        s_t = lax.dot_general(k_g, q_u, NT_DIMS, preferred_element_type=F32) + tile_ref[kind]
        slots[n % len(slots)][...] = s_t
        rows = rows_of(u, g)
        return functools.reduce(jnp.maximum, [
            jnp.max(s_t[b * L:(b + 1) * L], axis=0, keepdims=True) + rows[b] for b in range(G)])

    def consume(n, t, ua, pi, cand):
        u, g, w, _ = item(t, ua)
        rows = rows_of(u, g)
        s_ref = slots[n % len(slots)]
        st = pi * 2 + w
        m_old = m_ref[st]
        m_new = jnp.maximum(m_old, cand)
        pv = None
        for b in range(G):
            p = jnp.exp2(s_ref[b * L:(b + 1) * L, :] + (rows[b] - m_new)).astype(BF16)
            d = jnp.dot(vt_ref[g * G + b], p, preferred_element_type=F32)
            pv = d if pv is None else pv + d
        acc_ref[st] = jnp.exp2(m_old - m_new) * acc_ref[st] + pv
        m_ref[st] = m_new

    def finish(ua, pi):
        for w, u in ((0, ua), (1, nq - 1 - ua)):
            acc = acc_ref[pi * 2 + w]
            o_ref[pl.ds(pl.multiple_of(u * W, W), W), :] = (acc[:dh] / acc[dh:dh + 1]).T.astype(o_ref.dtype)

    def stream(it, carry):
        m_ref[...] = jnp.full(m_ref.shape, NEG_LOG2, F32)
        acc_ref[...] = jnp.zeros(acc_ref.shape, F32)
        seq = [(pi, t) for pi in range(PAIRS_PER_ITER) for t in range(nq + 1)]
        first_step = lambda pi: it * PAIRS_PER_ITER + pi
        cand = produce(0, seq[0][1], first_step(seq[0][0]))
        for n, (pi, t) in enumerate(seq):
            if n + 1 < len(seq):
                cand_next = produce(n + 1, seq[n + 1][1], first_step(seq[n + 1][0]))
            consume(n, t, first_step(pi), pi, cand)
            cand = cand_next
            if t == nq:
                finish(first_step(pi), pi)
        return carry

    lax.fori_loop(0, nq // 2 // PAIRS_PER_ITER, stream, 0)


def _moba(zqkv, rel_bias):
    s = zqkv.shape[0]
    dh = B_HEAD_DIM
    L = MOBA_BLOCK
    nb = s // L
    assert Q_STEP_BLOCKS == 2 and FAR_GROUP == 2 and nb % (2 * Q_STEP_BLOCKS) == 0
    assert nb % PREP_UNROLL == 0 and (nb // Q_STEP_BLOCKS) % GATE_STEPS == 0
    assert (nb // Q_STEP_BLOCKS // 2) % PAIRS_PER_ITER == 0
    w = Q_STEP_BLOCKS * L
    rb = rel_bias.T.reshape(-1).astype(F32)
    scratch = [pltpu.VMEM((nb, dh), F32),
               pltpu.VMEM((nb, dh + BF16_SUBLANES, L), BF16),
               pltpu.VMEM((3, FAR_GROUP * L, w), F32),
               pltpu.VMEM((nb // Q_STEP_BLOCKS, nb, w), F32),
               pltpu.VMEM((2 * PAIRS_PER_ITER, 1, w), F32),
               pltpu.VMEM((2 * PAIRS_PER_ITER, dh + BF16_SUBLANES, w), F32)]
    scratch += [pltpu.VMEM((FAR_GROUP * L, w), F32)] * SCORE_SLOTS
    windows = _nbytes((s, dh), BF16, 8) + sum(_nbytes(b.shape, b.dtype) for b in scratch)
    return pl.pallas_call(
        _moba_kernel,
        grid=(B_HEADS,),
        in_specs=[pl.BlockSpec(memory_space=pltpu.SMEM),
                  pl.BlockSpec((s, dh), lambda h: (0, h)),
                  pl.BlockSpec((s, dh), lambda h: (0, B_HEADS + h)),
                  pl.BlockSpec((s, dh), lambda h: (0, 2 * B_HEADS + h))],
        out_specs=pl.BlockSpec((s, dh), lambda h: (0, h)),
        out_shape=jax.ShapeDtypeStruct((s, B_HEADS * dh), BF16),
        scratch_shapes=scratch,
        compiler_params=_params(("arbitrary",), windows),
        name="moba_attention",
    )(rb, zqkv, zqkv, zqkv)


def _merge_kernel(attn_ref, gb_ref, gaya_ref, x_ref, wpb_ref, wout_ref, g_ref, x1_ref, h2_ref):
    y_b = jnp.dot(attn_ref[...], wpb_ref[...], preferred_element_type=F32)
    merged = gaya_ref[...] + gb_ref[...].astype(F32) * y_b
    x1 = x_ref[...] + jnp.dot(merged.astype(BF16), wout_ref[...], preferred_element_type=F32)
    x1_ref[...] = x1
    h2_ref[...] = _rms(x1, g_ref[...]).astype(h2_ref.dtype)


def _merge(attn, zg, gaya, x, w_proj_b, w_out, gain, tq=MIX_ROWS):
    s, d = x.shape
    bw = attn.shape[1]
    windows = (_nbytes((tq, bw), BF16, 2) + _nbytes((tq, d), BF16, 2) + _nbytes((tq, d), F32, 4)
               + _nbytes((bw, d), BF16) + _nbytes((d, d), BF16)
               + _nbytes((tq, d), F32, 2) + _nbytes((tq, d), BF16, 2))
    return pl.pallas_call(
        _merge_kernel,
        grid=(s // tq,),
        in_specs=[pl.BlockSpec((tq, bw), lambda i: (i, 0)),
                  pl.BlockSpec((tq, d), lambda i: (i, 1)),
                  pl.BlockSpec((tq, d), lambda i: (i, 0)),
                  pl.BlockSpec((tq, d), lambda i: (i, 0)),
                  _resident((bw, d), lambda i: (0, 0)),
                  _resident((d, d), lambda i: (0, 0)),
                  _resident((1, d), lambda i: (0, 0))],
        out_specs=[pl.BlockSpec((tq, d), lambda i: (i, 0)),
                   pl.BlockSpec((tq, d), lambda i: (i, 0))],
        out_shape=[jax.ShapeDtypeStruct((s, d), F32),
                   jax.ShapeDtypeStruct((s, d), BF16)],
        compiler_params=_params(("arbitrary",), windows),
        name="merge_out_proj",
    )(attn, zg, gaya, x, w_proj_b, w_out, gain.reshape(1, d))


def _ffn_kernel(h_ref, wu_ref, wd_ref, x1_ref, g_ref, o_ref, *, final_norm):
    j = pl.program_id(1)

    @pl.when(j == 0)
    def _():
        o_ref[...] = x1_ref[...]

    a = jnp.dot(h_ref[...], wu_ref[...].astype(BF16), preferred_element_type=F32)
    a = jnp.square(jnp.maximum(a, 0.0)).astype(BF16)
    o_ref[...] += jnp.dot(a, wd_ref[...].astype(BF16), preferred_element_type=F32)

    if final_norm:
        @pl.when(j == pl.num_programs(1) - 1)
        def _():
            o_ref[...] = _rms(o_ref[...], g_ref[...])


def _ffn(h2, w_up, w_down, x1, gain, final_norm, tm=FFN_ROWS, tf=FFN_COLS):
    s, d = x1.shape
    dff = w_up.shape[1]
    windows = (_nbytes((tm, d), BF16) + _nbytes((d, tf), F32, 4) + _nbytes((tm, d), F32)
               + _nbytes((tm, d), F32, 2))
    return pl.pallas_call(
        functools.partial(_ffn_kernel, final_norm=final_norm),
        grid=(s // tm, dff // tf),
        in_specs=[_resident((tm, d), lambda i, j: (i, 0)),
                  pl.BlockSpec((d, tf), lambda i, j: (0, j)),
                  pl.BlockSpec((tf, d), lambda i, j: (j, 0)),
                  _resident((tm, d), lambda i, j: (i, 0)),
                  _resident((1, d), lambda i, j: (0, 0))],
        out_specs=pl.BlockSpec((tm, d), lambda i, j: (i, 0)),
        out_shape=jax.ShapeDtypeStruct((s, d), F32),
        compiler_params=_params(("arbitrary", "arbitrary"), windows),
        name="ffn_relu2",
    )(h2, w_up, w_down, x1, gain.reshape(1, d))


def kernel(x, ln_mix, w_in, a_v_gain, a_spatial, a_spatial_bias, w_proj_a, w_proj_b, w_out,
           rel_bias, ln_mlp, w_up, w_down, ln_final):
    bsz, s, d = x.shape
    depth = ln_mix.shape[0]
    aw = A_GROUPS * A_GROUP_DIM
    bw = B_HEADS * B_HEAD_DIM
    assert w_in.shape[2] == 2 * aw + 3 * bw + 2 * d
    qkv_scale = jnp.concatenate([jnp.ones((bw,), F32), jnp.full((bw,), K_SCALE, F32), jnp.ones((bw,), F32)])
    outs = []
    for b in range(bsz):
        xb = x[b]
        for l in range(depth):
            last = l == depth - 1
            h = _norm(xb, ln_mix[l])
            zuv = _in_proj(h, w_in[l], 0, 2 * aw, "gelu")
            zqkv = _in_proj(h, w_in[l], 2 * aw, 3 * bw, col_scale=qkv_scale)
            zg = _in_proj(h, w_in[l], 2 * aw + 3 * bw, 2 * d, "sigmoid")
            gaya = _gmlp(zuv, zg, a_v_gain[l], a_spatial[l], a_spatial_bias[l], w_proj_a[l].astype(BF16))
            attn = _moba(zqkv, rel_bias)
            x1, h2 = _merge(attn, zg, gaya, xb, w_proj_b[l].astype(BF16), w_out[l].astype(BF16), ln_mlp[l])
            gain = ln_final if last else jnp.ones((d,), F32)
            xb = _ffn(h2, w_up[l], w_down[l], x1, gain, final_norm=last)
        outs.append(xb)
    return jnp.stack(outs)
```

```python
import functools
import math

import jax
import jax.numpy as jnp
from jax import lax
from jax.experimental import pallas as pl
from jax.experimental.pallas import tpu as pltpu

A_GROUPS = 16
A_GROUP_DIM = 128
A_CHUNK = 128
B_HEADS = 16
B_HEAD_DIM = 128
MOBA_BLOCK = 256
MOBA_TOPK = 3
FAR_GROUP = 2
Q_STEP_BLOCKS = 2
PAIRS_PER_ITER = 2
SCORE_SLOTS = 2
GATE_STEPS = 4
PREP_UNROLL = 4
REL_BUCKETS = 32
REL_MAX_DIST = 128
EPS = 1e-6
NEG = -1e30
LOG2E = math.log2(math.e)
NEG_LOG2 = NEG * LOG2E
FAR_MASKED = 2 * NEG_LOG2
REMOVED = -3e38
K_SCALE = (B_HEAD_DIM ** -0.5) * LOG2E
BF16_SUBLANES = 16

F32 = jnp.float32
BF16 = jnp.bfloat16

MIB = 1024 * 1024
NT_DIMS = (((1,), (1,)), ((), ()))

V7X_VMEM_MIB = 64
NORM_ROWS = 512
PROJ_ROWS = 1024
PROJ_COLS = 1024
MIX_ROWS = 512
FFN_ROWS = 1024
FFN_COLS = 512


def _params(semantics, window_bytes):
    limit = min(window_bytes + V7X_VMEM_MIB * MIB // 4, V7X_VMEM_MIB * MIB * 7 // 8)
    return pltpu.CompilerParams(dimension_semantics=semantics, vmem_limit_bytes=limit)


def _nbytes(shape, dtype, buffers=1):
    return math.prod(shape) * jnp.dtype(dtype).itemsize * buffers


def _resident(shape, index_map):
    return pl.BlockSpec(shape, index_map, pipeline_mode=pl.Buffered(1))


def _rms(xf, gain):
    return xf * lax.rsqrt(jnp.mean(xf * xf, axis=-1, keepdims=True) + EPS) * gain


def _norm_kernel(x_ref, g_ref, h_ref):
    h_ref[...] = _rms(x_ref[...], g_ref[...]).astype(h_ref.dtype)


def _norm(x, gain, tm=NORM_ROWS):
    s, d = x.shape
    windows = _nbytes((tm, d), F32, 2) + _nbytes((tm, d), BF16, 2)
    return pl.pallas_call(
        _norm_kernel,
        grid=(s // tm,),
        in_specs=[pl.BlockSpec((tm, d), lambda i: (i, 0)),
                  pl.BlockSpec((1, d), lambda i: (0, 0))],
        out_specs=pl.BlockSpec((tm, d), lambda i: (i, 0)),
        out_shape=jax.ShapeDtypeStruct((s, d), BF16),
        compiler_params=_params(("arbitrary",), windows),
        name="rms_norm",
    )(x, gain.reshape(1, d))


def _gelu(a):
    return 0.5 * a * (1.0 + lax.erf(a * math.sqrt(0.5)))


_EPILOGUES = {"gelu": _gelu, "sigmoid": jax.nn.sigmoid}


def _proj_kernel(h_ref, w_ref, o_ref, *, epilogue):
    acc = jnp.dot(h_ref[...], w_ref[...].astype(BF16), preferred_element_type=F32)
    o_ref[...] = _EPILOGUES[epilogue](acc).astype(o_ref.dtype)


def _proj_scale_kernel(h_ref, w_ref, scale_ref, o_ref):
    acc = jnp.dot(h_ref[...], w_ref[...].astype(BF16), preferred_element_type=F32)
    o_ref[...] = (acc * scale_ref[...]).astype(o_ref.dtype)


def _in_proj(h, w, col0, ncols, epilogue=None, col_scale=None, tm=PROJ_ROWS, tn=PROJ_COLS):
    s, d = h.shape
    jb = col0 // tn
    in_specs = [pl.BlockSpec((tm, d), lambda i, j: (i, 0)),
                pl.BlockSpec((d, tn), lambda i, j: (0, j + jb))]
    if col_scale is None:
        body, args, name = functools.partial(_proj_kernel, epilogue=epilogue), (h, w), epilogue
    else:
        in_specs.append(pl.BlockSpec((1, tn), lambda i, j: (0, j)))
        body, args, name = _proj_scale_kernel, (h, w, col_scale.reshape(1, ncols)), "scale"
    windows = _nbytes((tm, d), BF16, 2) + _nbytes((d, tn), F32, 2) + _nbytes((tm, tn), BF16, 2)
    return pl.pallas_call(
        body,
        grid=(s // tm, ncols // tn),
        in_specs=in_specs,
        out_specs=pl.BlockSpec((tm, tn), lambda i, j: (i, j)),
        out_shape=jax.ShapeDtypeStruct((s, ncols), BF16),
        compiler_params=_params(("arbitrary", "arbitrary"), windows),
        name="in_proj_" + name,
    )(*args)


def _gmlp_kernel(u_ref, v_ref, ga_ref, gain_ref, ws_ref, bt_ref, wpa_ref, o_ref, wm_ref, a_ref):
    tq = u_ref.shape[0]

    @pl.when(pl.program_id(0) == 0)
    def _():
        t = lax.broadcasted_iota(jnp.int32, (A_CHUNK, A_CHUNK), 0)
        s = lax.broadcasted_iota(jnp.int32, (A_CHUNK, A_CHUNK), 1)
        for g in range(A_GROUPS):
            wm_ref[g] = jnp.where(t >= s, ws_ref[g], 0.0).astype(BF16)

    def chunk(c, carry):
        r0 = pl.multiple_of(c * A_CHUNK, A_CHUNK)
        v = v_ref[pl.ds(r0, A_CHUNK), :].astype(F32)
        mu = jnp.mean(v, axis=-1, keepdims=True)
        d = v - mu
        var = jnp.mean(d * d, axis=-1, keepdims=True)
        vn = (d * lax.rsqrt(var + EPS) * gain_ref[...]).astype(BF16)
        for g in range(A_GROUPS):
            cols = slice(g * A_GROUP_DIM, (g + 1) * A_GROUP_DIM)
            sv = jnp.dot(wm_ref[g], vn[:, cols], preferred_element_type=F32) + bt_ref[:, g:g + 1]
            u = u_ref[pl.ds(r0, A_CHUNK), cols].astype(F32)
            a_ref[pl.ds(r0, A_CHUNK), cols] = (u * sv).astype(BF16)
        return carry

    lax.fori_loop(0, tq // A_CHUNK, chunk, 0)
    y = jnp.dot(a_ref[...], wpa_ref[...], preferred_element_type=F32)
    o_ref[...] = ga_ref[...].astype(F32) * y


def _gmlp(zuv, zg, gain, w_s, b_s, w_proj_a, tq=MIX_ROWS):
    s = zuv.shape[0]
    aw = A_GROUPS * A_GROUP_DIM
    d = w_proj_a.shape[1]
    windows = (_nbytes((tq, aw), BF16, 4) + _nbytes((tq, d), BF16, 2) + _nbytes((tq, d), F32, 2)
               + _nbytes((aw, d), BF16) + _nbytes((A_GROUPS, A_CHUNK, A_CHUNK), F32)
               + _nbytes((A_GROUPS, A_CHUNK, A_CHUNK), BF16) + _nbytes((tq, aw), BF16))
    return pl.pallas_call(
        _gmlp_kernel,
        grid=(s // tq,),
        in_specs=[pl.BlockSpec((tq, aw), lambda i: (i, 0)),
                  pl.BlockSpec((tq, aw), lambda i: (i, 1)),
                  pl.BlockSpec((tq, d), lambda i: (i, 0)),
                  _resident((1, aw), lambda i: (0, 0)),
                  _resident((A_GROUPS, A_CHUNK, A_CHUNK), lambda i: (0, 0, 0)),
                  _resident((A_CHUNK, A_GROUPS), lambda i: (0, 0)),
                  _resident((aw, d), lambda i: (0, 0))],
        out_specs=pl.BlockSpec((tq, d), lambda i: (i, 0)),
        out_shape=jax.ShapeDtypeStruct((s, d), F32),
        scratch_shapes=[pltpu.VMEM((A_GROUPS, A_CHUNK, A_CHUNK), BF16),
                        pltpu.VMEM((tq, aw), BF16)],
        compiler_params=_params(("arbitrary",), windows),
        name="gmlp_mixer",
    )(zuv, zuv, zg, gain.reshape(1, aw), w_s, b_s.T, w_proj_a)


def _t5_bias_log2(dist, rb_ref, base):
    max_exact = REL_BUCKETS // 2
    n = jnp.maximum(dist, 0)
    nf = jnp.maximum(n, 1).astype(F32)
    large = max_exact + (jnp.log(nf / max_exact) / math.log(REL_MAX_DIST / max_exact)
                         * (REL_BUCKETS - max_exact)).astype(jnp.int32)
    large = jnp.minimum(large, REL_BUCKETS - 1)
    bucket = jnp.where(n < max_exact, n, large)
    out = jnp.zeros(dist.shape, F32)
    for b in range(REL_BUCKETS):
        out = jnp.where(bucket == b, rb_ref[base + b] * LOG2E, out)
    return out


def _moba_kernel(rb_ref, q_ref, k_ref, v_ref, o_ref,
                 kmean_ref, vt_ref, tile_ref, rows_ref, m_ref, acc_ref, *slots):
    L = MOBA_BLOCK
    G = FAR_GROUP
    QB = Q_STEP_BLOCKS
    W = QB * L
    dh = B_HEAD_DIM
    nb = q_ref.shape[0] // L
    nq = nb // QB
    log2_l = L.bit_length() - 1
    base = pl.program_id(0) * REL_BUCKETS
    far_bias = rb_ref[base + REL_BUCKETS - 1] * LOG2E

    def prep(it, carry):
        for i in range(PREP_UNROLL):
            n = it * PREP_UNROLL + i
            r0 = pl.multiple_of(n * L, L)
            kmean_ref[pl.ds(n, 1), :] = jnp.mean(k_ref[pl.ds(r0, L), :].astype(F32), axis=0, keepdims=True)
            vt_ref[n, 0:dh, :] = v_ref[pl.ds(r0, L), :].astype(F32).T.astype(BF16)
            vt_ref[n, dh:, :] = jnp.ones((vt_ref.shape[1] - dh, L), BF16)
        return carry

    lax.fori_loop(0, nb // PREP_UNROLL, prep, 0)

    dist = lax.broadcasted_iota(jnp.int32, (1, 2 * L), 1) - L

    def toeplitz(row):
        return pltpu.roll(jnp.broadcast_to(row, (L, 2 * L)), L, 1, stride=1, stride_axis=0)[:, :L]

    b_prev = toeplitz(_t5_bias_log2(dist + L, rb_ref, base))
    b_own = toeplitz(jnp.where(dist >= 0, _t5_bias_log2(dist, rb_ref, base), NEG_LOG2))
    masked = jnp.full((L, L), NEG_LOG2, F32)
    zero = jnp.zeros((L, L), F32)
    kinds = [[[zero, zero], [zero, zero]], [[zero, zero], [b_prev, zero]], [[b_own, b_prev], [masked, b_own]]]
    for kind, blocks in enumerate(kinds):
        for r, tiles in enumerate(blocks):
            for c, tile in enumerate(tiles):
                tile_ref[kind, r * L:(r + 1) * L, c * L:(c + 1) * L] = tile

    kmean = kmean_ref[...].astype(BF16)

    def gate(it, carry):
        u0 = it * GATE_STEPS
        q_c = q_ref[pl.ds(pl.multiple_of(u0 * W, GATE_STEPS * W), GATE_STEPS * W), :]
        gs = lax.dot_general(kmean, q_c, NT_DIMS, preferred_element_type=F32) * (1.0 / K_SCALE)
        blk = lax.broadcasted_iota(jnp.int32, gs.shape, 0)
        own = QB * u0 + lax.shift_right_logical(lax.broadcasted_iota(jnp.int32, gs.shape, 1), log2_l)
        gs = jnp.where(blk < own, gs, NEG)
        sel = jnp.zeros(gs.shape, F32)
        for _ in range(MOBA_TOPK):
            top = jnp.max(gs, axis=0, keepdims=True)
            idx = jnp.min(jnp.where(gs == top, blk, nb), axis=0, keepdims=True)
            hit = blk == idx
            sel = jnp.where(hit, jnp.where(top > NEG / 2, 1.0, 0.0), sel)
            gs = jnp.where(hit, REMOVED, gs)
        picked = sel > 0.5
        rows = jnp.where(blk < own - 1, jnp.where(picked, far_bias, FAR_MASKED),
                         jnp.where(blk == own - 1, jnp.where(picked, 0.0, FAR_MASKED), 0.0))
        for i in range(GATE_STEPS):
            rows_ref[u0 + i] = rows[:, i * W:(i + 1) * W]
        return carry

    lax.fori_loop(0, nq // GATE_STEPS, gate, 0)

    def item(t, ua):
        first = t <= ua
        u = jnp.where(first, ua, nq - 1 - ua)
        g = jnp.where(first, t, t - ua - 1)
        return u, g, jnp.where(first, 0, 1), jnp.clip(g - (u - 2), 0, 2)

    def rows_of(u, g):
        return [rows_ref[u, pl.ds(g * G + b, 1), :] for b in range(G)]

    def produce(n, t, ua):
        u, g, _, kind = item(t, ua)
        q_u = q_ref[pl.ds(pl.multiple_of(u * W, W), W), :]
        k_g = k_ref[pl.ds(pl.multiple_of(g * (G * L), G * L), G * L), :]
        s_t = lax.dot_general(k_g, q_u, NT_DIMS, preferred_element_type=F32) + tile_ref[kind]
        slots[n % len(slots)][...] = s_t
        rows = rows_of(u, g)
        return functools.reduce(jnp.maximum, [
            jnp.max(s_t[b * L:(b + 1) * L], axis=0, keepdims=True) + rows[b] for b in range(G)])

    def consume(n, t, ua, pi, cand):
        u, g, w, _ = item(t, ua)
        rows = rows_of(u, g)
        s_ref = slots[n % len(slots)]
        st = pi * 2 + w
        m_old = m_ref[st]
        m_new = jnp.maximum(m_old, cand)
        pv = None
        for b in range(G):
            p = jnp.exp2(s_ref[b * L:(b + 1) * L, :] + (rows[b] - m_new)).astype(BF16)
            d = jnp.dot(vt_ref[g * G + b], p, preferred_element_type=F32)
            pv = d if pv is None else pv + d
        acc_ref[st] = jnp.exp2(m_old - m_new) * acc_ref[st] + pv
        m_ref[st] = m_new

    def finish(ua, pi):
        for w, u in ((0, ua), (1, nq - 1 - ua)):
            acc = acc_ref[pi * 2 + w]
            o_ref[pl.ds(pl.multiple_of(u * W, W), W), :] = (acc[:dh] / acc[dh:dh + 1]).T.astype(o_ref.dtype)

    def stream(it, carry):
        m_ref[...] = jnp.full(m_ref.shape, NEG_LOG2, F32)
        acc_ref[...] = jnp.zeros(acc_ref.shape, F32)
        seq = [(pi, t) for pi in range(PAIRS_PER_ITER) for t in range(nq + 1)]
        first_step = lambda pi: it * PAIRS_PER_ITER + pi
        cand = produce(0, seq[0][1], first_step(seq[0][0]))
        for n, (pi, t) in enumerate(seq):
            if n + 1 < len(seq):
                cand_next = produce(n + 1, seq[n + 1][1], first_step(seq[n + 1][0]))
            consume(n, t, first_step(pi), pi, cand)
            cand = cand_next
            if t == nq:
                finish(first_step(pi), pi)
        return carry

    lax.fori_loop(0, nq // 2 // PAIRS_PER_ITER, stream, 0)


def _moba(zqkv, rel_bias):
    s = zqkv.shape[0]
    dh = B_HEAD_DIM
    L = MOBA_BLOCK
    nb = s // L
    assert Q_STEP_BLOCKS == 2 and FAR_GROUP == 2 and nb % (2 * Q_STEP_BLOCKS) == 0
    assert nb % PREP_UNROLL == 0 and (nb // Q_STEP_BLOCKS) % GATE_STEPS == 0
    assert (nb // Q_STEP_BLOCKS // 2) % PAIRS_PER_ITER == 0
    w = Q_STEP_BLOCKS * L
    rb = rel_bias.T.reshape(-1).astype(F32)
    scratch = [pltpu.VMEM((nb, dh), F32),
               pltpu.VMEM((nb, dh + BF16_SUBLANES, L), BF16),
               pltpu.VMEM((3, FAR_GROUP * L, w), F32),
               pltpu.VMEM((nb // Q_STEP_BLOCKS, nb, w), F32),
               pltpu.VMEM((2 * PAIRS_PER_ITER, 1, w), F32),
               pltpu.VMEM((2 * PAIRS_PER_ITER, dh + BF16_SUBLANES, w), F32)]
    scratch += [pltpu.VMEM((FAR_GROUP * L, w), F32)] * SCORE_SLOTS
    windows = _nbytes((s, dh), BF16, 8) + sum(_nbytes(b.shape, b.dtype) for b in scratch)
    return pl.pallas_call(
        _moba_kernel,
        grid=(B_HEADS,),
        in_specs=[pl.BlockSpec(memory_space=pltpu.SMEM),
                  pl.BlockSpec((s, dh), lambda h: (0, h)),
                  pl.BlockSpec((s, dh), lambda h: (0, B_HEADS + h)),
                  pl.BlockSpec((s, dh), lambda h: (0, 2 * B_HEADS + h))],
        out_specs=pl.BlockSpec((s, dh), lambda h: (0, h)),
        out_shape=jax.ShapeDtypeStruct((s, B_HEADS * dh), BF16),
        scratch_shapes=scratch,
        compiler_params=_params(("arbitrary",), windows),
        name="moba_attention",
    )(rb, zqkv, zqkv, zqkv)


def _merge_kernel(attn_ref, gb_ref, gaya_ref, x_ref, wpb_ref, wout_ref, g_ref, x1_ref, h2_ref):
    y_b = jnp.dot(attn_ref[...], wpb_ref[...], preferred_element_type=F32)
    merged = gaya_ref[...] + gb_ref[...].astype(F32) * y_b
    x1 = x_ref[...] + jnp.dot(merged.astype(BF16), wout_ref[...], preferred_element_type=F32)
    x1_ref[...] = x1
    h2_ref[...] = _rms(x1, g_ref[...]).astype(h2_ref.dtype)


def _merge(attn, zg, gaya, x, w_proj_b, w_out, gain, tq=MIX_ROWS):
    s, d = x.shape
    bw = attn.shape[1]
    windows = (_nbytes((tq, bw), BF16, 2) + _nbytes((tq, d), BF16, 2) + _nbytes((tq, d), F32, 4)
               + _nbytes((bw, d), BF16) + _nbytes((d, d), BF16)
               + _nbytes((tq, d), F32, 2) + _nbytes((tq, d), BF16, 2))
    return pl.pallas_call(
        _merge_kernel,
        grid=(s // tq,),
        in_specs=[pl.BlockSpec((tq, bw), lambda i: (i, 0)),
                  pl.BlockSpec((tq, d), lambda i: (i, 1)),
                  pl.BlockSpec((tq, d), lambda i: (i, 0)),
                  pl.BlockSpec((tq, d), lambda i: (i, 0)),
                  _resident((bw, d), lambda i: (0, 0)),
                  _resident((d, d), lambda i: (0, 0)),
                  _resident((1, d), lambda i: (0, 0))],
        out_specs=[pl.BlockSpec((tq, d), lambda i: (i, 0)),
                   pl.BlockSpec((tq, d), lambda i: (i, 0))],
        out_shape=[jax.ShapeDtypeStruct((s, d), F32),
                   jax.ShapeDtypeStruct((s, d), BF16)],
        compiler_params=_params(("arbitrary",), windows),
        name="merge_out_proj",
    )(attn, zg, gaya, x, w_proj_b, w_out, gain.reshape(1, d))


def _ffn_kernel(h_ref, wu_ref, wd_ref, x1_ref, g_ref, o_ref, *, final_norm):
    j = pl.program_id(1)

    @pl.when(j == 0)
    def _():
        o_ref[...] = x1_ref[...]

    a = jnp.dot(h_ref[...], wu_ref[...].astype(BF16), preferred_element_type=F32)
    a = jnp.square(jnp.maximum(a, 0.0)).astype(BF16)
    o_ref[...] += jnp.dot(a, wd_ref[...].astype(BF16), preferred_element_type=F32)

    if final_norm:
        @pl.when(j == pl.num_programs(1) - 1)
        def _():
            o_ref[...] = _rms(o_ref[...], g_ref[...])


def _ffn(h2, w_up, w_down, x1, gain, final_norm, tm=FFN_ROWS, tf=FFN_COLS):
    s, d = x1.shape
    dff = w_up.shape[1]
    windows = (_nbytes((tm, d), BF16, 2) + _nbytes((d, tf), F32, 4) + _nbytes((tm, d), F32)
               + _nbytes((tm, d), F32, 2))
    return pl.pallas_call(
        functools.partial(_ffn_kernel, final_norm=final_norm),
        grid=(s // tm, dff // tf),
        in_specs=[pl.BlockSpec((tm, d), lambda i, j: (i, 0)),
                  pl.BlockSpec((d, tf), lambda i, j: (0, j)),
                  pl.BlockSpec((tf, d), lambda i, j: (j, 0)),
                  _resident((tm, d), lambda i, j: (i, 0)),
                  _resident((1, d), lambda i, j: (0, 0))],
        out_specs=pl.BlockSpec((tm, d), lambda i, j: (i, 0)),
        out_shape=jax.ShapeDtypeStruct((s, d), F32),
        compiler_params=_params(("arbitrary", "arbitrary"), windows),
        name="ffn_relu2",
    )(h2, w_up, w_down, x1, gain.reshape(1, d))


def kernel(x, ln_mix, w_in, a_v_gain, a_spatial, a_spatial_bias, w_proj_a, w_proj_b, w_out,
           rel_bias, ln_mlp, w_up, w_down, ln_final):
    bsz, s, d = x.shape
    depth = ln_mix.shape[0]
    aw = A_GROUPS * A_GROUP_DIM
    bw = B_HEADS * B_HEAD_DIM
    assert w_in.shape[2] == 2 * aw + 3 * bw + 2 * d
    qkv_scale = jnp.concatenate([jnp.ones((bw,), F32), jnp.full((bw,), K_SCALE, F32), jnp.ones((bw,), F32)])
    outs = []
    for b in range(bsz):
        xb = x[b]
        for l in range(depth):
            last = l == depth - 1
            h = _norm(xb, ln_mix[l])
            zuv = _in_proj(h, w_in[l], 0, 2 * aw, "gelu")
            zqkv = _in_proj(h, w_in[l], 2 * aw, 3 * bw, col_scale=qkv_scale)
            zg = _in_proj(h, w_in[l], 2 * aw + 3 * bw, 2 * d, "sigmoid")
            gaya = _gmlp(zuv, zg, a_v_gain[l], a_spatial[l], a_spatial_bias[l], w_proj_a[l].astype(BF16))
            attn = _moba(zqkv, rel_bias)
            x1, h2 = _merge(attn, zg, gaya, xb, w_proj_b[l].astype(BF16), w_out[l].astype(BF16), ln_mlp[l])
            gain = ln_final if last else jnp.ones((d,), F32)
            xb = _ffn(h2, w_up[l], w_down[l], x1, gain, final_norm=last)
        outs.append(xb)
    return jnp.stack(outs)
```

```python
import functools
import math

import jax
import jax.numpy as jnp
from jax import lax
from jax.experimental import pallas as pl
from jax.experimental.pallas import tpu as pltpu

A_GROUPS = 16
A_GROUP_DIM = 128
A_CHUNK = 128
B_HEADS = 16
B_HEAD_DIM = 128
MOBA_BLOCK = 256
MOBA_TOPK = 3
FAR_GROUP = 2
Q_STEP_BLOCKS = 2
PAIRS_PER_ITER = 2
SCORE_SLOTS = 2
GATE_STEPS = 8
PREP_UNROLL = 8
REL_BUCKETS = 32
REL_MAX_DIST = 128
EPS = 1e-6
NEG = -1e30
LOG2E = math.log2(math.e)
NEG_LOG2 = NEG * LOG2E
FAR_MASKED = 2 * NEG_LOG2
REMOVED = -3e38
K_SCALE = (B_HEAD_DIM ** -0.5) * LOG2E
BF16_SUBLANES = 16

F32 = jnp.float32
BF16 = jnp.bfloat16

MIB = 1024 * 1024
NT_DIMS = (((1,), (1,)), ((), ()))

V7X_VMEM_MIB = 64
NORM_ROWS = 512
PROJ_ROWS = 1024
PROJ_COLS = 1024
MIX_ROWS = 512
FFN_ROWS = 1024
FFN_COLS = 512


def _params(semantics, window_bytes):
    limit = min(window_bytes + V7X_VMEM_MIB * MIB // 4, V7X_VMEM_MIB * MIB * 7 // 8)
    return pltpu.CompilerParams(dimension_semantics=semantics, vmem_limit_bytes=limit)


def _nbytes(shape, dtype, buffers=1):
    return math.prod(shape) * jnp.dtype(dtype).itemsize * buffers


def _resident(shape, index_map):
    return pl.BlockSpec(shape, index_map, pipeline_mode=pl.Buffered(1))


def _rms(xf, gain):
    return xf * lax.rsqrt(jnp.mean(xf * xf, axis=-1, keepdims=True) + EPS) * gain


def _norm_kernel(x_ref, g_ref, h_ref):
    h_ref[...] = _rms(x_ref[...], g_ref[...]).astype(h_ref.dtype)


def _norm(x, gain):
    s, d = x.shape
    tm = NORM_ROWS
    windows = _nbytes((tm, d), F32, 2) + _nbytes((tm, d), BF16, 2)
    return pl.pallas_call(
        _norm_kernel,
        grid=(s // tm,),
        in_specs=[pl.BlockSpec((tm, d), lambda i: (i, 0)),
                  pl.BlockSpec((1, d), lambda i: (0, 0))],
        out_specs=pl.BlockSpec((tm, d), lambda i: (i, 0)),
        out_shape=jax.ShapeDtypeStruct((s, d), BF16),
        compiler_params=_params(("arbitrary",), windows),
        name="rms_norm",
    )(x, gain.reshape(1, d))


def _gelu(a):
    return 0.5 * a * (1.0 + lax.erf(a * math.sqrt(0.5)))


_EPILOGUES = {"gelu": _gelu, "sigmoid": jax.nn.sigmoid}


def _proj_kernel(h_ref, w_ref, o_ref, *, epilogue):
    acc = jnp.dot(h_ref[...], w_ref[...].astype(BF16), preferred_element_type=F32)
    o_ref[...] = _EPILOGUES[epilogue](acc).astype(o_ref.dtype)


def _proj_scale_kernel(h_ref, w_ref, scale_ref, o_ref):
    acc = jnp.dot(h_ref[...], w_ref[...].astype(BF16), preferred_element_type=F32)
    o_ref[...] = (acc * scale_ref[...]).astype(o_ref.dtype)


def _in_proj(h, w, col0, ncols, epilogue=None, col_scale=None):
    s, d = h.shape
    tm, tn = PROJ_ROWS, PROJ_COLS
    jb = col0 // tn
    in_specs = [pl.BlockSpec((tm, d), lambda i, j: (i, 0)),
                pl.BlockSpec((d, tn), lambda i, j: (0, j + jb))]
    if col_scale is None:
        body, args, name = functools.partial(_proj_kernel, epilogue=epilogue), (h, w), epilogue
    else:
        in_specs.append(pl.BlockSpec((1, tn), lambda i, j: (0, j)))
        body, args, name = _proj_scale_kernel, (h, w, col_scale.reshape(1, ncols)), "scale"
    windows = _nbytes((tm, d), BF16, 2) + _nbytes((d, tn), F32, 2) + _nbytes((tm, tn), BF16, 2)
    return pl.pallas_call(
        body,
        grid=(s // tm, ncols // tn),
        in_specs=in_specs,
        out_specs=pl.BlockSpec((tm, tn), lambda i, j: (i, j)),
        out_shape=jax.ShapeDtypeStruct((s, ncols), BF16),
        compiler_params=_params(("arbitrary", "arbitrary"), windows),
        name="in_proj_" + name,
    )(*args)


def _gmlp_kernel(u_ref, v_ref, ga_ref, gain_ref, ws_ref, bt_ref, wpa_ref, o_ref, wm_ref, a_ref):
    tq = u_ref.shape[0]

    @pl.when(pl.program_id(0) == 0)
    def _():
        t = lax.broadcasted_iota(jnp.int32, (A_CHUNK, A_CHUNK), 0)
        s = lax.broadcasted_iota(jnp.int32, (A_CHUNK, A_CHUNK), 1)
        for g in range(A_GROUPS):
            wm_ref[g] = jnp.where(t >= s, ws_ref[g], 0.0).astype(BF16)

    def chunk(c, carry):
        r0 = pl.multiple_of(c * A_CHUNK, A_CHUNK)
        v = v_ref[pl.ds(r0, A_CHUNK), :].astype(F32)
        mu = jnp.mean(v, axis=-1, keepdims=True)
        d = v - mu
        var = jnp.mean(d * d, axis=-1, keepdims=True)
        vn = (d * lax.rsqrt(var + EPS) * gain_ref[...]).astype(BF16)
        for g in range(A_GROUPS):
            cols = slice(g * A_GROUP_DIM, (g + 1) * A_GROUP_DIM)
            sv = jnp.dot(wm_ref[g], vn[:, cols], preferred_element_type=F32) + bt_ref[:, g:g + 1]
            u = u_ref[pl.ds(r0, A_CHUNK), cols].astype(F32)
            a_ref[pl.ds(r0, A_CHUNK), cols] = (u * sv).astype(BF16)
        return carry

    lax.fori_loop(0, tq // A_CHUNK, chunk, 0)
    y = jnp.dot(a_ref[...], wpa_ref[...], preferred_element_type=F32)
    o_ref[...] = ga_ref[...].astype(F32) * y


def _gmlp(zuv, zg, gain, w_s, b_s, w_proj_a):
    s = zuv.shape[0]
    tq = MIX_ROWS
    aw = A_GROUPS * A_GROUP_DIM
    d = w_proj_a.shape[1]
    windows = (_nbytes((tq, aw), BF16, 4) + _nbytes((tq, d), BF16, 2) + _nbytes((tq, d), F32, 2)
               + _nbytes((aw, d), BF16) + _nbytes((A_GROUPS, A_CHUNK, A_CHUNK), F32)
               + _nbytes((A_GROUPS, A_CHUNK, A_CHUNK), BF16) + _nbytes((tq, aw), BF16))
    return pl.pallas_call(
        _gmlp_kernel,
        grid=(s // tq,),
        in_specs=[pl.BlockSpec((tq, aw), lambda i: (i, 0)),
                  pl.BlockSpec((tq, aw), lambda i: (i, 1)),
                  pl.BlockSpec((tq, d), lambda i: (i, 0)),
                  _resident((1, aw), lambda i: (0, 0)),
                  _resident((A_GROUPS, A_CHUNK, A_CHUNK), lambda i: (0, 0, 0)),
                  _resident((A_CHUNK, A_GROUPS), lambda i: (0, 0)),
                  _resident((aw, d), lambda i: (0, 0))],
        out_specs=pl.BlockSpec((tq, d), lambda i: (i, 0)),
        out_shape=jax.ShapeDtypeStruct((s, d), F32),
        scratch_shapes=[pltpu.VMEM((A_GROUPS, A_CHUNK, A_CHUNK), BF16),
                        pltpu.VMEM((tq, aw), BF16)],
        compiler_params=_params(("arbitrary",), windows),
        name="gmlp_mixer",
    )(zuv, zuv, zg, gain.reshape(1, aw), w_s, b_s.T, w_proj_a)


def _t5_bias_log2(dist, rb_ref, base):
    max_exact = REL_BUCKETS // 2
    n = jnp.maximum(dist, 0)
    nf = jnp.maximum(n, 1).astype(F32)
    large = max_exact + (jnp.log(nf / max_exact) / math.log(REL_MAX_DIST / max_exact)
                         * (REL_BUCKETS - max_exact)).astype(jnp.int32)
    large = jnp.minimum(large, REL_BUCKETS - 1)
    bucket = jnp.where(n < max_exact, n, large)
    out = jnp.zeros(dist.shape, F32)
    for b in range(REL_BUCKETS):
        out = jnp.where(bucket == b, rb_ref[base + b] * LOG2E, out)
    return out


def _moba_kernel(rb_ref, q_ref, k_ref, v_ref, o_ref,
                 kmean_ref, vt_ref, tile_ref, rows_ref, m_ref, acc_ref, *slots):
    L = MOBA_BLOCK
    G = FAR_GROUP
    QB = Q_STEP_BLOCKS
    W = QB * L
    dh = B_HEAD_DIM
    nb = q_ref.shape[0] // L
    nq = nb // QB
    log2_l = L.bit_length() - 1
    base = pl.program_id(0) * REL_BUCKETS
    far_bias = rb_ref[base + REL_BUCKETS - 1] * LOG2E

    def prep(it, carry):
        for i in range(PREP_UNROLL):
            n = it * PREP_UNROLL + i
            r0 = pl.multiple_of(n * L, L)
            kmean_ref[pl.ds(n, 1), :] = jnp.mean(k_ref[pl.ds(r0, L), :].astype(F32), axis=0, keepdims=True)
            vt_ref[n, 0:dh, :] = v_ref[pl.ds(r0, L), :].astype(F32).T.astype(BF16)
            vt_ref[n, dh:, :] = jnp.ones((vt_ref.shape[1] - dh, L), BF16)
        return carry

    lax.fori_loop(0, nb // PREP_UNROLL, prep, 0)

    dist = lax.broadcasted_iota(jnp.int32, (1, 2 * L), 1) - L

    def toeplitz(row):
        return pltpu.roll(jnp.broadcast_to(row, (L, 2 * L)), L, 1, stride=1, stride_axis=0)[:, :L]

    b_prev = toeplitz(_t5_bias_log2(dist + L, rb_ref, base))
    b_own = toeplitz(jnp.where(dist >= 0, _t5_bias_log2(dist, rb_ref, base), NEG_LOG2))
    masked = jnp.full((L, L), NEG_LOG2, F32)
    zero = jnp.zeros((L, L), F32)
    kinds = [[[zero, zero], [zero, zero]], [[zero, zero], [b_prev, zero]], [[b_own, b_prev], [masked, b_own]]]
    for kind, blocks in enumerate(kinds):
        for r, tiles in enumerate(blocks):
            for c, tile in enumerate(tiles):
                tile_ref[kind, r * L:(r + 1) * L, c * L:(c + 1) * L] = tile

    kmean = kmean_ref[...].astype(BF16)

    def gate(it, carry):
        u0 = it * GATE_STEPS
        q_c = q_ref[pl.ds(pl.multiple_of(u0 * W, GATE_STEPS * W), GATE_STEPS * W), :]
        gs = lax.dot_general(kmean, q_c, NT_DIMS, preferred_element_type=F32) * (1.0 / K_SCALE)
        blk = lax.broadcasted_iota(jnp.int32, gs.shape, 0)
        own = QB * u0 + lax.shift_right_logical(lax.broadcasted_iota(jnp.int32, gs.shape, 1), log2_l)
        gs = jnp.where(blk < own, gs, NEG)
        sel = jnp.zeros(gs.shape, F32)
        for _ in range(MOBA_TOPK):
            top = jnp.max(gs, axis=0, keepdims=True)
            idx = jnp.min(jnp.where(gs == top, blk, nb), axis=0, keepdims=True)
            hit = blk == idx
            sel = jnp.where(hit, jnp.where(top > NEG / 2, 1.0, 0.0), sel)
            gs = jnp.where(hit, REMOVED, gs)
        picked = sel > 0.5
        rows = jnp.where(blk < own - 1, jnp.where(picked, far_bias, FAR_MASKED),
                         jnp.where(blk == own - 1, jnp.where(picked, 0.0, FAR_MASKED), 0.0))
        for i in range(GATE_STEPS):
            rows_ref[u0 + i] = rows[:, i * W:(i + 1) * W]
        return carry

    lax.fori_loop(0, nq // GATE_STEPS, gate, 0)

    def item(t, ua):
        first = t <= ua
        u = jnp.where(first, ua, nq - 1 - ua)
        g = jnp.where(first, t, t - ua - 1)
        return u, g, jnp.where(first, 0, 1), jnp.clip(g - (u - 2), 0, 2)

    def rows_of(u, g):
        return [rows_ref[u, pl.ds(g * G + b, 1), :] for b in range(G)]

    def produce(n, t, ua):
        u, g, _, kind = item(t, ua)
        q_u = q_ref[pl.ds(pl.multiple_of(u * W, W), W), :]
        k_g = k_ref[pl.ds(pl.multiple_of(g * (G * L), G * L), G * L), :]
        s_t = lax.dot_general(k_g, q_u, NT_DIMS, preferred_element_type=F32) + tile_ref[kind]
        slots[n % len(slots)][...] = s_t
        rows = rows_of(u, g)
        return functools.reduce(jnp.maximum, [
            jnp.max(s_t[b * L:(b + 1) * L], axis=0, keepdims=True) + rows[b] for b in range(G)])

    def consume(n, t, ua, pi, cand):
        u, g, w, _ = item(t, ua)
        rows = rows_of(u, g)
        s_ref = slots[n % len(slots)]
        st = pi * 2 + w
        m_old = m_ref[st]
        m_new = jnp.maximum(m_old, cand)
        pv = None
        for b in range(G):
            p = jnp.exp2(s_ref[b * L:(b + 1) * L, :] + (rows[b] - m_new)).astype(BF16)
            d = jnp.dot(vt_ref[g * G + b], p, preferred_element_type=F32)
            pv = d if pv is None else pv + d
        acc_ref[st] = jnp.exp2(m_old - m_new) * acc_ref[st] + pv
        m_ref[st] = m_new

    def finish(ua, pi):
        for w, u in ((0, ua), (1, nq - 1 - ua)):
            acc = acc_ref[pi * 2 + w]
            o_ref[pl.ds(pl.multiple_of(u * W, W), W), :] = (acc[:dh] / acc[dh:dh + 1]).T.astype(o_ref.dtype)

    def stream(it, carry):
        m_ref[...] = jnp.full(m_ref.shape, NEG_LOG2, F32)
        acc_ref[...] = jnp.zeros(acc_ref.shape, F32)
        seq = [(pi, t) for pi in range(PAIRS_PER_ITER) for t in range(nq + 1)]
        first_step = lambda pi: it * PAIRS_PER_ITER + pi
        cand = produce(0, seq[0][1], first_step(seq[0][0]))
        for n, (pi, t) in enumerate(seq):
            if n + 1 < len(seq):
                cand_next = produce(n + 1, seq[n + 1][1], first_step(seq[n + 1][0]))
            consume(n, t, first_step(pi), pi, cand)
            cand = cand_next
            if t == nq:
                finish(first_step(pi), pi)
        return carry

    lax.fori_loop(0, nq // 2 // PAIRS_PER_ITER, stream, 0)


def _moba(zqkv, rel_bias):
    s = zqkv.shape[0]
    dh = B_HEAD_DIM
    L = MOBA_BLOCK
    nb = s // L
    assert Q_STEP_BLOCKS == 2 and FAR_GROUP == 2 and nb % (2 * Q_STEP_BLOCKS) == 0
    assert nb % PREP_UNROLL == 0 and (nb // Q_STEP_BLOCKS) % GATE_STEPS == 0
    assert (nb // Q_STEP_BLOCKS // 2) % PAIRS_PER_ITER == 0
    w = Q_STEP_BLOCKS * L
    rb = rel_bias.T.reshape(-1).astype(F32)
    scratch = [pltpu.VMEM((nb, dh), F32),
               pltpu.VMEM((nb, dh + BF16_SUBLANES, L), BF16),
               pltpu.VMEM((3, FAR_GROUP * L, w), F32),
               pltpu.VMEM((nb // Q_STEP_BLOCKS, nb, w), F32),
               pltpu.VMEM((2 * PAIRS_PER_ITER, 1, w), F32),
               pltpu.VMEM((2 * PAIRS_PER_ITER, dh + BF16_SUBLANES, w), F32)]
    scratch += [pltpu.VMEM((FAR_GROUP * L, w), F32)] * SCORE_SLOTS
    windows = _nbytes((s, dh), BF16, 8) + sum(_nbytes(b.shape, b.dtype) for b in scratch)
    return pl.pallas_call(
        _moba_kernel,
        grid=(B_HEADS,),
        in_specs=[pl.BlockSpec(memory_space=pltpu.SMEM),
                  pl.BlockSpec((s, dh), lambda h: (0, h)),
                  pl.BlockSpec((s, dh), lambda h: (0, B_HEADS + h)),
                  pl.BlockSpec((s, dh), lambda h: (0, 2 * B_HEADS + h))],
        out_specs=pl.BlockSpec((s, dh), lambda h: (0, h)),
        out_shape=jax.ShapeDtypeStruct((s, B_HEADS * dh), BF16),
        scratch_shapes=scratch,
        compiler_params=_params(("arbitrary",), windows),
        name="moba_attention",
    )(rb, zqkv, zqkv, zqkv)


def _merge_kernel(attn_ref, gb_ref, gaya_ref, x_ref, wpb_ref, wout_ref, g_ref, x1_ref, h2_ref):
    y_b = jnp.dot(attn_ref[...], wpb_ref[...], preferred_element_type=F32)
    merged = gaya_ref[...] + gb_ref[...].astype(F32) * y_b
    x1 = x_ref[...] + jnp.dot(merged.astype(BF16), wout_ref[...], preferred_element_type=F32)
    x1_ref[...] = x1
    h2_ref[...] = _rms(x1, g_ref[...]).astype(h2_ref.dtype)


def _merge(attn, zg, gaya, x, w_proj_b, w_out, gain):
    s, d = x.shape
    tq = MIX_ROWS
    bw = attn.shape[1]
    windows = (_nbytes((tq, bw), BF16, 2) + _nbytes((tq, d), BF16, 2) + _nbytes((tq, d), F32, 4)
               + _nbytes((bw, d), BF16) + _nbytes((d, d), BF16)
               + _nbytes((tq, d), F32, 2) + _nbytes((tq, d), BF16, 2))
    return pl.pallas_call(
        _merge_kernel,
        grid=(s // tq,),
        in_specs=[pl.BlockSpec((tq, bw), lambda i: (i, 0)),
                  pl.BlockSpec((tq, d), lambda i: (i, 1)),
                  pl.BlockSpec((tq, d), lambda i: (i, 0)),
                  pl.BlockSpec((tq, d), lambda i: (i, 0)),
                  _resident((bw, d), lambda i: (0, 0)),
                  _resident((d, d), lambda i: (0, 0)),
                  _resident((1, d), lambda i: (0, 0))],
        out_specs=[pl.BlockSpec((tq, d), lambda i: (i, 0)),
                   pl.BlockSpec((tq, d), lambda i: (i, 0))],
        out_shape=[jax.ShapeDtypeStruct((s, d), F32),
                   jax.ShapeDtypeStruct((s, d), BF16)],
        compiler_params=_params(("arbitrary",), windows),
        name="merge_out_proj",
    )(attn, zg, gaya, x, w_proj_b, w_out, gain.reshape(1, d))


def _ffn_kernel(h_ref, wu_ref, wd_ref, x1_ref, g_ref, o_ref, *, final_norm):
    j = pl.program_id(1)

    @pl.when(j == 0)
    def _():
        o_ref[...] = x1_ref[...]

    a = jnp.dot(h_ref[...], wu_ref[...].astype(BF16), preferred_element_type=F32)
    a = jnp.square(jnp.maximum(a, 0.0)).astype(BF16)
    o_ref[...] += jnp.dot(a, wd_ref[...].astype(BF16), preferred_element_type=F32)

    if final_norm:
        @pl.when(j == pl.num_programs(1) - 1)
        def _():
            o_ref[...] = _rms(o_ref[...], g_ref[...])


def _ffn(h2, w_up, w_down, x1, gain, final_norm):
    s, d = x1.shape
    tm, tf = FFN_ROWS, FFN_COLS
    dff = w_up.shape[1]
    windows = (_nbytes((tm, d), BF16, 2) + _nbytes((d, tf), F32, 4) + _nbytes((tm, d), F32)
               + _nbytes((tm, d), F32, 2))
    return pl.pallas_call(
        functools.partial(_ffn_kernel, final_norm=final_norm),
        grid=(s // tm, dff // tf),
        in_specs=[pl.BlockSpec((tm, d), lambda i, j: (i, 0)),
                  pl.BlockSpec((d, tf), lambda i, j: (0, j)),
                  pl.BlockSpec((tf, d), lambda i, j: (j, 0)),
                  _resident((tm, d), lambda i, j: (i, 0)),
                  _resident((1, d), lambda i, j: (0, 0))],
        out_specs=pl.BlockSpec((tm, d), lambda i, j: (i, 0)),
        out_shape=jax.ShapeDtypeStruct((s, d), F32),
        compiler_params=_params(("arbitrary", "arbitrary"), windows),
        name="ffn_relu2",
    )(h2, w_up, w_down, x1, gain.reshape(1, d))


def kernel(x, ln_mix, w_in, a_v_gain, a_spatial, a_spatial_bias, w_proj_a, w_proj_b, w_out,
           rel_bias, ln_mlp, w_up, w_down, ln_final):
    bsz, s, d = x.shape
    depth = ln_mix.shape[0]
    aw = A_GROUPS * A_GROUP_DIM
    bw = B_HEADS * B_HEAD_DIM
    assert w_in.shape[2] == 2 * aw + 3 * bw + 2 * d
    qkv_scale = jnp.concatenate([jnp.ones((bw,), F32), jnp.full((bw,), K_SCALE, F32), jnp.ones((bw,), F32)])
    outs = []
    for b in range(bsz):
        xb = x[b]
        for l in range(depth):
            last = l == depth - 1
            h = _norm(xb, ln_mix[l])
            zuv = _in_proj(h, w_in[l], 0, 2 * aw, "gelu")
            zqkv = _in_proj(h, w_in[l], 2 * aw, 3 * bw, col_scale=qkv_scale)
            zg = _in_proj(h, w_in[l], 2 * aw + 3 * bw, 2 * d, "sigmoid")
            gaya = _gmlp(zuv, zg, a_v_gain[l], a_spatial[l], a_spatial_bias[l], w_proj_a[l].astype(BF16))
            attn = _moba(zqkv, rel_bias)
            x1, h2 = _merge(attn, zg, gaya, xb, w_proj_b[l].astype(BF16), w_out[l].astype(BF16), ln_mlp[l])
            gain = ln_final if last else jnp.ones((d,), F32)
            xb = _ffn(h2, w_up[l], w_down[l], x1, gain, final_norm=last)
        outs.append(xb)
    return jnp.stack(outs)
```

```python
import functools
import math

import jax
import jax.numpy as jnp
from jax import lax
from jax.experimental import pallas as pl
from jax.experimental.pallas import tpu as pltpu

A_GROUPS = 16
A_GROUP_DIM = 128
A_CHUNK = 128
B_HEADS = 16
B_HEAD_DIM = 128
MOBA_BLOCK = 256
MOBA_TOPK = 3
FAR_GROUP = 2
Q_STEP_BLOCKS = 2
PAIRS_PER_ITER = 2
SCORE_SLOTS = 2
GATE_STEPS = 8
PREP_UNROLL = 8
REL_BUCKETS = 32
REL_MAX_DIST = 128
EPS = 1e-6
NEG = -1e30
LOG2E = math.log2(math.e)
NEG_LOG2 = NEG * LOG2E
FAR_MASKED = 2 * NEG_LOG2
REMOVED = -3e38
K_SCALE = (B_HEAD_DIM ** -0.5) * LOG2E
BF16_SUBLANES = 16

F32 = jnp.float32
BF16 = jnp.bfloat16

MIB = 1024 * 1024
NT_DIMS = (((1,), (1,)), ((), ()))

V7X_VMEM_MIB = 64
NORM_ROWS = 512
PROJ_ROWS = 1024
PROJ_COLS = 1024
MIX_ROWS = 512
FFN_ROWS = 1024
FFN_COLS = 512


def _params(semantics, window_bytes):
    limit = min(window_bytes + V7X_VMEM_MIB * MIB // 4, V7X_VMEM_MIB * MIB * 15 // 16)
    return pltpu.CompilerParams(dimension_semantics=semantics, vmem_limit_bytes=limit)


def _nbytes(shape, dtype, buffers=1):
    return math.prod(shape) * jnp.dtype(dtype).itemsize * buffers


def _resident(shape, index_map):
    return pl.BlockSpec(shape, index_map, pipeline_mode=pl.Buffered(1))


def _rms(xf, gain):
    return xf * lax.rsqrt(jnp.mean(xf * xf, axis=-1, keepdims=True) + EPS) * gain


def _norm_kernel(x_ref, g_ref, h_ref):
    h_ref[...] = _rms(x_ref[...], g_ref[...]).astype(h_ref.dtype)


def _norm(x, gain):
    s, d = x.shape
    tm = NORM_ROWS
    windows = _nbytes((tm, d), F32, 2) + _nbytes((tm, d), BF16, 2)
    return pl.pallas_call(
        _norm_kernel,
        grid=(s // tm,),
        in_specs=[pl.BlockSpec((tm, d), lambda i: (i, 0)),
                  pl.BlockSpec((1, d), lambda i: (0, 0))],
        out_specs=pl.BlockSpec((tm, d), lambda i: (i, 0)),
        out_shape=jax.ShapeDtypeStruct((s, d), BF16),
        compiler_params=_params(("arbitrary",), windows),
        name="rms_norm",
    )(x, gain.reshape(1, d))


def _gelu(a):
    return 0.5 * a * (1.0 + lax.erf(a * math.sqrt(0.5)))


_EPILOGUES = {"gelu": _gelu, "sigmoid": jax.nn.sigmoid}


def _proj_kernel(h_ref, w_ref, o_ref, *, epilogue):
    acc = jnp.dot(h_ref[...], w_ref[...].astype(BF16), preferred_element_type=F32)
    o_ref[...] = _EPILOGUES[epilogue](acc).astype(o_ref.dtype)


def _proj_scale_kernel(h_ref, w_ref, scale_ref, o_ref):
    acc = jnp.dot(h_ref[...], w_ref[...].astype(BF16), preferred_element_type=F32)
    o_ref[...] = (acc * scale_ref[...]).astype(o_ref.dtype)


def _in_proj(h, w, col0, ncols, epilogue=None, col_scale=None):
    s, d = h.shape
    tm, tn = PROJ_ROWS, PROJ_COLS
    jb = col0 // tn
    in_specs = [pl.BlockSpec((tm, d), lambda i, j: (i, 0)),
                pl.BlockSpec((d, tn), lambda i, j: (0, j + jb))]
    if col_scale is None:
        body, args, name = functools.partial(_proj_kernel, epilogue=epilogue), (h, w), epilogue
    else:
        in_specs.append(pl.BlockSpec((1, tn), lambda i, j: (0, j)))
        body, args, name = _proj_scale_kernel, (h, w, col_scale.reshape(1, ncols)), "scale"
    windows = _nbytes((tm, d), BF16, 2) + _nbytes((d, tn), F32, 2) + _nbytes((tm, tn), BF16, 2)
    return pl.pallas_call(
        body,
        grid=(s // tm, ncols // tn),
        in_specs=in_specs,
        out_specs=pl.BlockSpec((tm, tn), lambda i, j: (i, j)),
        out_shape=jax.ShapeDtypeStruct((s, ncols), BF16),
        compiler_params=_params(("arbitrary", "arbitrary"), windows),
        name="in_proj_" + name,
    )(*args)


def _gmlp_kernel(u_ref, v_ref, ga_ref, gain_ref, ws_ref, bt_ref, wpa_ref, o_ref, wm_ref, a_ref):
    tq = u_ref.shape[0]

    @pl.when(pl.program_id(0) == 0)
    def _():
        t = lax.broadcasted_iota(jnp.int32, (A_CHUNK, A_CHUNK), 0)
        s = lax.broadcasted_iota(jnp.int32, (A_CHUNK, A_CHUNK), 1)
        for g in range(A_GROUPS):
            wm_ref[g] = jnp.where(t >= s, ws_ref[g], 0.0).astype(BF16)

    def chunk(c, carry):
        r0 = pl.multiple_of(c * A_CHUNK, A_CHUNK)
        v = v_ref[pl.ds(r0, A_CHUNK), :].astype(F32)
        mu = jnp.mean(v, axis=-1, keepdims=True)
        d = v - mu
        var = jnp.mean(d * d, axis=-1, keepdims=True)
        vn = (d * lax.rsqrt(var + EPS) * gain_ref[...]).astype(BF16)
        for g in range(A_GROUPS):
            cols = slice(g * A_GROUP_DIM, (g + 1) * A_GROUP_DIM)
            sv = jnp.dot(wm_ref[g], vn[:, cols], preferred_element_type=F32) + bt_ref[:, g:g + 1]
            u = u_ref[pl.ds(r0, A_CHUNK), cols].astype(F32)
            a_ref[pl.ds(r0, A_CHUNK), cols] = (u * sv).astype(BF16)
        return carry

    lax.fori_loop(0, tq // A_CHUNK, chunk, 0)
    y = jnp.dot(a_ref[...], wpa_ref[...].astype(BF16), preferred_element_type=F32)
    o_ref[...] = ga_ref[...].astype(F32) * y


def _gmlp(zuv, zg, gain, w_s, b_s, w_proj_a):
    s = zuv.shape[0]
    tq = MIX_ROWS
    aw = A_GROUPS * A_GROUP_DIM
    d = w_proj_a.shape[1]
    windows = (_nbytes((tq, aw), BF16, 4) + _nbytes((tq, d), BF16, 2) + _nbytes((tq, d), F32, 2)
               + _nbytes((aw, d), F32) + _nbytes((A_GROUPS, A_CHUNK, A_CHUNK), F32)
               + _nbytes((A_GROUPS, A_CHUNK, A_CHUNK), BF16) + _nbytes((tq, aw), BF16))
    return pl.pallas_call(
        _gmlp_kernel,
        grid=(s // tq,),
        in_specs=[pl.BlockSpec((tq, aw), lambda i: (i, 0)),
                  pl.BlockSpec((tq, aw), lambda i: (i, 1)),
                  pl.BlockSpec((tq, d), lambda i: (i, 0)),
                  _resident((1, aw), lambda i: (0, 0)),
                  _resident((A_GROUPS, A_CHUNK, A_CHUNK), lambda i: (0, 0, 0)),
                  _resident((A_CHUNK, A_GROUPS), lambda i: (0, 0)),
                  _resident((aw, d), lambda i: (0, 0))],
        out_specs=pl.BlockSpec((tq, d), lambda i: (i, 0)),
        out_shape=jax.ShapeDtypeStruct((s, d), F32),
        scratch_shapes=[pltpu.VMEM((A_GROUPS, A_CHUNK, A_CHUNK), BF16),
                        pltpu.VMEM((tq, aw), BF16)],
        compiler_params=_params(("arbitrary",), windows),
        name="gmlp_mixer",
    )(zuv, zuv, zg, gain.reshape(1, aw), w_s, b_s.T, w_proj_a)


def _t5_bias_log2(dist, rb_ref, base):
    max_exact = REL_BUCKETS // 2
    n = jnp.maximum(dist, 0)
    nf = jnp.maximum(n, 1).astype(F32)
    large = max_exact + (jnp.log(nf / max_exact) / math.log(REL_MAX_DIST / max_exact)
                         * (REL_BUCKETS - max_exact)).astype(jnp.int32)
    large = jnp.minimum(large, REL_BUCKETS - 1)
    bucket = jnp.where(n < max_exact, n, large)
    out = jnp.zeros(dist.shape, F32)
    for b in range(REL_BUCKETS):
        out = jnp.where(bucket == b, rb_ref[base + b] * LOG2E, out)
    return out


def _moba_kernel(rb_ref, q_ref, k_ref, v_ref, o_ref,
                 kmean_ref, vt_ref, tile_ref, rows_ref, m_ref, acc_ref, *slots):
    L = MOBA_BLOCK
    G = FAR_GROUP
    QB = Q_STEP_BLOCKS
    W = QB * L
    dh = B_HEAD_DIM
    nb = q_ref.shape[0] // L
    nq = nb // QB
    log2_l = L.bit_length() - 1
    base = pl.program_id(0) * REL_BUCKETS
    far_bias = rb_ref[base + REL_BUCKETS - 1] * LOG2E

    def prep(it, carry):
        for i in range(PREP_UNROLL):
            n = it * PREP_UNROLL + i
            r0 = pl.multiple_of(n * L, L)
            kmean_ref[pl.ds(n, 1), :] = jnp.mean(k_ref[pl.ds(r0, L), :].astype(F32), axis=0, keepdims=True)
            vt_ref[n, 0:dh, :] = v_ref[pl.ds(r0, L), :].astype(F32).T.astype(BF16)
            vt_ref[n, dh:, :] = jnp.ones((vt_ref.shape[1] - dh, L), BF16)
        return carry

    lax.fori_loop(0, nb // PREP_UNROLL, prep, 0)

    dist = lax.broadcasted_iota(jnp.int32, (1, 2 * L), 1) - L

    def toeplitz(row):
        return pltpu.roll(jnp.broadcast_to(row, (L, 2 * L)), L, 1, stride=1, stride_axis=0)[:, :L]

    b_prev = toeplitz(_t5_bias_log2(dist + L, rb_ref, base))
    b_own = toeplitz(jnp.where(dist >= 0, _t5_bias_log2(dist, rb_ref, base), NEG_LOG2))
    masked = jnp.full((L, L), NEG_LOG2, F32)
    zero = jnp.zeros((L, L), F32)
    kinds = [[[zero, zero], [zero, zero]], [[zero, zero], [b_prev, zero]], [[b_own, b_prev], [masked, b_own]]]
    for kind, blocks in enumerate(kinds):
        for r, tiles in enumerate(blocks):
            for c, tile in enumerate(tiles):
                tile_ref[kind, r * L:(r + 1) * L, c * L:(c + 1) * L] = tile

    kmean = kmean_ref[...].astype(BF16)

    def gate(it, carry):
        u0 = it * GATE_STEPS
        q_c = q_ref[pl.ds(pl.multiple_of(u0 * W, GATE_STEPS * W), GATE_STEPS * W), :]
        gs = lax.dot_general(kmean, q_c, NT_DIMS, preferred_element_type=F32) * (1.0 / K_SCALE)
        blk = lax.broadcasted_iota(jnp.int32, gs.shape, 0)
        own = QB * u0 + lax.shift_right_logical(lax.broadcasted_iota(jnp.int32, gs.shape, 1), log2_l)
        gs = jnp.where(blk < own, gs, NEG)
        sel = jnp.zeros(gs.shape, F32)
        for _ in range(MOBA_TOPK):
            top = jnp.max(gs, axis=0, keepdims=True)
            idx = jnp.min(jnp.where(gs == top, blk, nb), axis=0, keepdims=True)
            hit = blk == idx
            sel = jnp.where(hit, jnp.where(top > NEG / 2, 1.0, 0.0), sel)
            gs = jnp.where(hit, REMOVED, gs)
        picked = sel > 0.5
        rows = jnp.where(blk < own - 1, jnp.where(picked, far_bias, FAR_MASKED),
                         jnp.where(blk == own - 1, jnp.where(picked, 0.0, FAR_MASKED), 0.0))
        for i in range(GATE_STEPS):
            rows_ref[u0 + i] = rows[:, i * W:(i + 1) * W]
        return carry

    lax.fori_loop(0, nq // GATE_STEPS, gate, 0)

    def item(t, ua):
        first = t <= ua
        u = jnp.where(first, ua, nq - 1 - ua)
        g = jnp.where(first, t, t - ua - 1)
        return u, g, jnp.where(first, 0, 1), jnp.clip(g - (u - 2), 0, 2)

    def rows_of(u, g):
        return [rows_ref[u, pl.ds(g * G + b, 1), :] for b in range(G)]

    def produce(n, t, ua):
        u, g, _, kind = item(t, ua)
        q_u = q_ref[pl.ds(pl.multiple_of(u * W, W), W), :]
        k_g = k_ref[pl.ds(pl.multiple_of(g * (G * L), G * L), G * L), :]
        s_t = lax.dot_general(k_g, q_u, NT_DIMS, preferred_element_type=F32) + tile_ref[kind]
        slots[n % len(slots)][...] = s_t
        rows = rows_of(u, g)
        return functools.reduce(jnp.maximum, [
            jnp.max(s_t[b * L:(b + 1) * L], axis=0, keepdims=True) + rows[b] for b in range(G)])

    def consume(n, t, ua, pi, cand):
        u, g, w, _ = item(t, ua)
        rows = rows_of(u, g)
        s_ref = slots[n % len(slots)]
        st = pi * 2 + w
        m_old = m_ref[st]
        m_new = jnp.maximum(m_old, cand)
        pv = None
        for b in range(G):
            p = jnp.exp2(s_ref[b * L:(b + 1) * L, :] + (rows[b] - m_new)).astype(BF16)
            d = jnp.dot(vt_ref[g * G + b], p, preferred_element_type=F32)
            pv = d if pv is None else pv + d
        acc_ref[st] = jnp.exp2(m_old - m_new) * acc_ref[st] + pv
        m_ref[st] = m_new

    def finish(ua, pi):
        for w, u in ((0, ua), (1, nq - 1 - ua)):
            acc = acc_ref[pi * 2 + w]
            o_ref[pl.ds(pl.multiple_of(u * W, W), W), :] = (acc[:dh] / acc[dh:dh + 1]).T.astype(o_ref.dtype)

    def stream(it, carry):
        m_ref[...] = jnp.full(m_ref.shape, NEG_LOG2, F32)
        acc_ref[...] = jnp.zeros(acc_ref.shape, F32)
        seq = [(pi, t) for pi in range(PAIRS_PER_ITER) for t in range(nq + 1)]
        first_step = lambda pi: it * PAIRS_PER_ITER + pi
        cand = produce(0, seq[0][1], first_step(seq[0][0]))
        for n, (pi, t) in enumerate(seq):
            if n + 1 < len(seq):
                cand_next = produce(n + 1, seq[n + 1][1], first_step(seq[n + 1][0]))
            consume(n, t, first_step(pi), pi, cand)
            cand = cand_next
            if t == nq:
                finish(first_step(pi), pi)
        return carry

    lax.fori_loop(0, nq // 2 // PAIRS_PER_ITER, stream, 0)


def _moba(zqkv, rel_bias):
    s = zqkv.shape[0]
    dh = B_HEAD_DIM
    L = MOBA_BLOCK
    nb = s // L
    assert Q_STEP_BLOCKS == 2 and FAR_GROUP == 2 and nb % (2 * Q_STEP_BLOCKS) == 0
    assert nb % PREP_UNROLL == 0 and (nb // Q_STEP_BLOCKS) % GATE_STEPS == 0
    assert (nb // Q_STEP_BLOCKS // 2) % PAIRS_PER_ITER == 0
    w = Q_STEP_BLOCKS * L
    rb = rel_bias.T.reshape(-1).astype(F32)
    scratch = [pltpu.VMEM((nb, dh), F32),
               pltpu.VMEM((nb, dh + BF16_SUBLANES, L), BF16),
               pltpu.VMEM((3, FAR_GROUP * L, w), F32),
               pltpu.VMEM((nb // Q_STEP_BLOCKS, nb, w), F32),
               pltpu.VMEM((2 * PAIRS_PER_ITER, 1, w), F32),
               pltpu.VMEM((2 * PAIRS_PER_ITER, dh + BF16_SUBLANES, w), F32)]
    scratch += [pltpu.VMEM((FAR_GROUP * L, w), F32)] * SCORE_SLOTS
    windows = _nbytes((s, dh), BF16, 8) + sum(_nbytes(b.shape, b.dtype) for b in scratch)
    return pl.pallas_call(
        _moba_kernel,
        grid=(B_HEADS,),
        in_specs=[pl.BlockSpec(memory_space=pltpu.SMEM),
                  pl.BlockSpec((s, dh), lambda h: (0, h)),
                  pl.BlockSpec((s, dh), lambda h: (0, B_HEADS + h)),
                  pl.BlockSpec((s, dh), lambda h: (0, 2 * B_HEADS + h))],
        out_specs=pl.BlockSpec((s, dh), lambda h: (0, h)),
        out_shape=jax.ShapeDtypeStruct((s, B_HEADS * dh), BF16),
        scratch_shapes=scratch,
        compiler_params=_params(("arbitrary",), windows),
        name="moba_attention",
    )(rb, zqkv, zqkv, zqkv)


def _merge_kernel(attn_ref, gb_ref, gaya_ref, x_ref, wpb_ref, wout_ref, g_ref, x1_ref, h2_ref):
    y_b = jnp.dot(attn_ref[...], wpb_ref[...], preferred_element_type=F32)
    merged = gaya_ref[...] + gb_ref[...].astype(F32) * y_b
    x1 = x_ref[...] + jnp.dot(merged.astype(BF16), wout_ref[...], preferred_element_type=F32)
    x1_ref[...] = x1
    h2_ref[...] = _rms(x1, g_ref[...]).astype(h2_ref.dtype)


def _merge(attn, zg, gaya, x, w_proj_b, w_out, gain):
    s, d = x.shape
    tq = MIX_ROWS
    bw = attn.shape[1]
    windows = (_nbytes((tq, bw), BF16, 2) + _nbytes((tq, d), BF16, 2) + _nbytes((tq, d), F32, 4)
               + _nbytes((bw, d), BF16) + _nbytes((d, d), BF16)
               + _nbytes((tq, d), F32, 2) + _nbytes((tq, d), BF16, 2))
    return pl.pallas_call(
        _merge_kernel,
        grid=(s // tq,),
        in_specs=[pl.BlockSpec((tq, bw), lambda i: (i, 0)),
                  pl.BlockSpec((tq, d), lambda i: (i, 1)),
                  pl.BlockSpec((tq, d), lambda i: (i, 0)),
                  pl.BlockSpec((tq, d), lambda i: (i, 0)),
                  _resident((bw, d), lambda i: (0, 0)),
                  _resident((d, d), lambda i: (0, 0)),
                  _resident((1, d), lambda i: (0, 0))],
        out_specs=[pl.BlockSpec((tq, d), lambda i: (i, 0)),
                   pl.BlockSpec((tq, d), lambda i: (i, 0))],
        out_shape=[jax.ShapeDtypeStruct((s, d), F32),
                   jax.ShapeDtypeStruct((s, d), BF16)],
        compiler_params=_params(("arbitrary",), windows),
        name="merge_out_proj",
    )(attn, zg, gaya, x, w_proj_b, w_out, gain.reshape(1, d))


def _ffn_kernel(h_ref, wu_ref, wd_ref, x1_ref, g_ref, o_ref, *, final_norm):
    j = pl.program_id(1)

    @pl.when(j == 0)
    def _():
        o_ref[...] = x1_ref[...]

    a = jnp.dot(h_ref[...], wu_ref[...].astype(BF16), preferred_element_type=F32)
    a = jnp.square(jnp.maximum(a, 0.0)).astype(BF16)
    o_ref[...] += jnp.dot(a, wd_ref[...].astype(BF16), preferred_element_type=F32)

    if final_norm:
        @pl.when(j == pl.num_programs(1) - 1)
        def _():
            o_ref[...] = _rms(o_ref[...], g_ref[...])


def _ffn(h2, w_up, w_down, x1, gain, final_norm):
    s, d = x1.shape
    tm, tf = FFN_ROWS, FFN_COLS
    dff = w_up.shape[1]
    windows = _nbytes((tm, d), BF16, 2) + _nbytes((d, tf), F32, 4) + _nbytes((tm, d), F32, 4)
    return pl.pallas_call(
        functools.partial(_ffn_kernel, final_norm=final_norm),
        grid=(s // tm, dff // tf),
        in_specs=[pl.BlockSpec((tm, d), lambda i, j: (i, 0)),
                  pl.BlockSpec((d, tf), lambda i, j: (0, j)),
                  pl.BlockSpec((tf, d), lambda i, j: (j, 0)),
                  pl.BlockSpec((tm, d), lambda i, j: (i, 0)),
                  _resident((1, d), lambda i, j: (0, 0))],
        out_specs=pl.BlockSpec((tm, d), lambda i, j: (i, 0)),
        out_shape=jax.ShapeDtypeStruct((s, d), F32),
        compiler_params=_params(("arbitrary", "arbitrary"), windows),
        name="ffn_relu2",
    )(h2, w_up, w_down, x1, gain.reshape(1, d))


def kernel(x, ln_mix, w_in, a_v_gain, a_spatial, a_spatial_bias, w_proj_a, w_proj_b, w_out,
           rel_bias, ln_mlp, w_up, w_down, ln_final):
    bsz, s, d = x.shape
    depth = ln_mix.shape[0]
    aw = A_GROUPS * A_GROUP_DIM
    bw = B_HEADS * B_HEAD_DIM
    assert w_in.shape[2] == 2 * aw + 3 * bw + 2 * d
    qkv_scale = jnp.concatenate([jnp.ones((bw,), F32), jnp.full((bw,), K_SCALE, F32), jnp.ones((bw,), F32)])
    outs = []
    for b in range(bsz):
        xb = x[b]
        for l in range(depth):
            last = l == depth - 1
            h = _norm(xb, ln_mix[l])
            zuv = _in_proj(h, w_in[l], 0, 2 * aw, "gelu")
            zqkv = _in_proj(h, w_in[l], 2 * aw, 3 * bw, col_scale=qkv_scale)
            zg = _in_proj(h, w_in[l], 2 * aw + 3 * bw, 2 * d, "sigmoid")
            gaya = _gmlp(zuv, zg, a_v_gain[l], a_spatial[l], a_spatial_bias[l], w_proj_a[l])
            attn = _moba(zqkv, rel_bias)
            x1, h2 = _merge(attn, zg, gaya, xb, w_proj_b[l].astype(BF16), w_out[l].astype(BF16), ln_mlp[l])
            gain = ln_final if last else jnp.ones((d,), F32)
            xb = _ffn(h2, w_up[l], w_down[l], x1, gain, final_norm=last)
        outs.append(xb)
    return jnp.stack(outs)
```

```python
import functools
import math

import jax
import jax.numpy as jnp
from jax import lax
from jax.experimental import pallas as pl
from jax.experimental.pallas import tpu as pltpu

A_GROUPS = 16
A_GROUP_DIM = 128
A_CHUNK = 128
B_HEADS = 16
B_HEAD_DIM = 128
MOBA_BLOCK = 256
MOBA_TOPK = 3
FAR_GROUP = 2
Q_STEP_BLOCKS = 2
PAIRS_PER_ITER = 2
SCORE_SLOTS = 2
GATE_STEPS = 8
PREP_UNROLL = 8
REL_BUCKETS = 32
REL_MAX_DIST = 128
EPS = 1e-6
NEG = -1e30
LOG2E = math.log2(math.e)
NEG_LOG2 = NEG * LOG2E
FAR_MASKED = 2 * NEG_LOG2
REMOVED = -3e38
K_SCALE = (B_HEAD_DIM ** -0.5) * LOG2E
BF16_SUBLANES = 16

F32 = jnp.float32
BF16 = jnp.bfloat16

MIB = 1024 * 1024
NT_DIMS = (((1,), (1,)), ((), ()))

V7X_VMEM_MIB = 64
PROJ_ROWS = 1024
PROJ_COLS = 1024
MIX_ROWS = 512
FFN_ROWS = 1024
FFN_COLS = 512


def _params(semantics, window_bytes):
    limit = min(window_bytes + V7X_VMEM_MIB * MIB // 4, V7X_VMEM_MIB * MIB * 15 // 16)
    return pltpu.CompilerParams(dimension_semantics=semantics, vmem_limit_bytes=limit)


def _nbytes(shape, dtype, buffers=1):
    return math.prod(shape) * jnp.dtype(dtype).itemsize * buffers


def _resident(shape, index_map):
    return pl.BlockSpec(shape, index_map, pipeline_mode=pl.Buffered(1))


def _rms(xf, gain):
    return xf * lax.rsqrt(jnp.mean(xf * xf, axis=-1, keepdims=True) + EPS) * gain


def _gelu(a):
    return 0.5 * a * (1.0 + lax.erf(a * math.sqrt(0.5)))


_EPILOGUES = {"gelu": _gelu, "sigmoid": jax.nn.sigmoid}


def _proj_kernel(h_ref, w_ref, o_ref, *, epilogue):
    acc = jnp.dot(h_ref[...], w_ref[...].astype(BF16), preferred_element_type=F32)
    o_ref[...] = _EPILOGUES[epilogue](acc).astype(o_ref.dtype)


def _proj_scale_kernel(h_ref, w_ref, scale_ref, o_ref):
    acc = jnp.dot(h_ref[...], w_ref[...].astype(BF16), preferred_element_type=F32)
    o_ref[...] = (acc * scale_ref[...]).astype(o_ref.dtype)


def _norm_proj_kernel(x_ref, g_ref, w_ref, o_ref, h_ref, *, epilogue):
    @pl.when(pl.program_id(1) == 0)
    def _():
        h_ref[...] = _rms(x_ref[...], g_ref[...]).astype(h_ref.dtype)

    acc = jnp.dot(h_ref[...], w_ref[...].astype(BF16), preferred_element_type=F32)
    o_ref[...] = _EPILOGUES[epilogue](acc).astype(o_ref.dtype)


def _norm_in_proj(x, gain, w, ncols, epilogue):
    s, d = x.shape
    tm, tn = PROJ_ROWS, PROJ_COLS
    windows = (_nbytes((tm, d), F32, 2) + _nbytes((d, tn), F32, 2) + _nbytes((tm, tn), BF16, 2)
               + _nbytes((tm, d), BF16, 2))
    return pl.pallas_call(
        functools.partial(_norm_proj_kernel, epilogue=epilogue),
        grid=(s // tm, ncols // tn),
        in_specs=[pl.BlockSpec((tm, d), lambda i, j: (i, 0)),
                  pl.BlockSpec((1, d), lambda i, j: (0, 0)),
                  pl.BlockSpec((d, tn), lambda i, j: (0, j))],
        out_specs=[pl.BlockSpec((tm, tn), lambda i, j: (i, j)),
                   pl.BlockSpec((tm, d), lambda i, j: (i, 0))],
        out_shape=[jax.ShapeDtypeStruct((s, ncols), BF16),
                   jax.ShapeDtypeStruct((s, d), BF16)],
        compiler_params=_params(("arbitrary", "arbitrary"), windows),
        name="norm_in_proj_" + epilogue,
    )(x, gain.reshape(1, d), w)


def _in_proj(h, w, col0, ncols, epilogue=None, col_scale=None):
    s, d = h.shape
    tm, tn = PROJ_ROWS, PROJ_COLS
    jb = col0 // tn
    in_specs = [pl.BlockSpec((tm, d), lambda i, j: (i, 0)),
                pl.BlockSpec((d, tn), lambda i, j: (0, j + jb))]
    if col_scale is None:
        body, args, name = functools.partial(_proj_kernel, epilogue=epilogue), (h, w), epilogue
    else:
        in_specs.append(pl.BlockSpec((1, tn), lambda i, j: (0, j)))
        body, args, name = _proj_scale_kernel, (h, w, col_scale.reshape(1, ncols)), "scale"
    windows = _nbytes((tm, d), BF16, 2) + _nbytes((d, tn), F32, 2) + _nbytes((tm, tn), BF16, 2)
    return pl.pallas_call(
        body,
        grid=(s // tm, ncols // tn),
        in_specs=in_specs,
        out_specs=pl.BlockSpec((tm, tn), lambda i, j: (i, j)),
        out_shape=jax.ShapeDtypeStruct((s, ncols), BF16),
        compiler_params=_params(("arbitrary", "arbitrary"), windows),
        name="in_proj_" + name,
    )(*args)


def _gmlp_kernel(u_ref, v_ref, ga_ref, gain_ref, ws_ref, bt_ref, wpa_ref, o_ref, wm_ref, a_ref):
    tq = u_ref.shape[0]

    @pl.when(pl.program_id(0) == 0)
    def _():
        t = lax.broadcasted_iota(jnp.int32, (A_CHUNK, A_CHUNK), 0)
        s = lax.broadcasted_iota(jnp.int32, (A_CHUNK, A_CHUNK), 1)
        for g in range(A_GROUPS):
            wm_ref[g] = jnp.where(t >= s, ws_ref[g], 0.0).astype(BF16)

    def chunk(c, carry):
        r0 = pl.multiple_of(c * A_CHUNK, A_CHUNK)
        v = v_ref[pl.ds(r0, A_CHUNK), :].astype(F32)
        mu = jnp.mean(v, axis=-1, keepdims=True)
        d = v - mu
        var = jnp.mean(d * d, axis=-1, keepdims=True)
        vn = (d * lax.rsqrt(var + EPS) * gain_ref[...]).astype(BF16)
        for g in range(A_GROUPS):
            cols = slice(g * A_GROUP_DIM, (g + 1) * A_GROUP_DIM)
            sv = jnp.dot(wm_ref[g], vn[:, cols], preferred_element_type=F32) + bt_ref[:, g:g + 1]
            u = u_ref[pl.ds(r0, A_CHUNK), cols].astype(F32)
            a_ref[pl.ds(r0, A_CHUNK), cols] = (u * sv).astype(BF16)
        return carry

    lax.fori_loop(0, tq // A_CHUNK, chunk, 0)
    y = jnp.dot(a_ref[...], wpa_ref[...].astype(BF16), preferred_element_type=F32)
    o_ref[...] = ga_ref[...].astype(F32) * y


def _gmlp(zuv, zg, gain, w_s, b_s, w_proj_a):
    s = zuv.shape[0]
    tq = MIX_ROWS
    aw = A_GROUPS * A_GROUP_DIM
    d = w_proj_a.shape[1]
    windows = (_nbytes((tq, aw), BF16, 4) + _nbytes((tq, d), BF16, 2) + _nbytes((tq, d), F32, 2)
               + _nbytes((aw, d), F32) + _nbytes((A_GROUPS, A_CHUNK, A_CHUNK), F32)
               + _nbytes((A_GROUPS, A_CHUNK, A_CHUNK), BF16) + _nbytes((tq, aw), BF16))
    return pl.pallas_call(
        _gmlp_kernel,
        grid=(s // tq,),
        in_specs=[pl.BlockSpec((tq, aw), lambda i: (i, 0)),
                  pl.BlockSpec((tq, aw), lambda i: (i, 1)),
                  pl.BlockSpec((tq, d), lambda i: (i, 0)),
                  _resident((1, aw), lambda i: (0, 0)),
                  _resident((A_GROUPS, A_CHUNK, A_CHUNK), lambda i: (0, 0, 0)),
                  _resident((A_CHUNK, A_GROUPS), lambda i: (0, 0)),
                  _resident((aw, d), lambda i: (0, 0))],
        out_specs=pl.BlockSpec((tq, d), lambda i: (i, 0)),
        out_shape=jax.ShapeDtypeStruct((s, d), F32),
        scratch_shapes=[pltpu.VMEM((A_GROUPS, A_CHUNK, A_CHUNK), BF16),
                        pltpu.VMEM((tq, aw), BF16)],
        compiler_params=_params(("arbitrary",), windows),
        name="gmlp_mixer",
    )(zuv, zuv, zg, gain.reshape(1, aw), w_s, b_s.T, w_proj_a)


def _t5_bias_log2(dist, rb_ref, base):
    max_exact = REL_BUCKETS // 2
    n = jnp.maximum(dist, 0)
    nf = jnp.maximum(n, 1).astype(F32)
    large = max_exact + (jnp.log(nf / max_exact) / math.log(REL_MAX_DIST / max_exact)
                         * (REL_BUCKETS - max_exact)).astype(jnp.int32)
    large = jnp.minimum(large, REL_BUCKETS - 1)
    bucket = jnp.where(n < max_exact, n, large)
    out = jnp.zeros(dist.shape, F32)
    for b in range(REL_BUCKETS):
        out = jnp.where(bucket == b, rb_ref[base + b] * LOG2E, out)
    return out


def _moba_kernel(rb_ref, q_ref, k_ref, v_ref, o_ref,
                 kmean_ref, vt_ref, tile_ref, rows_ref, m_ref, acc_ref, *slots):
    L = MOBA_BLOCK
    G = FAR_GROUP
    QB = Q_STEP_BLOCKS
    W = QB * L
    dh = B_HEAD_DIM
    nb = q_ref.shape[0] // L
    nq = nb // QB
    log2_l = L.bit_length() - 1
    base = pl.program_id(0) * REL_BUCKETS
    far_bias = rb_ref[base + REL_BUCKETS - 1] * LOG2E

    def prep(it, carry):
        for i in range(PREP_UNROLL):
            n = it * PREP_UNROLL + i
            r0 = pl.multiple_of(n * L, L)
            kmean_ref[pl.ds(n, 1), :] = jnp.mean(k_ref[pl.ds(r0, L), :].astype(F32), axis=0, keepdims=True)
            vt_ref[n, 0:dh, :] = v_ref[pl.ds(r0, L), :].astype(F32).T.astype(BF16)
            vt_ref[n, dh:, :] = jnp.ones((vt_ref.shape[1] - dh, L), BF16)
        return carry

    lax.fori_loop(0, nb // PREP_UNROLL, prep, 0)

    dist = lax.broadcasted_iota(jnp.int32, (1, 2 * L), 1) - L

    def toeplitz(row):
        return pltpu.roll(jnp.broadcast_to(row, (L, 2 * L)), L, 1, stride=1, stride_axis=0)[:, :L]

    b_prev = toeplitz(_t5_bias_log2(dist + L, rb_ref, base))
    b_own = toeplitz(jnp.where(dist >= 0, _t5_bias_log2(dist, rb_ref, base), NEG_LOG2))
    masked = jnp.full((L, L), NEG_LOG2, F32)
    zero = jnp.zeros((L, L), F32)
    kinds = [[[zero, zero], [zero, zero]], [[zero, zero], [b_prev, zero]], [[b_own, b_prev], [masked, b_own]]]
    for kind, blocks in enumerate(kinds):
        for r, tiles in enumerate(blocks):
            for c, tile in enumerate(tiles):
                tile_ref[kind, r * L:(r + 1) * L, c * L:(c + 1) * L] = tile

    kmean = kmean_ref[...].astype(BF16)

    def gate(it, carry):
        u0 = it * GATE_STEPS
        q_c = q_ref[pl.ds(pl.multiple_of(u0 * W, GATE_STEPS * W), GATE_STEPS * W), :]
        gs = lax.dot_general(kmean, q_c, NT_DIMS, preferred_element_type=F32) * (1.0 / K_SCALE)
        blk = lax.broadcasted_iota(jnp.int32, gs.shape, 0)
        own = QB * u0 + lax.shift_right_logical(lax.broadcasted_iota(jnp.int32, gs.shape, 1), log2_l)
        gs = jnp.where(blk < own, gs, NEG)
        sel = jnp.zeros(gs.shape, F32)
        for _ in range(MOBA_TOPK):
            top = jnp.max(gs, axis=0, keepdims=True)
            idx = jnp.min(jnp.where(gs == top, blk, nb), axis=0, keepdims=True)
            hit = blk == idx
            sel = jnp.where(hit, jnp.where(top > NEG / 2, 1.0, 0.0), sel)
            gs = jnp.where(hit, REMOVED, gs)
        picked = sel > 0.5
        rows = jnp.where(blk < own - 1, jnp.where(picked, far_bias, FAR_MASKED),
                         jnp.where(blk == own - 1, jnp.where(picked, 0.0, FAR_MASKED), 0.0))
        for i in range(GATE_STEPS):
            rows_ref[u0 + i] = rows[:, i * W:(i + 1) * W]
        return carry

    lax.fori_loop(0, nq // GATE_STEPS, gate, 0)

    def item(t, ua):
        first = t <= ua
        u = jnp.where(first, ua, nq - 1 - ua)
        g = jnp.where(first, t, t - ua - 1)
        return u, g, jnp.where(first, 0, 1), jnp.clip(g - (u - 2), 0, 2)

    def rows_of(u, g):
        return [rows_ref[u, pl.ds(g * G + b, 1), :] for b in range(G)]

    def produce(n, t, ua):
        u, g, _, kind = item(t, ua)
        q_u = q_ref[pl.ds(pl.multiple_of(u * W, W), W), :]
        k_g = k_ref[pl.ds(pl.multiple_of(g * (G * L), G * L), G * L), :]
        s_t = lax.dot_general(k_g, q_u, NT_DIMS, preferred_element_type=F32) + tile_ref[kind]
        slots[n % len(slots)][...] = s_t
        rows = rows_of(u, g)
        return functools.reduce(jnp.maximum, [
            jnp.max(s_t[b * L:(b + 1) * L], axis=0, keepdims=True) + rows[b] for b in range(G)])

    def consume(n, t, ua, pi, cand):
        u, g, w, _ = item(t, ua)
        rows = rows_of(u, g)
        s_ref = slots[n % len(slots)]
        st = pi * 2 + w
        m_old = m_ref[st]
        m_new = jnp.maximum(m_old, cand)
        pv = None
        for b in range(G):
            p = jnp.exp2(s_ref[b * L:(b + 1) * L, :] + (rows[b] - m_new)).astype(BF16)
            d = jnp.dot(vt_ref[g * G + b], p, preferred_element_type=F32)
            pv = d if pv is None else pv + d
        acc_ref[st] = jnp.exp2(m_old - m_new) * acc_ref[st] + pv
        m_ref[st] = m_new

    def finish(ua, pi):
        for w, u in ((0, ua), (1, nq - 1 - ua)):
            acc = acc_ref[pi * 2 + w]
            o_ref[pl.ds(pl.multiple_of(u * W, W), W), :] = (acc[:dh] / acc[dh:dh + 1]).T.astype(o_ref.dtype)

    def stream(it, carry):
        m_ref[...] = jnp.full(m_ref.shape, NEG_LOG2, F32)
        acc_ref[...] = jnp.zeros(acc_ref.shape, F32)
        seq = [(pi, t) for pi in range(PAIRS_PER_ITER) for t in range(nq + 1)]
        first_step = lambda pi: it * PAIRS_PER_ITER + pi
        cand = produce(0, seq[0][1], first_step(seq[0][0]))
        for n, (pi, t) in enumerate(seq):
            if n + 1 < len(seq):
                cand_next = produce(n + 1, seq[n + 1][1], first_step(seq[n + 1][0]))
            consume(n, t, first_step(pi), pi, cand)
            cand = cand_next
            if t == nq:
                finish(first_step(pi), pi)
        return carry

    lax.fori_loop(0, nq // 2 // PAIRS_PER_ITER, stream, 0)


def _moba(zqkv, rel_bias):
    s = zqkv.shape[0]
    dh = B_HEAD_DIM
    L = MOBA_BLOCK
    nb = s // L
    assert Q_STEP_BLOCKS == 2 and FAR_GROUP == 2 and nb % (2 * Q_STEP_BLOCKS) == 0
    assert nb % PREP_UNROLL == 0 and (nb // Q_STEP_BLOCKS) % GATE_STEPS == 0
    assert (nb // Q_STEP_BLOCKS // 2) % PAIRS_PER_ITER == 0
    w = Q_STEP_BLOCKS * L
    rb = rel_bias.T.reshape(-1).astype(F32)
    scratch = [pltpu.VMEM((nb, dh), F32),
               pltpu.VMEM((nb, dh + BF16_SUBLANES, L), BF16),
               pltpu.VMEM((3, FAR_GROUP * L, w), F32),
               pltpu.VMEM((nb // Q_STEP_BLOCKS, nb, w), F32),
               pltpu.VMEM((2 * PAIRS_PER_ITER, 1, w), F32),
               pltpu.VMEM((2 * PAIRS_PER_ITER, dh + BF16_SUBLANES, w), F32)]
    scratch += [pltpu.VMEM((FAR_GROUP * L, w), F32)] * SCORE_SLOTS
    windows = _nbytes((s, dh), BF16, 8) + sum(_nbytes(b.shape, b.dtype) for b in scratch)
    return pl.pallas_call(
        _moba_kernel,
        grid=(B_HEADS,),
        in_specs=[pl.BlockSpec(memory_space=pltpu.SMEM),
                  pl.BlockSpec((s, dh), lambda h: (0, h)),
                  pl.BlockSpec((s, dh), lambda h: (0, B_HEADS + h)),
                  pl.BlockSpec((s, dh), lambda h: (0, 2 * B_HEADS + h))],
        out_specs=pl.BlockSpec((s, dh), lambda h: (0, h)),
        out_shape=jax.ShapeDtypeStruct((s, B_HEADS * dh), BF16),
        scratch_shapes=scratch,
        compiler_params=_params(("arbitrary",), windows),
        name="moba_attention",
    )(rb, zqkv, zqkv, zqkv)


def _merge_kernel(attn_ref, gb_ref, gaya_ref, x_ref, wpb_ref, wout_ref, g_ref, x1_ref, h2_ref):
    y_b = jnp.dot(attn_ref[...], wpb_ref[...], preferred_element_type=F32)
    merged = gaya_ref[...] + gb_ref[...].astype(F32) * y_b
    x1 = x_ref[...] + jnp.dot(merged.astype(BF16), wout_ref[...], preferred_element_type=F32)
    x1_ref[...] = x1
    h2_ref[...] = _rms(x1, g_ref[...]).astype(h2_ref.dtype)


def _merge(attn, zg, gaya, x, w_proj_b, w_out, gain):
    s, d = x.shape
    tq = MIX_ROWS
    bw = attn.shape[1]
    windows = (_nbytes((tq, bw), BF16, 2) + _nbytes((tq, d), BF16, 2) + _nbytes((tq, d), F32, 4)
               + _nbytes((bw, d), BF16) + _nbytes((d, d), BF16)
               + _nbytes((tq, d), F32, 2) + _nbytes((tq, d), BF16, 2))
    return pl.pallas_call(
        _merge_kernel,
        grid=(s // tq,),
        in_specs=[pl.BlockSpec((tq, bw), lambda i: (i, 0)),
                  pl.BlockSpec((tq, d), lambda i: (i, 1)),
                  pl.BlockSpec((tq, d), lambda i: (i, 0)),
                  pl.BlockSpec((tq, d), lambda i: (i, 0)),
                  _resident((bw, d), lambda i: (0, 0)),
                  _resident((d, d), lambda i: (0, 0)),
                  _resident((1, d), lambda i: (0, 0))],
        out_specs=[pl.BlockSpec((tq, d), lambda i: (i, 0)),
                   pl.BlockSpec((tq, d), lambda i: (i, 0))],
        out_shape=[jax.ShapeDtypeStruct((s, d), F32),
                   jax.ShapeDtypeStruct((s, d), BF16)],
        compiler_params=_params(("arbitrary",), windows),
        name="merge_out_proj",
    )(attn, zg, gaya, x, w_proj_b, w_out, gain.reshape(1, d))


def _ffn_kernel(h_ref, wu_ref, wd_ref, x1_ref, g_ref, o_ref, *, final_norm):
    j = pl.program_id(1)

    @pl.when(j == 0)
    def _():
        o_ref[...] = x1_ref[...]

    a = jnp.dot(h_ref[...], wu_ref[...].astype(BF16), preferred_element_type=F32)
    a = jnp.square(jnp.maximum(a, 0.0)).astype(BF16)
    o_ref[...] += jnp.dot(a, wd_ref[...].astype(BF16), preferred_element_type=F32)

    if final_norm:
        @pl.when(j == pl.num_programs(1) - 1)
        def _():
            o_ref[...] = _rms(o_ref[...], g_ref[...])


def _ffn(h2, w_up, w_down, x1, gain, final_norm):
    s, d = x1.shape
    tm, tf = FFN_ROWS, FFN_COLS
    dff = w_up.shape[1]
    windows = _nbytes((tm, d), BF16, 2) + _nbytes((d, tf), F32, 4) + _nbytes((tm, d), F32, 4)
    return pl.pallas_call(
        functools.partial(_ffn_kernel, final_norm=final_norm),
        grid=(s // tm, dff // tf),
        in_specs=[pl.BlockSpec((tm, d), lambda i, j: (i, 0)),
                  pl.BlockSpec((d, tf), lambda i, j: (0, j)),
                  pl.BlockSpec((tf, d), lambda i, j: (j, 0)),
                  pl.BlockSpec((tm, d), lambda i, j: (i, 0)),
                  _resident((1, d), lambda i, j: (0, 0))],
        out_specs=pl.BlockSpec((tm, d), lambda i, j: (i, 0)),
        out_shape=jax.ShapeDtypeStruct((s, d), F32),
        compiler_params=_params(("arbitrary", "arbitrary"), windows),
        name="ffn_relu2",
    )(h2, w_up, w_down, x1, gain.reshape(1, d))


def kernel(x, ln_mix, w_in, a_v_gain, a_spatial, a_spatial_bias, w_proj_a, w_proj_b, w_out,
           rel_bias, ln_mlp, w_up, w_down, ln_final):
    bsz, s, d = x.shape
    depth = ln_mix.shape[0]
    aw = A_GROUPS * A_GROUP_DIM
    bw = B_HEADS * B_HEAD_DIM
    assert w_in.shape[2] == 2 * aw + 3 * bw + 2 * d
    qkv_scale = jnp.concatenate([jnp.ones((bw,), F32), jnp.full((bw,), K_SCALE, F32), jnp.ones((bw,), F32)])
    outs = []
    for b in range(bsz):
        xb = x[b]
        for l in range(depth):
            last = l == depth - 1
            zuv, h = _norm_in_proj(xb, ln_mix[l], w_in[l], 2 * aw, "gelu")
            zqkv = _in_proj(h, w_in[l], 2 * aw, 3 * bw, col_scale=qkv_scale)
            zg = _in_proj(h, w_in[l], 2 * aw + 3 * bw, 2 * d, "sigmoid")
            gaya = _gmlp(zuv, zg, a_v_gain[l], a_spatial[l], a_spatial_bias[l], w_proj_a[l])
            attn = _moba(zqkv, rel_bias)
            x1, h2 = _merge(attn, zg, gaya, xb, w_proj_b[l].astype(BF16), w_out[l].astype(BF16), ln_mlp[l])
            gain = ln_final if last else jnp.ones((d,), F32)
            xb = _ffn(h2, w_up[l], w_down[l], x1, gain, final_norm=last)
        outs.append(xb)
    return jnp.stack(outs)
```

```python
import functools
import math

import jax
import jax.numpy as jnp
from jax import lax
from jax.experimental import pallas as pl
from jax.experimental.pallas import tpu as pltpu

A_GROUPS = 16
A_GROUP_DIM = 128
A_CHUNK = 128
B_HEADS = 16
B_HEAD_DIM = 128
MOBA_BLOCK = 256
MOBA_TOPK = 3
FAR_GROUP = 2
Q_STEP_BLOCKS = 2
PAIRS_PER_ITER = 2
SCORE_SLOTS = 2
GATE_STEPS = 8
PREP_UNROLL = 8
REL_BUCKETS = 32
REL_MAX_DIST = 128
EPS = 1e-6
NEG = -1e30
LOG2E = math.log2(math.e)
NEG_LOG2 = NEG * LOG2E
FAR_MASKED = 2 * NEG_LOG2
REMOVED = -3e38
K_SCALE = (B_HEAD_DIM ** -0.5) * LOG2E
BF16_SUBLANES = 16

F32 = jnp.float32
BF16 = jnp.bfloat16

MIB = 1024 * 1024
NT_DIMS = (((1,), (1,)), ((), ()))

V7X_VMEM_MIB = 64
PROJ_ROWS = 1024
PROJ_COLS = 1024
MIX_ROWS = 512
GMLP_PARTS = 2
FFN_ROWS = 1024
FFN_COLS = 512


def _params(semantics, window_bytes):
    limit = min(window_bytes + V7X_VMEM_MIB * MIB // 4, V7X_VMEM_MIB * MIB * 15 // 16)
    return pltpu.CompilerParams(dimension_semantics=semantics, vmem_limit_bytes=limit)


def _nbytes(shape, dtype, buffers=1):
    return math.prod(shape) * jnp.dtype(dtype).itemsize * buffers


def _resident(shape, index_map):
    return pl.BlockSpec(shape, index_map, pipeline_mode=pl.Buffered(1))


def _rms(xf, gain):
    return xf * lax.rsqrt(jnp.mean(xf * xf, axis=-1, keepdims=True) + EPS) * gain


def _gelu(a):
    return 0.5 * a * (1.0 + lax.erf(a * math.sqrt(0.5)))


_EPILOGUES = {"gelu": _gelu, "sigmoid": jax.nn.sigmoid}


def _proj_kernel(h_ref, w_ref, o_ref, *, epilogue):
    acc = jnp.dot(h_ref[...], w_ref[...].astype(BF16), preferred_element_type=F32)
    o_ref[...] = _EPILOGUES[epilogue](acc).astype(o_ref.dtype)


def _proj_scale_kernel(h_ref, w_ref, scale_ref, o_ref):
    acc = jnp.dot(h_ref[...], w_ref[...].astype(BF16), preferred_element_type=F32)
    o_ref[...] = (acc * scale_ref[...]).astype(o_ref.dtype)


def _norm_proj_kernel(x_ref, g_ref, w_ref, o_ref, h_ref, *, epilogue):
    @pl.when(pl.program_id(1) == 0)
    def _():
        h_ref[...] = _rms(x_ref[...], g_ref[...]).astype(h_ref.dtype)

    acc = jnp.dot(h_ref[...], w_ref[...].astype(BF16), preferred_element_type=F32)
    o_ref[...] = _EPILOGUES[epilogue](acc).astype(o_ref.dtype)


def _norm_in_proj(x, gain, w, ncols, epilogue):
    s, d = x.shape
    tm, tn = PROJ_ROWS, PROJ_COLS
    windows = (_nbytes((tm, d), F32, 2) + _nbytes((d, tn), F32, 2) + _nbytes((tm, tn), BF16, 2)
               + _nbytes((tm, d), BF16, 2))
    return pl.pallas_call(
        functools.partial(_norm_proj_kernel, epilogue=epilogue),
        grid=(s // tm, ncols // tn),
        in_specs=[pl.BlockSpec((tm, d), lambda i, j: (i, 0)),
                  pl.BlockSpec((1, d), lambda i, j: (0, 0)),
                  pl.BlockSpec((d, tn), lambda i, j: (0, j))],
        out_specs=[pl.BlockSpec((tm, tn), lambda i, j: (i, j)),
                   pl.BlockSpec((tm, d), lambda i, j: (i, 0))],
        out_shape=[jax.ShapeDtypeStruct((s, ncols), BF16),
                   jax.ShapeDtypeStruct((s, d), BF16)],
        compiler_params=_params(("arbitrary", "arbitrary"), windows),
        name="norm_in_proj_" + epilogue,
    )(x, gain.reshape(1, d), w)


def _in_proj(h, w, col0, ncols, epilogue=None, col_scale=None):
    s, d = h.shape
    tm, tn = PROJ_ROWS, PROJ_COLS
    jb = col0 // tn
    in_specs = [pl.BlockSpec((tm, d), lambda i, j: (i, 0)),
                pl.BlockSpec((d, tn), lambda i, j: (0, j + jb))]
    if col_scale is None:
        body, args, name = functools.partial(_proj_kernel, epilogue=epilogue), (h, w), epilogue
    else:
        in_specs.append(pl.BlockSpec((1, tn), lambda i, j: (0, j)))
        body, args, name = _proj_scale_kernel, (h, w, col_scale.reshape(1, ncols)), "scale"
    windows = _nbytes((tm, d), BF16, 2) + _nbytes((d, tn), F32, 2) + _nbytes((tm, tn), BF16, 2)
    return pl.pallas_call(
        body,
        grid=(s // tm, ncols // tn),
        in_specs=in_specs,
        out_specs=pl.BlockSpec((tm, tn), lambda i, j: (i, j)),
        out_shape=jax.ShapeDtypeStruct((s, ncols), BF16),
        compiler_params=_params(("arbitrary", "arbitrary"), windows),
        name="in_proj_" + name,
    )(*args)


def _gmlp_kernel(u_ref, v_ref, ga_ref, gain_ref, ws_ref, bt_ref, wpa_ref, o_ref, wm_ref, a_ref):
    tq = u_ref.shape[0]

    @pl.when(pl.program_id(0) == 0)
    def _():
        t = lax.broadcasted_iota(jnp.int32, (A_CHUNK, A_CHUNK), 0)
        s = lax.broadcasted_iota(jnp.int32, (A_CHUNK, A_CHUNK), 1)
        for g in range(A_GROUPS):
            wm_ref[g] = jnp.where(t >= s, ws_ref[g], 0.0).astype(BF16)

    def gate_chunk(c):
        rows = slice(c * A_CHUNK, (c + 1) * A_CHUNK)
        v = v_ref[rows, :].astype(F32)
        mu = jnp.mean(v, axis=-1, keepdims=True)
        d = v - mu
        var = jnp.mean(d * d, axis=-1, keepdims=True)
        vn = (d * lax.rsqrt(var + EPS) * gain_ref[...]).astype(BF16)
        for g in range(A_GROUPS):
            cols = slice(g * A_GROUP_DIM, (g + 1) * A_GROUP_DIM)
            sv = jnp.dot(wm_ref[g], vn[:, cols], preferred_element_type=F32) + bt_ref[:, g:g + 1]
            a_ref[rows, cols] = (u_ref[rows, cols].astype(F32) * sv).astype(BF16)

    wpa = wpa_ref[...].astype(BF16)
    n_chunks = tq // A_CHUNK
    for part in range(GMLP_PARTS):
        for c in range(part * n_chunks // GMLP_PARTS, (part + 1) * n_chunks // GMLP_PARTS):
            gate_chunk(c)
        rows = slice(part * tq // GMLP_PARTS, (part + 1) * tq // GMLP_PARTS)
        y = jnp.dot(a_ref[rows, :], wpa, preferred_element_type=F32)
        o_ref[rows, :] = ga_ref[rows, :].astype(F32) * y


def _gmlp(zuv, zg, gain, w_s, b_s, w_proj_a):
    s = zuv.shape[0]
    tq = MIX_ROWS
    aw = A_GROUPS * A_GROUP_DIM
    d = w_proj_a.shape[1]
    windows = (_nbytes((tq, aw), BF16, 4) + _nbytes((tq, d), BF16, 2) + _nbytes((tq, d), F32, 2)
               + _nbytes((aw, d), F32) + _nbytes((A_GROUPS, A_CHUNK, A_CHUNK), F32)
               + _nbytes((A_GROUPS, A_CHUNK, A_CHUNK), BF16) + _nbytes((tq, aw), BF16))
    return pl.pallas_call(
        _gmlp_kernel,
        grid=(s // tq,),
        in_specs=[pl.BlockSpec((tq, aw), lambda i: (i, 0)),
                  pl.BlockSpec((tq, aw), lambda i: (i, 1)),
                  pl.BlockSpec((tq, d), lambda i: (i, 0)),
                  _resident((1, aw), lambda i: (0, 0)),
                  _resident((A_GROUPS, A_CHUNK, A_CHUNK), lambda i: (0, 0, 0)),
                  _resident((A_CHUNK, A_GROUPS), lambda i: (0, 0)),
                  _resident((aw, d), lambda i: (0, 0))],
        out_specs=pl.BlockSpec((tq, d), lambda i: (i, 0)),
        out_shape=jax.ShapeDtypeStruct((s, d), F32),
        scratch_shapes=[pltpu.VMEM((A_GROUPS, A_CHUNK, A_CHUNK), BF16),
                        pltpu.VMEM((tq, aw), BF16)],
        compiler_params=_params(("arbitrary",), windows),
        name="gmlp_mixer",
    )(zuv, zuv, zg, gain.reshape(1, aw), w_s, b_s.T, w_proj_a)


def _t5_bias_log2(dist, rb_ref, base):
    max_exact = REL_BUCKETS // 2
    n = jnp.maximum(dist, 0)
    nf = jnp.maximum(n, 1).astype(F32)
    large = max_exact + (jnp.log(nf / max_exact) / math.log(REL_MAX_DIST / max_exact)
                         * (REL_BUCKETS - max_exact)).astype(jnp.int32)
    large = jnp.minimum(large, REL_BUCKETS - 1)
    bucket = jnp.where(n < max_exact, n, large)
    out = jnp.zeros(dist.shape, F32)
    for b in range(REL_BUCKETS):
        out = jnp.where(bucket == b, rb_ref[base + b] * LOG2E, out)
    return out


def _moba_kernel(rb_ref, q_ref, k_ref, v_ref, o_ref,
                 kmean_ref, vt_ref, tile_ref, rows_ref, m_ref, acc_ref, *slots):
    L = MOBA_BLOCK
    G = FAR_GROUP
    QB = Q_STEP_BLOCKS
    W = QB * L
    dh = B_HEAD_DIM
    nb = q_ref.shape[0] // L
    nq = nb // QB
    log2_l = L.bit_length() - 1
    base = pl.program_id(0) * REL_BUCKETS
    far_bias = rb_ref[base + REL_BUCKETS - 1] * LOG2E

    def prep(it, carry):
        for i in range(PREP_UNROLL):
            n = it * PREP_UNROLL + i
            r0 = pl.multiple_of(n * L, L)
            kmean_ref[pl.ds(n, 1), :] = jnp.mean(k_ref[pl.ds(r0, L), :].astype(F32), axis=0, keepdims=True)
            vt_ref[n, 0:dh, :] = v_ref[pl.ds(r0, L), :].astype(F32).T.astype(BF16)
            vt_ref[n, dh:, :] = jnp.ones((vt_ref.shape[1] - dh, L), BF16)
        return carry

    lax.fori_loop(0, nb // PREP_UNROLL, prep, 0)

    dist = lax.broadcasted_iota(jnp.int32, (1, 2 * L), 1) - L

    def toeplitz(row):
        return pltpu.roll(jnp.broadcast_to(row, (L, 2 * L)), L, 1, stride=1, stride_axis=0)[:, :L]

    b_prev = toeplitz(_t5_bias_log2(dist + L, rb_ref, base))
    b_own = toeplitz(jnp.where(dist >= 0, _t5_bias_log2(dist, rb_ref, base), NEG_LOG2))
    masked = jnp.full((L, L), NEG_LOG2, F32)
    zero = jnp.zeros((L, L), F32)
    kinds = [[[zero, zero], [zero, zero]], [[zero, zero], [b_prev, zero]], [[b_own, b_prev], [masked, b_own]]]
    for kind, blocks in enumerate(kinds):
        for r, tiles in enumerate(blocks):
            for c, tile in enumerate(tiles):
                tile_ref[kind, r * L:(r + 1) * L, c * L:(c + 1) * L] = tile

    kmean = kmean_ref[...].astype(BF16)

    def gate(it, carry):
        u0 = it * GATE_STEPS
        q_c = q_ref[pl.ds(pl.multiple_of(u0 * W, GATE_STEPS * W), GATE_STEPS * W), :]
        gs = lax.dot_general(kmean, q_c, NT_DIMS, preferred_element_type=F32) * (1.0 / K_SCALE)
        blk = lax.broadcasted_iota(jnp.int32, gs.shape, 0)
        own = QB * u0 + lax.shift_right_logical(lax.broadcasted_iota(jnp.int32, gs.shape, 1), log2_l)
        gs = jnp.where(blk < own, gs, NEG)
        sel = jnp.zeros(gs.shape, F32)
        for _ in range(MOBA_TOPK):
            top = jnp.max(gs, axis=0, keepdims=True)
            idx = jnp.min(jnp.where(gs == top, blk, nb), axis=0, keepdims=True)
            hit = blk == idx
            sel = jnp.where(hit, jnp.where(top > NEG / 2, 1.0, 0.0), sel)
            gs = jnp.where(hit, REMOVED, gs)
        picked = sel > 0.5
        rows = jnp.where(blk < own - 1, jnp.where(picked, far_bias, FAR_MASKED),
                         jnp.where(blk == own - 1, jnp.where(picked, 0.0, FAR_MASKED), 0.0))
        for i in range(GATE_STEPS):
            rows_ref[u0 + i] = rows[:, i * W:(i + 1) * W]
        return carry

    lax.fori_loop(0, nq // GATE_STEPS, gate, 0)

    def item(t, ua):
        first = t <= ua
        u = jnp.where(first, ua, nq - 1 - ua)
        g = jnp.where(first, t, t - ua - 1)
        return u, g, jnp.where(first, 0, 1), jnp.clip(g - (u - 2), 0, 2)

    def rows_of(u, g):
        return [rows_ref[u, pl.ds(g * G + b, 1), :] for b in range(G)]

    def produce(n, t, ua):
        u, g, _, kind = item(t, ua)
        q_u = q_ref[pl.ds(pl.multiple_of(u * W, W), W), :]
        k_g = k_ref[pl.ds(pl.multiple_of(g * (G * L), G * L), G * L), :]
        s_t = lax.dot_general(k_g, q_u, NT_DIMS, preferred_element_type=F32) + tile_ref[kind]
        slots[n % len(slots)][...] = s_t
        rows = rows_of(u, g)
        return functools.reduce(jnp.maximum, [
            jnp.max(s_t[b * L:(b + 1) * L], axis=0, keepdims=True) + rows[b] for b in range(G)])

    def consume(n, t, ua, pi, cand):
        u, g, w, _ = item(t, ua)
        rows = rows_of(u, g)
        s_ref = slots[n % len(slots)]
        st = pi * 2 + w
        m_old = m_ref[st]
        m_new = jnp.maximum(m_old, cand)
        pv = None
        for b in range(G):
            p = jnp.exp2(s_ref[b * L:(b + 1) * L, :] + (rows[b] - m_new)).astype(BF16)
            d = jnp.dot(vt_ref[g * G + b], p, preferred_element_type=F32)
            pv = d if pv is None else pv + d
        acc_ref[st] = jnp.exp2(m_old - m_new) * acc_ref[st] + pv
        m_ref[st] = m_new

    def finish(ua, pi):
        for w, u in ((0, ua), (1, nq - 1 - ua)):
            acc = acc_ref[pi * 2 + w]
            o_ref[pl.ds(pl.multiple_of(u * W, W), W), :] = (acc[:dh] / acc[dh:dh + 1]).T.astype(o_ref.dtype)

    def stream(it, carry):
        m_ref[...] = jnp.full(m_ref.shape, NEG_LOG2, F32)
        acc_ref[...] = jnp.zeros(acc_ref.shape, F32)
        seq = [(pi, t) for pi in range(PAIRS_PER_ITER) for t in range(nq + 1)]
        first_step = lambda pi: it * PAIRS_PER_ITER + pi
        cand = produce(0, seq[0][1], first_step(seq[0][0]))
        for n, (pi, t) in enumerate(seq):
            if n + 1 < len(seq):
                cand_next = produce(n + 1, seq[n + 1][1], first_step(seq[n + 1][0]))
            consume(n, t, first_step(pi), pi, cand)
            cand = cand_next
            if t == nq:
                finish(first_step(pi), pi)
        return carry

    lax.fori_loop(0, nq // 2 // PAIRS_PER_ITER, stream, 0)


def _moba(zqkv, rel_bias):
    s = zqkv.shape[0]
    dh = B_HEAD_DIM
    L = MOBA_BLOCK
    nb = s // L
    assert Q_STEP_BLOCKS == 2 and FAR_GROUP == 2 and nb % (2 * Q_STEP_BLOCKS) == 0
    assert nb % PREP_UNROLL == 0 and (nb // Q_STEP_BLOCKS) % GATE_STEPS == 0
    assert (nb // Q_STEP_BLOCKS // 2) % PAIRS_PER_ITER == 0
    w = Q_STEP_BLOCKS * L
    rb = rel_bias.T.reshape(-1).astype(F32)
    scratch = [pltpu.VMEM((nb, dh), F32),
               pltpu.VMEM((nb, dh + BF16_SUBLANES, L), BF16),
               pltpu.VMEM((3, FAR_GROUP * L, w), F32),
               pltpu.VMEM((nb // Q_STEP_BLOCKS, nb, w), F32),
               pltpu.VMEM((2 * PAIRS_PER_ITER, 1, w), F32),
               pltpu.VMEM((2 * PAIRS_PER_ITER, dh + BF16_SUBLANES, w), F32)]
    scratch += [pltpu.VMEM((FAR_GROUP * L, w), F32)] * SCORE_SLOTS
    windows = _nbytes((s, dh), BF16, 8) + sum(_nbytes(b.shape, b.dtype) for b in scratch)
    return pl.pallas_call(
        _moba_kernel,
        grid=(B_HEADS,),
        in_specs=[pl.BlockSpec(memory_space=pltpu.SMEM),
                  pl.BlockSpec((s, dh), lambda h: (0, h)),
                  pl.BlockSpec((s, dh), lambda h: (0, B_HEADS + h)),
                  pl.BlockSpec((s, dh), lambda h: (0, 2 * B_HEADS + h))],
        out_specs=pl.BlockSpec((s, dh), lambda h: (0, h)),
        out_shape=jax.ShapeDtypeStruct((s, B_HEADS * dh), BF16),
        scratch_shapes=scratch,
        compiler_params=_params(("arbitrary",), windows),
        name="moba_attention",
    )(rb, zqkv, zqkv, zqkv)


def _merge_kernel(attn_ref, gb_ref, gaya_ref, x_ref, wpb_ref, wout_ref, g_ref, x1_ref, h2_ref):
    y_b = jnp.dot(attn_ref[...], wpb_ref[...], preferred_element_type=F32)
    merged = gaya_ref[...] + gb_ref[...].astype(F32) * y_b
    x1 = x_ref[...] + jnp.dot(merged.astype(BF16), wout_ref[...], preferred_element_type=F32)
    x1_ref[...] = x1
    h2_ref[...] = _rms(x1, g_ref[...]).astype(h2_ref.dtype)


def _merge(attn, zg, gaya, x, w_proj_b, w_out, gain):
    s, d = x.shape
    tq = MIX_ROWS
    bw = attn.shape[1]
    windows = (_nbytes((tq, bw), BF16, 2) + _nbytes((tq, d), BF16, 2) + _nbytes((tq, d), F32, 4)
               + _nbytes((bw, d), BF16) + _nbytes((d, d), BF16)
               + _nbytes((tq, d), F32, 2) + _nbytes((tq, d), BF16, 2))
    return pl.pallas_call(
        _merge_kernel,
        grid=(s // tq,),
        in_specs=[pl.BlockSpec((tq, bw), lambda i: (i, 0)),
                  pl.BlockSpec((tq, d), lambda i: (i, 1)),
                  pl.BlockSpec((tq, d), lambda i: (i, 0)),
                  pl.BlockSpec((tq, d), lambda i: (i, 0)),
                  _resident((bw, d), lambda i: (0, 0)),
                  _resident((d, d), lambda i: (0, 0)),
                  _resident((1, d), lambda i: (0, 0))],
        out_specs=[pl.BlockSpec((tq, d), lambda i: (i, 0)),
                   pl.BlockSpec((tq, d), lambda i: (i, 0))],
        out_shape=[jax.ShapeDtypeStruct((s, d), F32),
                   jax.ShapeDtypeStruct((s, d), BF16)],
        compiler_params=_params(("arbitrary",), windows),
        name="merge_out_proj",
    )(attn, zg, gaya, x, w_proj_b, w_out, gain.reshape(1, d))


def _ffn_kernel(h_ref, wu_ref, wd_ref, x1_ref, g_ref, o_ref, *, final_norm):
    j = pl.program_id(1)

    @pl.when(j == 0)
    def _():
        o_ref[...] = x1_ref[...]

    a = jnp.dot(h_ref[...], wu_ref[...].astype(BF16), preferred_element_type=F32)
    a = jnp.square(jnp.maximum(a, 0.0)).astype(BF16)
    o_ref[...] += jnp.dot(a, wd_ref[...].astype(BF16), preferred_element_type=F32)

    if final_norm:
        @pl.when(j == pl.num_programs(1) - 1)
        def _():
            o_ref[...] = _rms(o_ref[...], g_ref[...])


def _ffn(h2, w_up, w_down, x1, gain, final_norm):
    s, d = x1.shape
    tm, tf = FFN_ROWS, FFN_COLS
    dff = w_up.shape[1]
    windows = _nbytes((tm, d), BF16, 2) + _nbytes((d, tf), F32, 4) + _nbytes((tm, d), F32, 4)
    return pl.pallas_call(
        functools.partial(_ffn_kernel, final_norm=final_norm),
        grid=(s // tm, dff // tf),
        in_specs=[pl.BlockSpec((tm, d), lambda i, j: (i, 0)),
                  pl.BlockSpec((d, tf), lambda i, j: (0, j)),
                  pl.BlockSpec((tf, d), lambda i, j: (j, 0)),
                  pl.BlockSpec((tm, d), lambda i, j: (i, 0)),
                  _resident((1, d), lambda i, j: (0, 0))],
        out_specs=pl.BlockSpec((tm, d), lambda i, j: (i, 0)),
        out_shape=jax.ShapeDtypeStruct((s, d), F32),
        compiler_params=_params(("arbitrary", "arbitrary"), windows),
        name="ffn_relu2",
    )(h2, w_up, w_down, x1, gain.reshape(1, d))


def kernel(x, ln_mix, w_in, a_v_gain, a_spatial, a_spatial_bias, w_proj_a, w_proj_b, w_out,
           rel_bias, ln_mlp, w_up, w_down, ln_final):
    bsz, s, d = x.shape
    depth = ln_mix.shape[0]
    aw = A_GROUPS * A_GROUP_DIM
    bw = B_HEADS * B_HEAD_DIM
    assert w_in.shape[2] == 2 * aw + 3 * bw + 2 * d
    qkv_scale = jnp.concatenate([jnp.ones((bw,), F32), jnp.full((bw,), K_SCALE, F32), jnp.ones((bw,), F32)])
    outs = []
    for b in range(bsz):
        xb = x[b]
        for l in range(depth):
            last = l == depth - 1
            zuv, h = _norm_in_proj(xb, ln_mix[l], w_in[l], 2 * aw, "gelu")
            zqkv = _in_proj(h, w_in[l], 2 * aw, 3 * bw, col_scale=qkv_scale)
            zg = _in_proj(h, w_in[l], 2 * aw + 3 * bw, 2 * d, "sigmoid")
            gaya = _gmlp(zuv, zg, a_v_gain[l], a_spatial[l], a_spatial_bias[l], w_proj_a[l])
            attn = _moba(zqkv, rel_bias)
            x1, h2 = _merge(attn, zg, gaya, xb, w_proj_b[l].astype(BF16), w_out[l].astype(BF16), ln_mlp[l])
            gain = ln_final if last else jnp.ones((d,), F32)
            xb = _ffn(h2, w_up[l], w_down[l], x1, gain, final_norm=last)
        outs.append(xb)
    return jnp.stack(outs)
```

```python
import functools
import math

import jax
import jax.numpy as jnp
from jax import lax
from jax.experimental import pallas as pl
from jax.experimental.pallas import tpu as pltpu

A_GROUPS = 16
A_GROUP_DIM = 128
A_CHUNK = 128
B_HEADS = 16
B_HEAD_DIM = 128
MOBA_BLOCK = 256
MOBA_TOPK = 3
FAR_GROUP = 2
Q_STEP_BLOCKS = 2
PAIRS_PER_ITER = 2
SCORE_SLOTS = 2
GATE_STEPS = 8
PREP_UNROLL = 8
REL_BUCKETS = 32
REL_MAX_DIST = 128
EPS = 1e-6
NEG = -1e30
LOG2E = math.log2(math.e)
NEG_LOG2 = NEG * LOG2E
FAR_MASKED = 2 * NEG_LOG2
REMOVED = -3e38
K_SCALE = (B_HEAD_DIM ** -0.5) * LOG2E
BF16_SUBLANES = 16

F32 = jnp.float32
BF16 = jnp.bfloat16

MIB = 1024 * 1024
NT_DIMS = (((1,), (1,)), ((), ()))

V7X_VMEM_MIB = 64
PROJ_ROWS = 1024
PROJ_COLS = 1024
MIX_ROWS = 512
GMLP_PARTS = 2
FFN_ROWS = 1024
FFN_COLS = 512


def _params(semantics, window_bytes):
    limit = min(window_bytes + V7X_VMEM_MIB * MIB // 4, V7X_VMEM_MIB * MIB * 15 // 16)
    return pltpu.CompilerParams(dimension_semantics=semantics, vmem_limit_bytes=limit)


def _nbytes(shape, dtype, buffers=1):
    return math.prod(shape) * jnp.dtype(dtype).itemsize * buffers


def _resident(shape, index_map):
    return pl.BlockSpec(shape, index_map, pipeline_mode=pl.Buffered(1))


def _rms(xf, gain):
    return xf * lax.rsqrt(jnp.mean(xf * xf, axis=-1, keepdims=True) + EPS) * gain


def _gelu(a):
    return 0.5 * a * (1.0 + lax.erf(a * math.sqrt(0.5)))


_EPILOGUES = {"gelu": _gelu, "sigmoid": jax.nn.sigmoid}


def _proj_kernel(h_ref, w_ref, o_ref, *, epilogue):
    acc = jnp.dot(h_ref[...], w_ref[...].astype(BF16), preferred_element_type=F32)
    o_ref[...] = _EPILOGUES[epilogue](acc).astype(o_ref.dtype)


def _proj_scale_kernel(h_ref, w_ref, scale_ref, o_ref):
    acc = jnp.dot(h_ref[...], w_ref[...].astype(BF16), preferred_element_type=F32)
    z = (acc * scale_ref[...]).astype(o_ref.dtype)
    dh = o_ref.shape[2]
    for hh in range(o_ref.shape[0]):
        o_ref[hh] = z[:, hh * dh:(hh + 1) * dh]


def _norm_proj_kernel(x_ref, g_ref, w_ref, o_ref, h_ref, *, epilogue):
    @pl.when(pl.program_id(1) == 0)
    def _():
        h_ref[...] = _rms(x_ref[...], g_ref[...]).astype(h_ref.dtype)

    acc = jnp.dot(h_ref[...], w_ref[...].astype(BF16), preferred_element_type=F32)
    o_ref[...] = _EPILOGUES[epilogue](acc).astype(o_ref.dtype)


def _norm_in_proj(x, gain, w, ncols, epilogue):
    s, d = x.shape
    tm, tn = PROJ_ROWS, PROJ_COLS
    windows = (_nbytes((tm, d), F32, 2) + _nbytes((d, tn), F32, 2) + _nbytes((tm, tn), BF16, 2)
               + _nbytes((tm, d), BF16, 2))
    return pl.pallas_call(
        functools.partial(_norm_proj_kernel, epilogue=epilogue),
        grid=(s // tm, ncols // tn),
        in_specs=[pl.BlockSpec((tm, d), lambda i, j: (i, 0)),
                  pl.BlockSpec((1, d), lambda i, j: (0, 0)),
                  pl.BlockSpec((d, tn), lambda i, j: (0, j))],
        out_specs=[pl.BlockSpec((tm, tn), lambda i, j: (i, j)),
                   pl.BlockSpec((tm, d), lambda i, j: (i, 0))],
        out_shape=[jax.ShapeDtypeStruct((s, ncols), BF16),
                   jax.ShapeDtypeStruct((s, d), BF16)],
        compiler_params=_params(("arbitrary", "arbitrary"), windows),
        name="norm_in_proj_" + epilogue,
    )(x, gain.reshape(1, d), w)


def _in_proj(h, w, col0, ncols, epilogue=None, col_scale=None):
    s, d = h.shape
    tm, tn = PROJ_ROWS, PROJ_COLS
    jb = col0 // tn
    in_specs = [pl.BlockSpec((tm, d), lambda i, j: (i, 0)),
                pl.BlockSpec((d, tn), lambda i, j: (0, j + jb))]
    if col_scale is None:
        body, args, name = functools.partial(_proj_kernel, epilogue=epilogue), (h, w), epilogue
        out_spec = pl.BlockSpec((tm, tn), lambda i, j: (i, j))
        out_shape = jax.ShapeDtypeStruct((s, ncols), BF16)
    else:
        in_specs.append(pl.BlockSpec((1, tn), lambda i, j: (0, j)))
        body, args, name = _proj_scale_kernel, (h, w, col_scale.reshape(1, ncols)), "scale"
        out_spec = pl.BlockSpec((tn // B_HEAD_DIM, tm, B_HEAD_DIM), lambda i, j: (j, i, 0))
        out_shape = jax.ShapeDtypeStruct((ncols // B_HEAD_DIM, s, B_HEAD_DIM), BF16)
    windows = _nbytes((tm, d), BF16, 2) + _nbytes((d, tn), F32, 2) + _nbytes((tm, tn), BF16, 2)
    return pl.pallas_call(
        body,
        grid=(s // tm, ncols // tn),
        in_specs=in_specs,
        out_specs=out_spec,
        out_shape=out_shape,
        compiler_params=_params(("arbitrary", "arbitrary"), windows),
        name="in_proj_" + name,
    )(*args)


def _gmlp_kernel(u_ref, v_ref, ga_ref, gain_ref, ws_ref, bt_ref, wpa_ref, o_ref, wm_ref, a_ref):
    tq = u_ref.shape[0]

    @pl.when(pl.program_id(0) == 0)
    def _():
        t = lax.broadcasted_iota(jnp.int32, (A_CHUNK, A_CHUNK), 0)
        s = lax.broadcasted_iota(jnp.int32, (A_CHUNK, A_CHUNK), 1)
        for g in range(A_GROUPS):
            wm_ref[g] = jnp.where(t >= s, ws_ref[g], 0.0).astype(BF16)

    def gate_chunk(c):
        rows = slice(c * A_CHUNK, (c + 1) * A_CHUNK)
        v = v_ref[rows, :].astype(F32)
        mu = jnp.mean(v, axis=-1, keepdims=True)
        d = v - mu
        var = jnp.mean(d * d, axis=-1, keepdims=True)
        vn = (d * lax.rsqrt(var + EPS) * gain_ref[...]).astype(BF16)
        for g in range(A_GROUPS):
            cols = slice(g * A_GROUP_DIM, (g + 1) * A_GROUP_DIM)
            sv = jnp.dot(wm_ref[g], vn[:, cols], preferred_element_type=F32) + bt_ref[:, g:g + 1]
            a_ref[rows, cols] = (u_ref[rows, cols].astype(F32) * sv).astype(BF16)

    wpa = wpa_ref[...].astype(BF16)
    n_chunks = tq // A_CHUNK
    for part in range(GMLP_PARTS):
        for c in range(part * n_chunks // GMLP_PARTS, (part + 1) * n_chunks // GMLP_PARTS):
            gate_chunk(c)
        rows = slice(part * tq // GMLP_PARTS, (part + 1) * tq // GMLP_PARTS)
        y = jnp.dot(a_ref[rows, :], wpa, preferred_element_type=F32)
        o_ref[rows, :] = ga_ref[rows, :].astype(F32) * y


def _gmlp(zuv, zg, gain, w_s, b_s, w_proj_a):
    s = zuv.shape[0]
    tq = MIX_ROWS
    aw = A_GROUPS * A_GROUP_DIM
    d = w_proj_a.shape[1]
    windows = (_nbytes((tq, aw), BF16, 4) + _nbytes((tq, d), BF16, 2) + _nbytes((tq, d), F32, 2)
               + _nbytes((aw, d), F32) + _nbytes((A_GROUPS, A_CHUNK, A_CHUNK), F32)
               + _nbytes((A_GROUPS, A_CHUNK, A_CHUNK), BF16) + _nbytes((tq, aw), BF16))
    return pl.pallas_call(
        _gmlp_kernel,
        grid=(s // tq,),
        in_specs=[pl.BlockSpec((tq, aw), lambda i: (i, 0)),
                  pl.BlockSpec((tq, aw), lambda i: (i, 1)),
                  pl.BlockSpec((tq, d), lambda i: (i, 0)),
                  _resident((1, aw), lambda i: (0, 0)),
                  _resident((A_GROUPS, A_CHUNK, A_CHUNK), lambda i: (0, 0, 0)),
                  _resident((A_CHUNK, A_GROUPS), lambda i: (0, 0)),
                  _resident((aw, d), lambda i: (0, 0))],
        out_specs=pl.BlockSpec((tq, d), lambda i: (i, 0)),
        out_shape=jax.ShapeDtypeStruct((s, d), F32),
        scratch_shapes=[pltpu.VMEM((A_GROUPS, A_CHUNK, A_CHUNK), BF16),
                        pltpu.VMEM((tq, aw), BF16)],
        compiler_params=_params(("arbitrary",), windows),
        name="gmlp_mixer",
    )(zuv, zuv, zg, gain.reshape(1, aw), w_s, b_s.T, w_proj_a)


def _t5_bias_log2(dist, rb_ref, base):
    max_exact = REL_BUCKETS // 2
    n = jnp.maximum(dist, 0)
    nf = jnp.maximum(n, 1).astype(F32)
    large = max_exact + (jnp.log(nf / max_exact) / math.log(REL_MAX_DIST / max_exact)
                         * (REL_BUCKETS - max_exact)).astype(jnp.int32)
    large = jnp.minimum(large, REL_BUCKETS - 1)
    bucket = jnp.where(n < max_exact, n, large)
    out = jnp.zeros(dist.shape, F32)
    for b in range(REL_BUCKETS):
        out = jnp.where(bucket == b, rb_ref[base + b] * LOG2E, out)
    return out


def _moba_kernel(rb_ref, q_ref, k_ref, v_ref, o_ref,
                 kmean_ref, vt_ref, tile_ref, rows_ref, m_ref, acc_ref, *slots):
    L = MOBA_BLOCK
    G = FAR_GROUP
    QB = Q_STEP_BLOCKS
    W = QB * L
    dh = B_HEAD_DIM
    nb = q_ref.shape[0] // L
    nq = nb // QB
    log2_l = L.bit_length() - 1
    base = pl.program_id(0) * REL_BUCKETS
    far_bias = rb_ref[base + REL_BUCKETS - 1] * LOG2E

    def prep(it, carry):
        for i in range(PREP_UNROLL):
            n = it * PREP_UNROLL + i
            r0 = pl.multiple_of(n * L, L)
            kmean_ref[pl.ds(n, 1), :] = jnp.mean(k_ref[pl.ds(r0, L), :].astype(F32), axis=0, keepdims=True)
            vt_ref[n, 0:dh, :] = v_ref[pl.ds(r0, L), :].astype(F32).T.astype(BF16)
            vt_ref[n, dh:, :] = jnp.ones((vt_ref.shape[1] - dh, L), BF16)
        return carry

    lax.fori_loop(0, nb // PREP_UNROLL, prep, 0)

    dist = lax.broadcasted_iota(jnp.int32, (1, 2 * L), 1) - L

    def toeplitz(row):
        return pltpu.roll(jnp.broadcast_to(row, (L, 2 * L)), L, 1, stride=1, stride_axis=0)[:, :L]

    b_prev = toeplitz(_t5_bias_log2(dist + L, rb_ref, base))
    b_own = toeplitz(jnp.where(dist >= 0, _t5_bias_log2(dist, rb_ref, base), NEG_LOG2))
    masked = jnp.full((L, L), NEG_LOG2, F32)
    zero = jnp.zeros((L, L), F32)
    kinds = [[[zero, zero], [zero, zero]], [[zero, zero], [b_prev, zero]], [[b_own, b_prev], [masked, b_own]]]
    for kind, blocks in enumerate(kinds):
        for r, tiles in enumerate(blocks):
            for c, tile in enumerate(tiles):
                tile_ref[kind, r * L:(r + 1) * L, c * L:(c + 1) * L] = tile

    kmean = kmean_ref[...].astype(BF16)

    def gate(it, carry):
        u0 = it * GATE_STEPS
        q_c = q_ref[pl.ds(pl.multiple_of(u0 * W, GATE_STEPS * W), GATE_STEPS * W), :]
        gs = lax.dot_general(kmean, q_c, NT_DIMS, preferred_element_type=F32) * (1.0 / K_SCALE)
        blk = lax.broadcasted_iota(jnp.int32, gs.shape, 0)
        own = QB * u0 + lax.shift_right_logical(lax.broadcasted_iota(jnp.int32, gs.shape, 1), log2_l)
        gs = jnp.where(blk < own, gs, NEG)
        sel = jnp.zeros(gs.shape, F32)
        for _ in range(MOBA_TOPK):
            top = jnp.max(gs, axis=0, keepdims=True)
            idx = jnp.min(jnp.where(gs == top, blk, nb), axis=0, keepdims=True)
            hit = blk == idx
            sel = jnp.where(hit, jnp.where(top > NEG / 2, 1.0, 0.0), sel)
            gs = jnp.where(hit, REMOVED, gs)
        picked = sel > 0.5
        rows = jnp.where(blk < own - 1, jnp.where(picked, far_bias, FAR_MASKED),
                         jnp.where(blk == own - 1, jnp.where(picked, 0.0, FAR_MASKED), 0.0))
        for i in range(GATE_STEPS):
            rows_ref[u0 + i] = rows[:, i * W:(i + 1) * W]
        return carry

    lax.fori_loop(0, nq // GATE_STEPS, gate, 0)

    def item(t, ua):
        first = t <= ua
        u = jnp.where(first, ua, nq - 1 - ua)
        g = jnp.where(first, t, t - ua - 1)
        return u, g, jnp.where(first, 0, 1), jnp.clip(g - (u - 2), 0, 2)

    def rows_of(u, g):
        return [rows_ref[u, pl.ds(g * G + b, 1), :] for b in range(G)]

    def produce(n, t, ua):
        u, g, _, kind = item(t, ua)
        q_u = q_ref[pl.ds(pl.multiple_of(u * W, W), W), :]
        k_g = k_ref[pl.ds(pl.multiple_of(g * (G * L), G * L), G * L), :]
        s_t = lax.dot_general(k_g, q_u, NT_DIMS, preferred_element_type=F32) + tile_ref[kind]
        slots[n % len(slots)][...] = s_t
        rows = rows_of(u, g)
        return functools.reduce(jnp.maximum, [
            jnp.max(s_t[b * L:(b + 1) * L], axis=0, keepdims=True) + rows[b] for b in range(G)])

    def consume(n, t, ua, pi, cand):
        u, g, w, _ = item(t, ua)
        rows = rows_of(u, g)
        s_ref = slots[n % len(slots)]
        st = pi * 2 + w
        m_old = m_ref[st]
        m_new = jnp.maximum(m_old, cand)
        pv = None
        for b in range(G):
            p = jnp.exp2(s_ref[b * L:(b + 1) * L, :] + (rows[b] - m_new)).astype(BF16)
            d = jnp.dot(vt_ref[g * G + b], p, preferred_element_type=F32)
            pv = d if pv is None else pv + d
        acc_ref[st] = jnp.exp2(m_old - m_new) * acc_ref[st] + pv
        m_ref[st] = m_new

    def finish(ua, pi):
        for w, u in ((0, ua), (1, nq - 1 - ua)):
            acc = acc_ref[pi * 2 + w]
            o_ref[pl.ds(pl.multiple_of(u * W, W), W), :] = (acc[:dh] / acc[dh:dh + 1]).T.astype(o_ref.dtype)

    def stream(it, carry):
        m_ref[...] = jnp.full(m_ref.shape, NEG_LOG2, F32)
        acc_ref[...] = jnp.zeros(acc_ref.shape, F32)
        seq = [(pi, t) for pi in range(PAIRS_PER_ITER) for t in range(nq + 1)]
        first_step = lambda pi: it * PAIRS_PER_ITER + pi
        cand = produce(0, seq[0][1], first_step(seq[0][0]))
        for n, (pi, t) in enumerate(seq):
            if n + 1 < len(seq):
                cand_next = produce(n + 1, seq[n + 1][1], first_step(seq[n + 1][0]))
            consume(n, t, first_step(pi), pi, cand)
            cand = cand_next
            if t == nq:
                finish(first_step(pi), pi)
        return carry

    lax.fori_loop(0, nq // 2 // PAIRS_PER_ITER, stream, 0)


def _moba(zqkv, rel_bias):
    s = zqkv.shape[1]
    dh = B_HEAD_DIM
    L = MOBA_BLOCK
    nb = s // L
    assert Q_STEP_BLOCKS == 2 and FAR_GROUP == 2 and nb % (2 * Q_STEP_BLOCKS) == 0
    assert nb % PREP_UNROLL == 0 and (nb // Q_STEP_BLOCKS) % GATE_STEPS == 0
    assert (nb // Q_STEP_BLOCKS // 2) % PAIRS_PER_ITER == 0
    w = Q_STEP_BLOCKS * L
    rb = rel_bias.T.reshape(-1).astype(F32)
    scratch = [pltpu.VMEM((nb, dh), F32),
               pltpu.VMEM((nb, dh + BF16_SUBLANES, L), BF16),
               pltpu.VMEM((3, FAR_GROUP * L, w), F32),
               pltpu.VMEM((nb // Q_STEP_BLOCKS, nb, w), F32),
               pltpu.VMEM((2 * PAIRS_PER_ITER, 1, w), F32),
               pltpu.VMEM((2 * PAIRS_PER_ITER, dh + BF16_SUBLANES, w), F32)]
    scratch += [pltpu.VMEM((FAR_GROUP * L, w), F32)] * SCORE_SLOTS
    windows = _nbytes((s, dh), BF16, 8) + sum(_nbytes(b.shape, b.dtype) for b in scratch)
    return pl.pallas_call(
        _moba_kernel,
        grid=(B_HEADS,),
        in_specs=[pl.BlockSpec(memory_space=pltpu.SMEM),
                  pl.BlockSpec((None, s, dh), lambda h: (h, 0, 0)),
                  pl.BlockSpec((None, s, dh), lambda h: (B_HEADS + h, 0, 0)),
                  pl.BlockSpec((None, s, dh), lambda h: (2 * B_HEADS + h, 0, 0))],
        out_specs=pl.BlockSpec((s, dh), lambda h: (0, h)),
        out_shape=jax.ShapeDtypeStruct((s, B_HEADS * dh), BF16),
        scratch_shapes=scratch,
        compiler_params=_params(("arbitrary",), windows),
        name="moba_attention",
    )(rb, zqkv, zqkv, zqkv)


def _merge_kernel(attn_ref, gb_ref, gaya_ref, x_ref, wpb_ref, wout_ref, g_ref, x1_ref, h2_ref):
    y_b = jnp.dot(attn_ref[...], wpb_ref[...], preferred_element_type=F32)
    merged = gaya_ref[...] + gb_ref[...].astype(F32) * y_b
    x1 = x_ref[...] + jnp.dot(merged.astype(BF16), wout_ref[...], preferred_element_type=F32)
    x1_ref[...] = x1
    h2_ref[...] = _rms(x1, g_ref[...]).astype(h2_ref.dtype)


def _merge(attn, zg, gaya, x, w_proj_b, w_out, gain):
    s, d = x.shape
    tq = MIX_ROWS
    bw = attn.shape[1]
    windows = (_nbytes((tq, bw), BF16, 2) + _nbytes((tq, d), BF16, 2) + _nbytes((tq, d), F32, 4)
               + _nbytes((bw, d), BF16) + _nbytes((d, d), BF16)
               + _nbytes((tq, d), F32, 2) + _nbytes((tq, d), BF16, 2))
    return pl.pallas_call(
        _merge_kernel,
        grid=(s // tq,),
        in_specs=[pl.BlockSpec((tq, bw), lambda i: (i, 0)),
                  pl.BlockSpec((tq, d), lambda i: (i, 1)),
                  pl.BlockSpec((tq, d), lambda i: (i, 0)),
                  pl.BlockSpec((tq, d), lambda i: (i, 0)),
                  _resident((bw, d), lambda i: (0, 0)),
                  _resident((d, d), lambda i: (0, 0)),
                  _resident((1, d), lambda i: (0, 0))],
        out_specs=[pl.BlockSpec((tq, d), lambda i: (i, 0)),
                   pl.BlockSpec((tq, d), lambda i: (i, 0))],
        out_shape=[jax.ShapeDtypeStruct((s, d), F32),
                   jax.ShapeDtypeStruct((s, d), BF16)],
        compiler_params=_params(("arbitrary",), windows),
        name="merge_out_proj",
    )(attn, zg, gaya, x, w_proj_b, w_out, gain.reshape(1, d))


def _ffn_kernel(h_ref, wu_ref, wd_ref, x1_ref, g_ref, o_ref, *, final_norm):
    j = pl.program_id(1)

    @pl.when(j == 0)
    def _():
        o_ref[...] = x1_ref[...]

    a = jnp.dot(h_ref[...], wu_ref[...].astype(BF16), preferred_element_type=F32)
    a = jnp.square(jnp.maximum(a, 0.0)).astype(BF16)
    o_ref[...] += jnp.dot(a, wd_ref[...].astype(BF16), preferred_element_type=F32)

    if final_norm:
        @pl.when(j == pl.num_programs(1) - 1)
        def _():
            o_ref[...] = _rms(o_ref[...], g_ref[...])


def _ffn(h2, w_up, w_down, x1, gain, final_norm):
    s, d = x1.shape
    tm, tf = FFN_ROWS, FFN_COLS
    dff = w_up.shape[1]
    windows = _nbytes((tm, d), BF16, 2) + _nbytes((d, tf), F32, 4) + _nbytes((tm, d), F32, 4)
    return pl.pallas_call(
        functools.partial(_ffn_kernel, final_norm=final_norm),
        grid=(s // tm, dff // tf),
        in_specs=[pl.BlockSpec((tm, d), lambda i, j: (i, 0)),
                  pl.BlockSpec((d, tf), lambda i, j: (0, j)),
                  pl.BlockSpec((tf, d), lambda i, j: (j, 0)),
                  pl.BlockSpec((tm, d), lambda i, j: (i, 0)),
                  _resident((1, d), lambda i, j: (0, 0))],
        out_specs=pl.BlockSpec((tm, d), lambda i, j: (i, 0)),
        out_shape=jax.ShapeDtypeStruct((s, d), F32),
        compiler_params=_params(("arbitrary", "arbitrary"), windows),
        name="ffn_relu2",
    )(h2, w_up, w_down, x1, gain.reshape(1, d))


def kernel(x, ln_mix, w_in, a_v_gain, a_spatial, a_spatial_bias, w_proj_a, w_proj_b, w_out,
           rel_bias, ln_mlp, w_up, w_down, ln_final):
    bsz, s, d = x.shape
    depth = ln_mix.shape[0]
    aw = A_GROUPS * A_GROUP_DIM
    bw = B_HEADS * B_HEAD_DIM
    assert w_in.shape[2] == 2 * aw + 3 * bw + 2 * d
    qkv_scale = jnp.concatenate([jnp.ones((bw,), F32), jnp.full((bw,), K_SCALE, F32), jnp.ones((bw,), F32)])
    outs = []
    for b in range(bsz):
        xb = x[b]
        for l in range(depth):
            last = l == depth - 1
            zuv, h = _norm_in_proj(xb, ln_mix[l], w_in[l], 2 * aw, "gelu")
            zqkv = _in_proj(h, w_in[l], 2 * aw, 3 * bw, col_scale=qkv_scale)
            zg = _in_proj(h, w_in[l], 2 * aw + 3 * bw, 2 * d, "sigmoid")
            gaya = _gmlp(zuv, zg, a_v_gain[l], a_spatial[l], a_spatial_bias[l], w_proj_a[l])
            attn = _moba(zqkv, rel_bias)
            x1, h2 = _merge(attn, zg, gaya, xb, w_proj_b[l].astype(BF16), w_out[l].astype(BF16), ln_mlp[l])
            gain = ln_final if last else jnp.ones((d,), F32)
            xb = _ffn(h2, w_up[l], w_down[l], x1, gain, final_norm=last)
        outs.append(xb)
    return jnp.stack(outs)
```

```python
import functools
import math

import jax
import jax.numpy as jnp
from jax import lax
from jax.experimental import pallas as pl
from jax.experimental.pallas import tpu as pltpu

A_GROUPS = 16
A_GROUP_DIM = 128
A_CHUNK = 128
B_HEADS = 16
B_HEAD_DIM = 128
MOBA_BLOCK = 256
MOBA_TOPK = 3
FAR_GROUP = 2
Q_STEP_BLOCKS = 2
PAIRS_PER_ITER = 8
SCORE_SLOTS = 2
GATE_STEPS = 8
PREP_UNROLL = 8
REL_BUCKETS = 32
REL_MAX_DIST = 128
EPS = 1e-6
NEG = -1e30
LOG2E = math.log2(math.e)
NEG_LOG2 = NEG * LOG2E
FAR_MASKED = 2 * NEG_LOG2
REMOVED = -3e38
K_SCALE = (B_HEAD_DIM ** -0.5) * LOG2E
BF16_SUBLANES = 16

F32 = jnp.float32
BF16 = jnp.bfloat16

MIB = 1024 * 1024
NT_DIMS = (((1,), (1,)), ((), ()))

V7X_VMEM_MIB = 64
PROJ_ROWS = 1024
PROJ_COLS = 1024
MIX_ROWS = 512
GMLP_PARTS = 2
FFN_ROWS = 1024
FFN_COLS = 512


def _params(semantics, window_bytes):
    limit = min(window_bytes + V7X_VMEM_MIB * MIB // 4, V7X_VMEM_MIB * MIB * 15 // 16)
    return pltpu.CompilerParams(dimension_semantics=semantics, vmem_limit_bytes=limit)


def _nbytes(shape, dtype, buffers=1):
    return math.prod(shape) * jnp.dtype(dtype).itemsize * buffers


def _resident(shape, index_map):
    return pl.BlockSpec(shape, index_map, pipeline_mode=pl.Buffered(1))


def _rms(xf, gain):
    return xf * lax.rsqrt(jnp.mean(xf * xf, axis=-1, keepdims=True) + EPS) * gain


def _gelu(a):
    return 0.5 * a * (1.0 + lax.erf(a * math.sqrt(0.5)))


_EPILOGUES = {"gelu": _gelu, "sigmoid": jax.nn.sigmoid}


def _proj_kernel(h_ref, w_ref, o_ref, *, epilogue):
    acc = jnp.dot(h_ref[...], w_ref[...].astype(BF16), preferred_element_type=F32)
    o_ref[...] = _EPILOGUES[epilogue](acc).astype(o_ref.dtype)


def _proj_scale_kernel(h_ref, w_ref, scale_ref, o_ref):
    acc = jnp.dot(h_ref[...], w_ref[...].astype(BF16), preferred_element_type=F32)
    z = (acc * scale_ref[...]).astype(o_ref.dtype)
    dh = o_ref.shape[2]
    for hh in range(o_ref.shape[0]):
        o_ref[hh] = z[:, hh * dh:(hh + 1) * dh]


def _norm_proj_kernel(x_ref, g_ref, w_ref, o_ref, h_ref, *, epilogue):
    @pl.when(pl.program_id(1) == 0)
    def _():
        h_ref[...] = _rms(x_ref[...], g_ref[...]).astype(h_ref.dtype)

    acc = jnp.dot(h_ref[...], w_ref[...].astype(BF16), preferred_element_type=F32)
    o_ref[...] = _EPILOGUES[epilogue](acc).astype(o_ref.dtype)


def _norm_in_proj(x, gain, w, ncols, epilogue):
    s, d = x.shape
    tm, tn = PROJ_ROWS, PROJ_COLS
    windows = (_nbytes((tm, d), F32, 2) + _nbytes((d, tn), F32, 2) + _nbytes((tm, tn), BF16, 2)
               + _nbytes((tm, d), BF16, 2))
    return pl.pallas_call(
        functools.partial(_norm_proj_kernel, epilogue=epilogue),
        grid=(s // tm, ncols // tn),
        in_specs=[pl.BlockSpec((tm, d), lambda i, j: (i, 0)),
                  pl.BlockSpec((1, d), lambda i, j: (0, 0)),
                  pl.BlockSpec((d, tn), lambda i, j: (0, j))],
        out_specs=[pl.BlockSpec((tm, tn), lambda i, j: (i, j)),
                   pl.BlockSpec((tm, d), lambda i, j: (i, 0))],
        out_shape=[jax.ShapeDtypeStruct((s, ncols), BF16),
                   jax.ShapeDtypeStruct((s, d), BF16)],
        compiler_params=_params(("arbitrary", "arbitrary"), windows),
        name="norm_in_proj_" + epilogue,
    )(x, gain.reshape(1, d), w)


def _in_proj(h, w, col0, ncols, epilogue=None, col_scale=None):
    s, d = h.shape
    tm, tn = PROJ_ROWS, PROJ_COLS
    jb = col0 // tn
    in_specs = [pl.BlockSpec((tm, d), lambda i, j: (i, 0)),
                pl.BlockSpec((d, tn), lambda i, j: (0, j + jb))]
    if col_scale is None:
        body, args, name = functools.partial(_proj_kernel, epilogue=epilogue), (h, w), epilogue
        out_spec = pl.BlockSpec((tm, tn), lambda i, j: (i, j))
        out_shape = jax.ShapeDtypeStruct((s, ncols), BF16)
    else:
        in_specs.append(pl.BlockSpec((1, tn), lambda i, j: (0, j)))
        body, args, name = _proj_scale_kernel, (h, w, col_scale.reshape(1, ncols)), "scale"
        out_spec = pl.BlockSpec((tn // B_HEAD_DIM, tm, B_HEAD_DIM), lambda i, j: (j, i, 0))
        out_shape = jax.ShapeDtypeStruct((ncols // B_HEAD_DIM, s, B_HEAD_DIM), BF16)
    windows = _nbytes((tm, d), BF16, 2) + _nbytes((d, tn), F32, 2) + _nbytes((tm, tn), BF16, 2)
    return pl.pallas_call(
        body,
        grid=(s // tm, ncols // tn),
        in_specs=in_specs,
        out_specs=out_spec,
        out_shape=out_shape,
        compiler_params=_params(("arbitrary", "arbitrary"), windows),
        name="in_proj_" + name,
    )(*args)


def _gmlp_kernel(u_ref, v_ref, ga_ref, gain_ref, ws_ref, bt_ref, wpa_ref, o_ref, wm_ref, a_ref):
    tq = u_ref.shape[0]

    @pl.when(pl.program_id(0) == 0)
    def _():
        t = lax.broadcasted_iota(jnp.int32, (A_CHUNK, A_CHUNK), 0)
        s = lax.broadcasted_iota(jnp.int32, (A_CHUNK, A_CHUNK), 1)
        for g in range(A_GROUPS):
            wm_ref[g] = jnp.where(t >= s, ws_ref[g], 0.0).astype(BF16)

    def gate_chunk(c):
        rows = slice(c * A_CHUNK, (c + 1) * A_CHUNK)
        v = v_ref[rows, :].astype(F32)
        mu = jnp.mean(v, axis=-1, keepdims=True)
        d = v - mu
        var = jnp.mean(d * d, axis=-1, keepdims=True)
        vn = (d * lax.rsqrt(var + EPS) * gain_ref[...]).astype(BF16)
        for g in range(A_GROUPS):
            cols = slice(g * A_GROUP_DIM, (g + 1) * A_GROUP_DIM)
            sv = jnp.dot(wm_ref[g], vn[:, cols], preferred_element_type=F32) + bt_ref[:, g:g + 1]
            a_ref[rows, cols] = (u_ref[rows, cols].astype(F32) * sv).astype(BF16)

    wpa = wpa_ref[...].astype(BF16)
    n_chunks = tq // A_CHUNK
    for part in range(GMLP_PARTS):
        for c in range(part * n_chunks // GMLP_PARTS, (part + 1) * n_chunks // GMLP_PARTS):
            gate_chunk(c)
        rows = slice(part * tq // GMLP_PARTS, (part + 1) * tq // GMLP_PARTS)
        y = jnp.dot(a_ref[rows, :], wpa, preferred_element_type=F32)
        o_ref[rows, :] = ga_ref[rows, :].astype(F32) * y


def _gmlp(zuv, zg, gain, w_s, b_s, w_proj_a):
    s = zuv.shape[0]
    tq = MIX_ROWS
    aw = A_GROUPS * A_GROUP_DIM
    d = w_proj_a.shape[1]
    windows = (_nbytes((tq, aw), BF16, 4) + _nbytes((tq, d), BF16, 2) + _nbytes((tq, d), F32, 2)
               + _nbytes((aw, d), F32) + _nbytes((A_GROUPS, A_CHUNK, A_CHUNK), F32)
               + _nbytes((A_GROUPS, A_CHUNK, A_CHUNK), BF16) + _nbytes((tq, aw), BF16))
    return pl.pallas_call(
        _gmlp_kernel,
        grid=(s // tq,),
        in_specs=[pl.BlockSpec((tq, aw), lambda i: (i, 0)),
                  pl.BlockSpec((tq, aw), lambda i: (i, 1)),
                  pl.BlockSpec((tq, d), lambda i: (i, 0)),
                  _resident((1, aw), lambda i: (0, 0)),
                  _resident((A_GROUPS, A_CHUNK, A_CHUNK), lambda i: (0, 0, 0)),
                  _resident((A_CHUNK, A_GROUPS), lambda i: (0, 0)),
                  _resident((aw, d), lambda i: (0, 0))],
        out_specs=pl.BlockSpec((tq, d), lambda i: (i, 0)),
        out_shape=jax.ShapeDtypeStruct((s, d), F32),
        scratch_shapes=[pltpu.VMEM((A_GROUPS, A_CHUNK, A_CHUNK), BF16),
                        pltpu.VMEM((tq, aw), BF16)],
        compiler_params=_params(("arbitrary",), windows),
        name="gmlp_mixer",
    )(zuv, zuv, zg, gain.reshape(1, aw), w_s, b_s.T, w_proj_a)


def _t5_bias_log2(dist, rb_ref, base):
    max_exact = REL_BUCKETS // 2
    n = jnp.maximum(dist, 0)
    nf = jnp.maximum(n, 1).astype(F32)
    large = max_exact + (jnp.log(nf / max_exact) / math.log(REL_MAX_DIST / max_exact)
                         * (REL_BUCKETS - max_exact)).astype(jnp.int32)
    large = jnp.minimum(large, REL_BUCKETS - 1)
    bucket = jnp.where(n < max_exact, n, large)
    out = jnp.zeros(dist.shape, F32)
    for b in range(REL_BUCKETS):
        out = jnp.where(bucket == b, rb_ref[base + b] * LOG2E, out)
    return out


def _moba_kernel(rb_ref, q_ref, k_ref, v_ref, o_ref,
                 kmean_ref, vt_ref, tile_ref, rows_ref, m_ref, acc_ref, *slots):
    L = MOBA_BLOCK
    G = FAR_GROUP
    QB = Q_STEP_BLOCKS
    W = QB * L
    dh = B_HEAD_DIM
    nb = q_ref.shape[0] // L
    nq = nb // QB
    log2_l = L.bit_length() - 1
    base = pl.program_id(0) * REL_BUCKETS
    far_bias = rb_ref[base + REL_BUCKETS - 1] * LOG2E

    def prep(it, carry):
        for i in range(PREP_UNROLL):
            n = it * PREP_UNROLL + i
            r0 = pl.multiple_of(n * L, L)
            kmean_ref[pl.ds(n, 1), :] = jnp.mean(k_ref[pl.ds(r0, L), :].astype(F32), axis=0, keepdims=True)
            vt_ref[n, 0:dh, :] = v_ref[pl.ds(r0, L), :].astype(F32).T.astype(BF16)
            vt_ref[n, dh:, :] = jnp.ones((vt_ref.shape[1] - dh, L), BF16)
        return carry

    lax.fori_loop(0, nb // PREP_UNROLL, prep, 0)

    dist = lax.broadcasted_iota(jnp.int32, (1, 2 * L), 1) - L

    def toeplitz(row):
        return pltpu.roll(jnp.broadcast_to(row, (L, 2 * L)), L, 1, stride=1, stride_axis=0)[:, :L]

    b_prev = toeplitz(_t5_bias_log2(dist + L, rb_ref, base))
    b_own = toeplitz(jnp.where(dist >= 0, _t5_bias_log2(dist, rb_ref, base), NEG_LOG2))
    masked = jnp.full((L, L), NEG_LOG2, F32)
    zero = jnp.zeros((L, L), F32)
    kinds = [[[zero, zero], [zero, zero]], [[zero, zero], [b_prev, zero]], [[b_own, b_prev], [masked, b_own]]]
    for kind, blocks in enumerate(kinds):
        for r, tiles in enumerate(blocks):
            for c, tile in enumerate(tiles):
                tile_ref[kind, r * L:(r + 1) * L, c * L:(c + 1) * L] = tile

    kmean = kmean_ref[...].astype(BF16)

    def gate(it, carry):
        u0 = it * GATE_STEPS
        q_c = q_ref[pl.ds(pl.multiple_of(u0 * W, GATE_STEPS * W), GATE_STEPS * W), :]
        gs = lax.dot_general(kmean, q_c, NT_DIMS, preferred_element_type=F32) * (1.0 / K_SCALE)
        blk = lax.broadcasted_iota(jnp.int32, gs.shape, 0)
        own = QB * u0 + lax.shift_right_logical(lax.broadcasted_iota(jnp.int32, gs.shape, 1), log2_l)
        gs = jnp.where(blk < own, gs, NEG)
        sel = jnp.zeros(gs.shape, F32)
        for _ in range(MOBA_TOPK):
            top = jnp.max(gs, axis=0, keepdims=True)
            idx = jnp.min(jnp.where(gs == top, blk, nb), axis=0, keepdims=True)
            hit = blk == idx
            sel = jnp.where(hit, jnp.where(top > NEG / 2, 1.0, 0.0), sel)
            gs = jnp.where(hit, REMOVED, gs)
        picked = sel > 0.5
        rows = jnp.where(blk < own - 1, jnp.where(picked, far_bias, FAR_MASKED),
                         jnp.where(blk == own - 1, jnp.where(picked, 0.0, FAR_MASKED), 0.0))
        for i in range(GATE_STEPS):
            rows_ref[u0 + i] = rows[:, i * W:(i + 1) * W]
        return carry

    lax.fori_loop(0, nq // GATE_STEPS, gate, 0)

    def item(t, ua):
        first = t <= ua
        u = jnp.where(first, ua, nq - 1 - ua)
        g = jnp.where(first, t, t - ua - 1)
        return u, g, jnp.where(first, 0, 1), jnp.clip(g - (u - 2), 0, 2)

    def rows_of(u, g):
        return [rows_ref[u, pl.ds(g * G + b, 1), :] for b in range(G)]

    def produce(n, t, ua):
        u, g, _, kind = item(t, ua)
        q_u = q_ref[pl.ds(pl.multiple_of(u * W, W), W), :]
        k_g = k_ref[pl.ds(pl.multiple_of(g * (G * L), G * L), G * L), :]
        s_t = lax.dot_general(k_g, q_u, NT_DIMS, preferred_element_type=F32) + tile_ref[kind]
        slots[n % len(slots)][...] = s_t
        rows = rows_of(u, g)
        return functools.reduce(jnp.maximum, [
            jnp.max(s_t[b * L:(b + 1) * L], axis=0, keepdims=True) + rows[b] for b in range(G)])

    def consume(n, t, ua, pi, cand):
        u, g, w, _ = item(t, ua)
        rows = rows_of(u, g)
        s_ref = slots[n % len(slots)]
        st = pi * 2 + w
        m_old = m_ref[st]
        m_new = jnp.maximum(m_old, cand)
        pv = None
        for b in range(G):
            p = jnp.exp2(s_ref[b * L:(b + 1) * L, :] + (rows[b] - m_new)).astype(BF16)
            d = jnp.dot(vt_ref[g * G + b], p, preferred_element_type=F32)
            pv = d if pv is None else pv + d
        acc_ref[st] = jnp.exp2(m_old - m_new) * acc_ref[st] + pv
        m_ref[st] = m_new

    def finish(ua, pi):
        for w, u in ((0, ua), (1, nq - 1 - ua)):
            acc = acc_ref[pi * 2 + w]
            o_ref[pl.ds(pl.multiple_of(u * W, W), W), :] = (acc[:dh] / acc[dh:dh + 1]).T.astype(o_ref.dtype)

    def stream(it, carry):
        m_ref[...] = jnp.full(m_ref.shape, NEG_LOG2, F32)
        acc_ref[...] = jnp.zeros(acc_ref.shape, F32)
        seq = [(pi, t) for pi in range(PAIRS_PER_ITER) for t in range(nq + 1)]
        first_step = lambda pi: it * PAIRS_PER_ITER + pi
        cand = produce(0, seq[0][1], first_step(seq[0][0]))
        for n, (pi, t) in enumerate(seq):
            if n + 1 < len(seq):
                cand_next = produce(n + 1, seq[n + 1][1], first_step(seq[n + 1][0]))
            consume(n, t, first_step(pi), pi, cand)
            cand = cand_next
            if t == nq:
                finish(first_step(pi), pi)
        return carry

    lax.fori_loop(0, nq // 2 // PAIRS_PER_ITER, stream, 0)


def _moba(zqkv, rel_bias):
    s = zqkv.shape[1]
    dh = B_HEAD_DIM
    L = MOBA_BLOCK
    nb = s // L
    assert Q_STEP_BLOCKS == 2 and FAR_GROUP == 2 and nb % (2 * Q_STEP_BLOCKS) == 0
    assert nb % PREP_UNROLL == 0 and (nb // Q_STEP_BLOCKS) % GATE_STEPS == 0
    assert (nb // Q_STEP_BLOCKS // 2) % PAIRS_PER_ITER == 0
    w = Q_STEP_BLOCKS * L
    rb = rel_bias.T.reshape(-1).astype(F32)
    scratch = [pltpu.VMEM((nb, dh), F32),
               pltpu.VMEM((nb, dh + BF16_SUBLANES, L), BF16),
               pltpu.VMEM((3, FAR_GROUP * L, w), F32),
               pltpu.VMEM((nb // Q_STEP_BLOCKS, nb, w), F32),
               pltpu.VMEM((2 * PAIRS_PER_ITER, 1, w), F32),
               pltpu.VMEM((2 * PAIRS_PER_ITER, dh + BF16_SUBLANES, w), F32)]
    scratch += [pltpu.VMEM((FAR_GROUP * L, w), F32)] * SCORE_SLOTS
    windows = _nbytes((s, dh), BF16, 8) + sum(_nbytes(b.shape, b.dtype) for b in scratch)
    return pl.pallas_call(
        _moba_kernel,
        grid=(B_HEADS,),
        in_specs=[pl.BlockSpec(memory_space=pltpu.SMEM),
                  pl.BlockSpec((None, s, dh), lambda h: (h, 0, 0)),
                  pl.BlockSpec((None, s, dh), lambda h: (B_HEADS + h, 0, 0)),
                  pl.BlockSpec((None, s, dh), lambda h: (2 * B_HEADS + h, 0, 0))],
        out_specs=pl.BlockSpec((s, dh), lambda h: (0, h)),
        out_shape=jax.ShapeDtypeStruct((s, B_HEADS * dh), BF16),
        scratch_shapes=scratch,
        compiler_params=_params(("arbitrary",), windows),
        name="moba_attention",
    )(rb, zqkv, zqkv, zqkv)


def _merge_kernel(attn_ref, gb_ref, gaya_ref, x_ref, wpb_ref, wout_ref, g_ref, x1_ref, h2_ref):
    y_b = jnp.dot(attn_ref[...], wpb_ref[...], preferred_element_type=F32)
    merged = gaya_ref[...] + gb_ref[...].astype(F32) * y_b
    x1 = x_ref[...] + jnp.dot(merged.astype(BF16), wout_ref[...], preferred_element_type=F32)
    x1_ref[...] = x1
    h2_ref[...] = _rms(x1, g_ref[...]).astype(h2_ref.dtype)


def _merge(attn, zg, gaya, x, w_proj_b, w_out, gain):
    s, d = x.shape
    tq = MIX_ROWS
    bw = attn.shape[1]
    windows = (_nbytes((tq, bw), BF16, 2) + _nbytes((tq, d), BF16, 2) + _nbytes((tq, d), F32, 4)
               + _nbytes((bw, d), BF16) + _nbytes((d, d), BF16)
               + _nbytes((tq, d), F32, 2) + _nbytes((tq, d), BF16, 2))
    return pl.pallas_call(
        _merge_kernel,
        grid=(s // tq,),
        in_specs=[pl.BlockSpec((tq, bw), lambda i: (i, 0)),
                  pl.BlockSpec((tq, d), lambda i: (i, 1)),
                  pl.BlockSpec((tq, d), lambda i: (i, 0)),
                  pl.BlockSpec((tq, d), lambda i: (i, 0)),
                  _resident((bw, d), lambda i: (0, 0)),
                  _resident((d, d), lambda i: (0, 0)),
                  _resident((1, d), lambda i: (0, 0))],
        out_specs=[pl.BlockSpec((tq, d), lambda i: (i, 0)),
                   pl.BlockSpec((tq, d), lambda i: (i, 0))],
        out_shape=[jax.ShapeDtypeStruct((s, d), F32),
                   jax.ShapeDtypeStruct((s, d), BF16)],
        compiler_params=_params(("arbitrary",), windows),
        name="merge_out_proj",
    )(attn, zg, gaya, x, w_proj_b, w_out, gain.reshape(1, d))


def _ffn_kernel(h_ref, wu_ref, wd_ref, x1_ref, g_ref, o_ref, *, final_norm):
    j = pl.program_id(1)

    @pl.when(j == 0)
    def _():
        o_ref[...] = x1_ref[...]

    a = jnp.dot(h_ref[...], wu_ref[...].astype(BF16), preferred_element_type=F32)
    a = jnp.square(jnp.maximum(a, 0.0)).astype(BF16)
    o_ref[...] += jnp.dot(a, wd_ref[...].astype(BF16), preferred_element_type=F32)

    if final_norm:
        @pl.when(j == pl.num_programs(1) - 1)
        def _():
            o_ref[...] = _rms(o_ref[...], g_ref[...])


def _ffn(h2, w_up, w_down, x1, gain, final_norm):
    s, d = x1.shape
    tm, tf = FFN_ROWS, FFN_COLS
    dff = w_up.shape[1]
    windows = _nbytes((tm, d), BF16, 2) + _nbytes((d, tf), F32, 4) + _nbytes((tm, d), F32, 4)
    return pl.pallas_call(
        functools.partial(_ffn_kernel, final_norm=final_norm),
        grid=(s // tm, dff // tf),
        in_specs=[pl.BlockSpec((tm, d), lambda i, j: (i, 0)),
                  pl.BlockSpec((d, tf), lambda i, j: (0, j)),
                  pl.BlockSpec((tf, d), lambda i, j: (j, 0)),
                  pl.BlockSpec((tm, d), lambda i, j: (i, 0)),
                  _resident((1, d), lambda i, j: (0, 0))],
        out_specs=pl.BlockSpec((tm, d), lambda i, j: (i, 0)),
        out_shape=jax.ShapeDtypeStruct((s, d), F32),
        compiler_params=_params(("arbitrary", "arbitrary"), windows),
        name="ffn_relu2",
    )(h2, w_up, w_down, x1, gain.reshape(1, d))


def kernel(x, ln_mix, w_in, a_v_gain, a_spatial, a_spatial_bias, w_proj_a, w_proj_b, w_out,
           rel_bias, ln_mlp, w_up, w_down, ln_final):
    bsz, s, d = x.shape
    depth = ln_mix.shape[0]
    aw = A_GROUPS * A_GROUP_DIM
    bw = B_HEADS * B_HEAD_DIM
    assert w_in.shape[2] == 2 * aw + 3 * bw + 2 * d
    qkv_scale = jnp.concatenate([jnp.ones((bw,), F32), jnp.full((bw,), K_SCALE, F32), jnp.ones((bw,), F32)])
    outs = []
    for b in range(bsz):
        xb = x[b]
        for l in range(depth):
            last = l == depth - 1
            zuv, h = _norm_in_proj(xb, ln_mix[l], w_in[l], 2 * aw, "gelu")
            zqkv = _in_proj(h, w_in[l], 2 * aw, 3 * bw, col_scale=qkv_scale)
            zg = _in_proj(h, w_in[l], 2 * aw + 3 * bw, 2 * d, "sigmoid")
            gaya = _gmlp(zuv, zg, a_v_gain[l], a_spatial[l], a_spatial_bias[l], w_proj_a[l])
            attn = _moba(zqkv, rel_bias)
            x1, h2 = _merge(attn, zg, gaya, xb, w_proj_b[l].astype(BF16), w_out[l].astype(BF16), ln_mlp[l])
            gain = ln_final if last else jnp.ones((d,), F32)
            xb = _ffn(h2, w_up[l], w_down[l], x1, gain, final_norm=last)
        outs.append(xb)
    return jnp.stack(outs)
```

```python
import functools
import math

import jax
import jax.numpy as jnp
from jax import lax
from jax.experimental import pallas as pl
from jax.experimental.pallas import tpu as pltpu

A_GROUPS = 16
A_GROUP_DIM = 128
A_CHUNK = 128
B_HEADS = 16
B_HEAD_DIM = 128
MOBA_BLOCK = 256
MOBA_TOPK = 3
FAR_GROUP = 2
Q_STEP_BLOCKS = 2
PAIRS_PER_ITER = 4
SCORE_SLOTS = 2
GATE_STEPS = 8
PREP_UNROLL = 8
REL_BUCKETS = 32
REL_MAX_DIST = 128
EPS = 1e-6
NEG = -1e30
LOG2E = math.log2(math.e)
NEG_LOG2 = NEG * LOG2E
FAR_MASKED = 2 * NEG_LOG2
REMOVED = -3e38
K_SCALE = (B_HEAD_DIM ** -0.5) * LOG2E
BF16_SUBLANES = 16

F32 = jnp.float32
BF16 = jnp.bfloat16

MIB = 1024 * 1024
NT_DIMS = (((1,), (1,)), ((), ()))

V7X_VMEM_MIB = 64
PROJ_ROWS = 1024
PROJ_COLS = 1024
WEIGHT_RING = 3
MIX_ROWS = 512
GMLP_PARTS = 2
FFN_ROWS = 1024
FFN_COLS = 512


def _params(semantics, window_bytes):
    limit = min(window_bytes + V7X_VMEM_MIB * MIB // 4, V7X_VMEM_MIB * MIB * 15 // 16)
    return pltpu.CompilerParams(dimension_semantics=semantics, vmem_limit_bytes=limit)


def _nbytes(shape, dtype, buffers=1):
    return math.prod(shape) * jnp.dtype(dtype).itemsize * buffers


def _resident(shape, index_map):
    return pl.BlockSpec(shape, index_map, pipeline_mode=pl.Buffered(1))


def _rms(xf, gain):
    return xf * lax.rsqrt(jnp.mean(xf * xf, axis=-1, keepdims=True) + EPS) * gain


def _gelu(a):
    return 0.5 * a * (1.0 + lax.erf(a * math.sqrt(0.5)))


_EPILOGUES = {"gelu": _gelu, "sigmoid": jax.nn.sigmoid}


def _ring_weight_tile(w_hbm, wbuf, sem, jb):
    nj = pl.num_programs(1)
    total = pl.num_programs(0) * nj
    t = pl.program_id(0) * nj + pl.program_id(1)
    tn = wbuf.shape[2]

    def copy(step):
        col = pl.multiple_of((lax.rem(step, nj) + jb) * tn, tn)
        slot = lax.rem(step, WEIGHT_RING)
        return pltpu.make_async_copy(w_hbm.at[:, pl.ds(col, tn)], wbuf.at[slot], sem.at[slot])

    @pl.when(t == 0)
    def _():
        for step in range(WEIGHT_RING - 1):
            copy(step).start()

    @pl.when(t + (WEIGHT_RING - 1) < total)
    def _():
        copy(t + (WEIGHT_RING - 1)).start()

    copy(t).wait()
    return wbuf[lax.rem(t, WEIGHT_RING)].astype(BF16)


def _proj_kernel(h_ref, w_hbm, o_ref, wbuf, sem, *, epilogue, jb):
    acc = jnp.dot(h_ref[...], _ring_weight_tile(w_hbm, wbuf, sem, jb), preferred_element_type=F32)
    o_ref[...] = _EPILOGUES[epilogue](acc).astype(o_ref.dtype)


def _proj_scale_kernel(h_ref, w_hbm, scale_ref, o_ref, wbuf, sem, *, jb):
    acc = jnp.dot(h_ref[...], _ring_weight_tile(w_hbm, wbuf, sem, jb), preferred_element_type=F32)
    z = (acc * scale_ref[...]).astype(o_ref.dtype)
    dh = o_ref.shape[2]
    for hh in range(o_ref.shape[0]):
        o_ref[hh] = z[:, hh * dh:(hh + 1) * dh]


def _norm_proj_kernel(x_ref, g_ref, w_ref, o_ref, h_ref, *, epilogue):
    @pl.when(pl.program_id(1) == 0)
    def _():
        h_ref[...] = _rms(x_ref[...], g_ref[...]).astype(h_ref.dtype)

    acc = jnp.dot(h_ref[...], w_ref[...].astype(BF16), preferred_element_type=F32)
    o_ref[...] = _EPILOGUES[epilogue](acc).astype(o_ref.dtype)


def _norm_in_proj(x, gain, w, ncols, epilogue):
    s, d = x.shape
    tm, tn = PROJ_ROWS, PROJ_COLS
    windows = (_nbytes((tm, d), F32, 2) + _nbytes((d, tn), F32, 2) + _nbytes((tm, tn), BF16, 2)
               + _nbytes((tm, d), BF16, 2))
    return pl.pallas_call(
        functools.partial(_norm_proj_kernel, epilogue=epilogue),
        grid=(s // tm, ncols // tn),
        in_specs=[pl.BlockSpec((tm, d), lambda i, j: (i, 0)),
                  pl.BlockSpec((1, d), lambda i, j: (0, 0)),
                  pl.BlockSpec((d, tn), lambda i, j: (0, j))],
        out_specs=[pl.BlockSpec((tm, tn), lambda i, j: (i, j)),
                   pl.BlockSpec((tm, d), lambda i, j: (i, 0))],
        out_shape=[jax.ShapeDtypeStruct((s, ncols), BF16),
                   jax.ShapeDtypeStruct((s, d), BF16)],
        compiler_params=_params(("arbitrary", "arbitrary"), windows),
        name="norm_in_proj_" + epilogue,
    )(x, gain.reshape(1, d), w)


def _in_proj(h, w, col0, ncols, epilogue=None, col_scale=None):
    s, d = h.shape
    tm, tn = PROJ_ROWS, PROJ_COLS
    jb = col0 // tn
    in_specs = [pl.BlockSpec((tm, d), lambda i, j: (i, 0)),
                pl.BlockSpec(memory_space=pl.ANY)]
    if col_scale is None:
        body, args, name = functools.partial(_proj_kernel, epilogue=epilogue, jb=jb), (h, w), epilogue
        out_spec = pl.BlockSpec((tm, tn), lambda i, j: (i, j))
        out_shape = jax.ShapeDtypeStruct((s, ncols), BF16)
    else:
        in_specs.append(pl.BlockSpec((1, tn), lambda i, j: (0, j)))
        body = functools.partial(_proj_scale_kernel, jb=jb)
        args, name = (h, w, col_scale.reshape(1, ncols)), "scale"
        out_spec = pl.BlockSpec((tn // B_HEAD_DIM, tm, B_HEAD_DIM), lambda i, j: (j, i, 0))
        out_shape = jax.ShapeDtypeStruct((ncols // B_HEAD_DIM, s, B_HEAD_DIM), BF16)
    assert (s // tm) * (ncols // tn) >= WEIGHT_RING
    windows = _nbytes((tm, d), BF16, 2) + _nbytes((d, tn), F32, WEIGHT_RING) + _nbytes((tm, tn), BF16, 2)
    return pl.pallas_call(
        body,
        grid=(s // tm, ncols // tn),
        in_specs=in_specs,
        out_specs=out_spec,
        out_shape=out_shape,
        scratch_shapes=[pltpu.VMEM((WEIGHT_RING, d, tn), F32), pltpu.SemaphoreType.DMA((WEIGHT_RING,))],
        compiler_params=_params(("arbitrary", "arbitrary"), windows),
        name="in_proj_" + name,
    )(*args)


def _gmlp_kernel(u_ref, v_ref, ga_ref, gain_ref, ws_ref, bt_ref, wpa_ref, o_ref, wm_ref, a_ref):
    tq = u_ref.shape[0]

    @pl.when(pl.program_id(0) == 0)
    def _():
        t = lax.broadcasted_iota(jnp.int32, (A_CHUNK, A_CHUNK), 0)
        s = lax.broadcasted_iota(jnp.int32, (A_CHUNK, A_CHUNK), 1)
        for g in range(A_GROUPS):
            wm_ref[g] = jnp.where(t >= s, ws_ref[g], 0.0).astype(BF16)

    def gate_chunk(c):
        rows = slice(c * A_CHUNK, (c + 1) * A_CHUNK)
        v = v_ref[rows, :].astype(F32)
        mu = jnp.mean(v, axis=-1, keepdims=True)
        d = v - mu
        var = jnp.mean(d * d, axis=-1, keepdims=True)
        vn = (d * lax.rsqrt(var + EPS) * gain_ref[...]).astype(BF16)
        for g in range(A_GROUPS):
            cols = slice(g * A_GROUP_DIM, (g + 1) * A_GROUP_DIM)
            sv = jnp.dot(wm_ref[g], vn[:, cols], preferred_element_type=F32) + bt_ref[:, g:g + 1]
            a_ref[rows, cols] = (u_ref[rows, cols].astype(F32) * sv).astype(BF16)

    wpa = wpa_ref[...].astype(BF16)
    n_chunks = tq // A_CHUNK
    for part in range(GMLP_PARTS):
        for c in range(part * n_chunks // GMLP_PARTS, (part + 1) * n_chunks // GMLP_PARTS):
            gate_chunk(c)
        rows = slice(part * tq // GMLP_PARTS, (part + 1) * tq // GMLP_PARTS)
        y = jnp.dot(a_ref[rows, :], wpa, preferred_element_type=F32)
        o_ref[rows, :] = ga_ref[rows, :].astype(F32) * y


def _gmlp(zuv, zg, gain, w_s, b_s, w_proj_a):
    s = zuv.shape[0]
    tq = MIX_ROWS
    aw = A_GROUPS * A_GROUP_DIM
    d = w_proj_a.shape[1]
    windows = (_nbytes((tq, aw), BF16, 4) + _nbytes((tq, d), BF16, 2) + _nbytes((tq, d), F32, 2)
               + _nbytes((aw, d), F32) + _nbytes((A_GROUPS, A_CHUNK, A_CHUNK), F32)
               + _nbytes((A_GROUPS, A_CHUNK, A_CHUNK), BF16) + _nbytes((tq, aw), BF16))
    return pl.pallas_call(
        _gmlp_kernel,
        grid=(s // tq,),
        in_specs=[pl.BlockSpec((tq, aw), lambda i: (i, 0)),
                  pl.BlockSpec((tq, aw), lambda i: (i, 1)),
                  pl.BlockSpec((tq, d), lambda i: (i, 0)),
                  _resident((1, aw), lambda i: (0, 0)),
                  _resident((A_GROUPS, A_CHUNK, A_CHUNK), lambda i: (0, 0, 0)),
                  _resident((A_CHUNK, A_GROUPS), lambda i: (0, 0)),
                  _resident((aw, d), lambda i: (0, 0))],
        out_specs=pl.BlockSpec((tq, d), lambda i: (i, 0)),
        out_shape=jax.ShapeDtypeStruct((s, d), F32),
        scratch_shapes=[pltpu.VMEM((A_GROUPS, A_CHUNK, A_CHUNK), BF16),
                        pltpu.VMEM((tq, aw), BF16)],
        compiler_params=_params(("arbitrary",), windows),
        name="gmlp_mixer",
    )(zuv, zuv, zg, gain.reshape(1, aw), w_s, b_s.T, w_proj_a)


def _t5_bias_log2(dist, rb_ref, base):
    max_exact = REL_BUCKETS // 2
    n = jnp.maximum(dist, 0)
    nf = jnp.maximum(n, 1).astype(F32)
    large = max_exact + (jnp.log(nf / max_exact) / math.log(REL_MAX_DIST / max_exact)
                         * (REL_BUCKETS - max_exact)).astype(jnp.int32)
    large = jnp.minimum(large, REL_BUCKETS - 1)
    bucket = jnp.where(n < max_exact, n, large)
    out = jnp.zeros(dist.shape, F32)
    for b in range(REL_BUCKETS):
        out = jnp.where(bucket == b, rb_ref[base + b] * LOG2E, out)
    return out


def _moba_kernel(rb_ref, q_ref, k_ref, v_ref, o_ref,
                 kmean_ref, vt_ref, tile_ref, rows_ref, m_ref, acc_ref, *slots):
    L = MOBA_BLOCK
    G = FAR_GROUP
    QB = Q_STEP_BLOCKS
    W = QB * L
    dh = B_HEAD_DIM
    nb = q_ref.shape[0] // L
    nq = nb // QB
    log2_l = L.bit_length() - 1
    base = pl.program_id(0) * REL_BUCKETS
    far_bias = rb_ref[base + REL_BUCKETS - 1] * LOG2E

    def prep(it, carry):
        for i in range(PREP_UNROLL):
            n = it * PREP_UNROLL + i
            r0 = pl.multiple_of(n * L, L)
            kmean_ref[pl.ds(n, 1), :] = jnp.mean(k_ref[pl.ds(r0, L), :].astype(F32), axis=0, keepdims=True)
            vt_ref[n, 0:dh, :] = v_ref[pl.ds(r0, L), :].astype(F32).T.astype(BF16)
            vt_ref[n, dh:, :] = jnp.ones((vt_ref.shape[1] - dh, L), BF16)
        return carry

    lax.fori_loop(0, nb // PREP_UNROLL, prep, 0)

    dist = lax.broadcasted_iota(jnp.int32, (1, 2 * L), 1) - L

    def toeplitz(row):
        return pltpu.roll(jnp.broadcast_to(row, (L, 2 * L)), L, 1, stride=1, stride_axis=0)[:, :L]

    b_prev = toeplitz(_t5_bias_log2(dist + L, rb_ref, base))
    b_own = toeplitz(jnp.where(dist >= 0, _t5_bias_log2(dist, rb_ref, base), NEG_LOG2))
    masked = jnp.full((L, L), NEG_LOG2, F32)
    zero = jnp.zeros((L, L), F32)
    kinds = [[[zero, zero], [zero, zero]], [[zero, zero], [b_prev, zero]], [[b_own, b_prev], [masked, b_own]]]
    for kind, blocks in enumerate(kinds):
        for r, tiles in enumerate(blocks):
            for c, tile in enumerate(tiles):
                tile_ref[kind, r * L:(r + 1) * L, c * L:(c + 1) * L] = tile

    kmean = kmean_ref[...].astype(BF16)

    def gate(it, carry):
        u0 = it * GATE_STEPS
        q_c = q_ref[pl.ds(pl.multiple_of(u0 * W, GATE_STEPS * W), GATE_STEPS * W), :]
        gs = lax.dot_general(kmean, q_c, NT_DIMS, preferred_element_type=F32) * (1.0 / K_SCALE)
        blk = lax.broadcasted_iota(jnp.int32, gs.shape, 0)
        own = QB * u0 + lax.shift_right_logical(lax.broadcasted_iota(jnp.int32, gs.shape, 1), log2_l)
        gs = jnp.where(blk < own, gs, NEG)
        sel = jnp.zeros(gs.shape, F32)
        for _ in range(MOBA_TOPK):
            top = jnp.max(gs, axis=0, keepdims=True)
            idx = jnp.min(jnp.where(gs == top, blk, nb), axis=0, keepdims=True)
            hit = blk == idx
            sel = jnp.where(hit, jnp.where(top > NEG / 2, 1.0, 0.0), sel)
            gs = jnp.where(hit, REMOVED, gs)
        picked = sel > 0.5
        rows = jnp.where(blk < own - 1, jnp.where(picked, far_bias, FAR_MASKED),
                         jnp.where(blk == own - 1, jnp.where(picked, 0.0, FAR_MASKED), 0.0))
        for i in range(GATE_STEPS):
            rows_ref[u0 + i] = rows[:, i * W:(i + 1) * W]
        return carry

    lax.fori_loop(0, nq // GATE_STEPS, gate, 0)

    def item(t, ua):
        first = t <= ua
        u = jnp.where(first, ua, nq - 1 - ua)
        g = jnp.where(first, t, t - ua - 1)
        return u, g, jnp.where(first, 0, 1), jnp.clip(g - (u - 2), 0, 2)

    def rows_of(u, g):
        return [rows_ref[u, pl.ds(g * G + b, 1), :] for b in range(G)]

    def produce(n, t, ua):
        u, g, _, kind = item(t, ua)
        q_u = q_ref[pl.ds(pl.multiple_of(u * W, W), W), :]
        k_g = k_ref[pl.ds(pl.multiple_of(g * (G * L), G * L), G * L), :]
        s_t = lax.dot_general(k_g, q_u, NT_DIMS, preferred_element_type=F32) + tile_ref[kind]
        slots[n % len(slots)][...] = s_t
        rows = rows_of(u, g)
        return functools.reduce(jnp.maximum, [
            jnp.max(s_t[b * L:(b + 1) * L], axis=0, keepdims=True) + rows[b] for b in range(G)])

    def consume(n, t, ua, pi, cand):
        u, g, w, _ = item(t, ua)
        rows = rows_of(u, g)
        s_ref = slots[n % len(slots)]
        st = pi * 2 + w
        m_old = m_ref[st]
        m_new = jnp.maximum(m_old, cand)
        pv = None
        for b in range(G):
            p = jnp.exp2(s_ref[b * L:(b + 1) * L, :] + (rows[b] - m_new)).astype(BF16)
            d = jnp.dot(vt_ref[g * G + b], p, preferred_element_type=F32)
            pv = d if pv is None else pv + d
        acc_ref[st] = jnp.exp2(m_old - m_new) * acc_ref[st] + pv
        m_ref[st] = m_new

    def finish(ua, pi):
        for w, u in ((0, ua), (1, nq - 1 - ua)):
            acc = acc_ref[pi * 2 + w]
            o_ref[pl.ds(pl.multiple_of(u * W, W), W), :] = (acc[:dh] / acc[dh:dh + 1]).T.astype(o_ref.dtype)

    def stream(it, carry):
        m_ref[...] = jnp.full(m_ref.shape, NEG_LOG2, F32)
        acc_ref[...] = jnp.zeros(acc_ref.shape, F32)
        seq = [(pi, t) for pi in range(PAIRS_PER_ITER) for t in range(nq + 1)]
        first_step = lambda pi: it * PAIRS_PER_ITER + pi
        cand = produce(0, seq[0][1], first_step(seq[0][0]))
        for n, (pi, t) in enumerate(seq):
            if n + 1 < len(seq):
                cand_next = produce(n + 1, seq[n + 1][1], first_step(seq[n + 1][0]))
            consume(n, t, first_step(pi), pi, cand)
            cand = cand_next
            if t == nq:
                finish(first_step(pi), pi)
        return carry

    lax.fori_loop(0, nq // 2 // PAIRS_PER_ITER, stream, 0)


def _moba(zqkv, rel_bias):
    s = zqkv.shape[1]
    dh = B_HEAD_DIM
    L = MOBA_BLOCK
    nb = s // L
    assert Q_STEP_BLOCKS == 2 and FAR_GROUP == 2 and nb % (2 * Q_STEP_BLOCKS) == 0
    assert nb % PREP_UNROLL == 0 and (nb // Q_STEP_BLOCKS) % GATE_STEPS == 0
    assert (nb // Q_STEP_BLOCKS // 2) % PAIRS_PER_ITER == 0
    w = Q_STEP_BLOCKS * L
    rb = rel_bias.T.reshape(-1).astype(F32)
    scratch = [pltpu.VMEM((nb, dh), F32),
               pltpu.VMEM((nb, dh + BF16_SUBLANES, L), BF16),
               pltpu.VMEM((3, FAR_GROUP * L, w), F32),
               pltpu.VMEM((nb // Q_STEP_BLOCKS, nb, w), F32),
               pltpu.VMEM((2 * PAIRS_PER_ITER, 1, w), F32),
               pltpu.VMEM((2 * PAIRS_PER_ITER, dh + BF16_SUBLANES, w), F32)]
    scratch += [pltpu.VMEM((FAR_GROUP * L, w), F32)] * SCORE_SLOTS
    windows = _nbytes((s, dh), BF16, 8) + sum(_nbytes(b.shape, b.dtype) for b in scratch)
    return pl.pallas_call(
        _moba_kernel,
        grid=(B_HEADS,),
        in_specs=[pl.BlockSpec(memory_space=pltpu.SMEM),
                  pl.BlockSpec((None, s, dh), lambda h: (h, 0, 0)),
                  pl.BlockSpec((None, s, dh), lambda h: (B_HEADS + h, 0, 0)),
                  pl.BlockSpec((None, s, dh), lambda h: (2 * B_HEADS + h, 0, 0))],
        out_specs=pl.BlockSpec((s, dh), lambda h: (0, h)),
        out_shape=jax.ShapeDtypeStruct((s, B_HEADS * dh), BF16),
        scratch_shapes=scratch,
        compiler_params=_params(("arbitrary",), windows),
        name="moba_attention",
    )(rb, zqkv, zqkv, zqkv)


def _merge_kernel(attn_ref, gb_ref, gaya_ref, x_ref, wpb_ref, wout_ref, g_ref, x1_ref, h2_ref):
    y_b = jnp.dot(attn_ref[...], wpb_ref[...], preferred_element_type=F32)
    merged = gaya_ref[...] + gb_ref[...].astype(F32) * y_b
    x1 = x_ref[...] + jnp.dot(merged.astype(BF16), wout_ref[...], preferred_element_type=F32)
    x1_ref[...] = x1
    h2_ref[...] = _rms(x1, g_ref[...]).astype(h2_ref.dtype)


def _merge(attn, zg, gaya, x, w_proj_b, w_out, gain):
    s, d = x.shape
    tq = MIX_ROWS
    bw = attn.shape[1]
    windows = (_nbytes((tq, bw), BF16, 2) + _nbytes((tq, d), BF16, 2) + _nbytes((tq, d), F32, 4)
               + _nbytes((bw, d), BF16) + _nbytes((d, d), BF16)
               + _nbytes((tq, d), F32, 2) + _nbytes((tq, d), BF16, 2))
    return pl.pallas_call(
        _merge_kernel,
        grid=(s // tq,),
        in_specs=[pl.BlockSpec((tq, bw), lambda i: (i, 0)),
                  pl.BlockSpec((tq, d), lambda i: (i, 1)),
                  pl.BlockSpec((tq, d), lambda i: (i, 0)),
                  pl.BlockSpec((tq, d), lambda i: (i, 0)),
                  _resident((bw, d), lambda i: (0, 0)),
                  _resident((d, d), lambda i: (0, 0)),
                  _resident((1, d), lambda i: (0, 0))],
        out_specs=[pl.BlockSpec((tq, d), lambda i: (i, 0)),
                   pl.BlockSpec((tq, d), lambda i: (i, 0))],
        out_shape=[jax.ShapeDtypeStruct((s, d), F32),
                   jax.ShapeDtypeStruct((s, d), BF16)],
        compiler_params=_params(("arbitrary",), windows),
        name="merge_out_proj",
    )(attn, zg, gaya, x, w_proj_b, w_out, gain.reshape(1, d))


def _ffn_kernel(h_ref, wu_ref, wd_ref, x1_ref, g_ref, o_ref, *, final_norm):
    j = pl.program_id(1)

    @pl.when(j == 0)
    def _():
        o_ref[...] = x1_ref[...]

    a = jnp.dot(h_ref[...], wu_ref[...].astype(BF16), preferred_element_type=F32)
    a = jnp.square(jnp.maximum(a, 0.0)).astype(BF16)
    o_ref[...] += jnp.dot(a, wd_ref[...].astype(BF16), preferred_element_type=F32)

    if final_norm:
        @pl.when(j == pl.num_programs(1) - 1)
        def _():
            o_ref[...] = _rms(o_ref[...], g_ref[...])


def _ffn(h2, w_up, w_down, x1, gain, final_norm):
    s, d = x1.shape
    tm, tf = FFN_ROWS, FFN_COLS
    dff = w_up.shape[1]
    windows = _nbytes((tm, d), BF16, 2) + _nbytes((d, tf), F32, 4) + _nbytes((tm, d), F32, 4)
    return pl.pallas_call(
        functools.partial(_ffn_kernel, final_norm=final_norm),
        grid=(s // tm, dff // tf),
        in_specs=[pl.BlockSpec((tm, d), lambda i, j: (i, 0)),
                  pl.BlockSpec((d, tf), lambda i, j: (0, j)),
                  pl.BlockSpec((tf, d), lambda i, j: (j, 0)),
                  pl.BlockSpec((tm, d), lambda i, j: (i, 0)),
                  _resident((1, d), lambda i, j: (0, 0))],
        out_specs=pl.BlockSpec((tm, d), lambda i, j: (i, 0)),
        out_shape=jax.ShapeDtypeStruct((s, d), F32),
        compiler_params=_params(("arbitrary", "arbitrary"), windows),
        name="ffn_relu2",
    )(h2, w_up, w_down, x1, gain.reshape(1, d))


def kernel(x, ln_mix, w_in, a_v_gain, a_spatial, a_spatial_bias, w_proj_a, w_proj_b, w_out,
           rel_bias, ln_mlp, w_up, w_down, ln_final):
    bsz, s, d = x.shape
    depth = ln_mix.shape[0]
    aw = A_GROUPS * A_GROUP_DIM
    bw = B_HEADS * B_HEAD_DIM
    assert w_in.shape[2] == 2 * aw + 3 * bw + 2 * d
    qkv_scale = jnp.concatenate([jnp.ones((bw,), F32), jnp.full((bw,), K_SCALE, F32), jnp.ones((bw,), F32)])
    outs = []
    for b in range(bsz):
        xb = x[b]
        for l in range(depth):
            last = l == depth - 1
            zuv, h = _norm_in_proj(xb, ln_mix[l], w_in[l], 2 * aw, "gelu")
            zqkv = _in_proj(h, w_in[l], 2 * aw, 3 * bw, col_scale=qkv_scale)
            zg = _in_proj(h, w_in[l], 2 * aw + 3 * bw, 2 * d, "sigmoid")
            gaya = _gmlp(zuv, zg, a_v_gain[l], a_spatial[l], a_spatial_bias[l], w_proj_a[l])
            attn = _moba(zqkv, rel_bias)
            x1, h2 = _merge(attn, zg, gaya, xb, w_proj_b[l].astype(BF16), w_out[l].astype(BF16), ln_mlp[l])
            gain = ln_final if last else jnp.ones((d,), F32)
            xb = _ffn(h2, w_up[l], w_down[l], x1, gain, final_norm=last)
        outs.append(xb)
    return jnp.stack(outs)
```
